```python
import jax, jax.numpy as jnp
from jax import lax
import numpy as np

D_MODEL = 2048
BATCH = 4
SEQ = 2048
DEPTH = 1
DEC_BATCH = 128
DEC_SEQ = 1
PAST_LEN = 16384
PAGE_SIZE = 128

PLE_DIM = 256
GLA_HEADS = 4
GLA_DK = D_MODEL // 2 // GLA_HEADS
GLA_DV = D_MODEL // GLA_HEADS
GLA_RANK = 16
GLA_TAU = 16.0
RET_HEADS = 8
RET_DK = D_MODEL // RET_HEADS
RET_DV = D_MODEL // RET_HEADS
ROPE_BASE = 10000.0
CHUNK = 64
EPS = 1e-6

GLA_QK = GLA_HEADS * GLA_DK
GLA_V = GLA_HEADS * GLA_DV
RET_QK = RET_HEADS * RET_DK
RET_V = RET_HEADS * RET_DV
IN_SPLITS = (GLA_QK, GLA_QK, GLA_V, GLA_V, GLA_RANK, RET_QK, RET_QK, RET_V, RET_V, D_MODEL, D_MODEL)
N_IN = GLA_QK * 2 + GLA_V * 2 + GLA_RANK + RET_QK * 2 + RET_V * 2 + D_MODEL * 2

kernel_name = "gla_retnet_parallel_gated_decode_step"


def rmsnorm(x, g=None):
    xf = x.astype(jnp.float32)
    y = xf * lax.rsqrt(jnp.mean(xf * xf, axis=-1, keepdims=True) + EPS)
    if g is not None:
        y = y * g.astype(jnp.float32)
    return y.astype(x.dtype)


def rotary(x, pos):
    half = x.shape[-1] // 2
    inv = 1.0 / (ROPE_BASE ** jnp.linspace(0.0, 1.0, half, dtype=jnp.float32))
    ang = pos[:, None] * inv[None, :]
    cos = jnp.cos(ang)[None, :, None, :]
    sin = jnp.sin(ang)[None, :, None, :]
    xf = x.astype(jnp.float32)
    x1, x2 = xf[..., :half], xf[..., half:]
    return jnp.concatenate([x1 * cos - x2 * sin, x1 * sin + x2 * cos], axis=-1).astype(x.dtype)


def chunked_linear_recurrence(q, k, v, log_a, state):
    B, L, H, dk = q.shape
    dv = v.shape[-1]
    da = log_a.shape[-1]
    c = min(CHUNK, L)
    n = -(-L // c)
    pad = n * c - L

    def blocks(t):
        t = jnp.pad(t.astype(jnp.float32), ((0, 0), (0, pad), (0, 0), (0, 0)))
        return t.reshape(B, n, c, H, t.shape[-1]).transpose(1, 0, 3, 2, 4)

    qs, ks, vs, als = blocks(q), blocks(k), blocks(v), blocks(log_a)
    causal = jnp.tril(jnp.ones((c, c), dtype=bool))

    def step(S, blk):
        qc, kc, vc, ac = blk
        b = jnp.cumsum(ac, axis=2)
        diff = jnp.where(causal[:, :, None], b[:, :, :, None, :] - b[:, :, None, :, :], -jnp.inf)
        decay = jnp.exp(diff)
        if da == 1:
            A = jnp.einsum('bhtk,bhsk->bhts', qc, kc) * decay[..., 0]
        else:
            A = jnp.einsum('bhtk,bhsk,bhtsk->bhts', qc, kc, decay)
        o = (jnp.einsum('bhtk,bhkv->bhtv', qc * jnp.exp(b), S)
             + jnp.einsum('bhts,bhsv->bhtv', A, vc))
        b_last = b[:, :, -1:, :]
        S = (jnp.exp(b_last[:, :, 0, :, None]) * S
             + jnp.einsum('bhsk,bhsv->bhkv', kc * jnp.exp(b_last - b), vc))
        return S, o

    S, o = lax.scan(step, state.astype(jnp.float32), (qs, ks, vs, als))
    o = o.transpose(1, 0, 3, 2, 4).reshape(B, n * c, H, dv)[:, :L]
    return o.astype(v.dtype), S.astype(state.dtype)


def hybrid_layer(x, p, pos, st_gla, st_ret, norm_mix, w_in, w_gla_up, b_gla, gla_norm,
                 w_out, norm_ple, w_ple_gate, w_ple_proj):
    B, L, _ = x.shape
    u = rmsnorm(x, norm_mix)
    z = u @ w_in
    offs = np.cumsum(IN_SPLITS)[:-1].tolist()
    q_a, k_a, v_a, g_a, r_a, q_b, k_b, v_b, g_b, m_a, m_b = jnp.split(z, offs, axis=-1)

    q_a = q_a.reshape(B, L, GLA_HEADS, GLA_DK) * (GLA_DK ** -0.5)
    k_a = k_a.reshape(B, L, GLA_HEADS, GLA_DK)
    v_a = v_a.reshape(B, L, GLA_HEADS, GLA_DV)
    log_alpha = jax.nn.log_sigmoid((r_a @ w_gla_up + b_gla).astype(jnp.float32)) / GLA_TAU
    log_alpha = log_alpha.reshape(B, L, GLA_HEADS, GLA_DK)
    o_a, new_gla = chunked_linear_recurrence(q_a, k_a, v_a, log_alpha, st_gla)
    o_a = rmsnorm(o_a, gla_norm).reshape(B, L, GLA_V) * jax.nn.silu(g_a)

    q_b = rotary(q_b.reshape(B, L, RET_HEADS, RET_DK), pos)
    k_b = rotary(k_b.reshape(B, L, RET_HEADS, RET_DK), pos) * (RET_DK ** -0.5)
    v_b = v_b.reshape(B, L, RET_HEADS, RET_DV)
    log_gamma = jnp.log(1.0 - jnp.exp2(-5.0 - jnp.arange(RET_HEADS, dtype=jnp.float32)))
    log_g = jnp.broadcast_to(log_gamma[None, None, :, None], (B, L, RET_HEADS, 1))
    o_b, new_ret = chunked_linear_recurrence(q_b, k_b, v_b, log_g, st_ret)
    o_b = rmsnorm(o_b).reshape(B, L, RET_V) * jax.nn.silu(g_b)

    merged = jax.nn.sigmoid(m_a) * o_a + jax.nn.sigmoid(m_b) * o_b
    h = x + merged @ w_out

    gate = jax.nn.sigmoid(rmsnorm(h, norm_ple) @ w_ple_gate)
    h = h + gate * (p @ w_ple_proj)
    return h, new_gla, new_ret


def setup_inputs(seed: int = 0) -> dict:
    key = jax.random.key(seed)
    ks = jax.random.split(key, 16)

    def nrm(k, shape, s):
        return jax.random.normal(k, shape, jnp.float32) * s

    return {
        'x_prompt': nrm(ks[0], (BATCH, SEQ, D_MODEL), 1.0),
        'x_sample': nrm(ks[1], (DEC_BATCH, DEC_SEQ, D_MODEL), 1.0),
        'state_gla': nrm(ks[2], (DEPTH, DEC_BATCH, GLA_HEADS, GLA_DK, GLA_DV), 0.5),
        'state_ret': nrm(ks[3], (DEPTH, DEC_BATCH, RET_HEADS, RET_DK, RET_DV), 0.5),
        'p_prompt': nrm(ks[4], (DEPTH, BATCH, SEQ, PLE_DIM), 1.0),
        'p_sample': nrm(ks[5], (DEPTH, DEC_BATCH, DEC_SEQ, PLE_DIM), 1.0),
        'norm_mix': 1.0 + nrm(ks[6], (DEPTH, D_MODEL), 0.02),
        'w_in': nrm(ks[7], (DEPTH, D_MODEL, N_IN), D_MODEL ** -0.5),
        'w_gla_up': nrm(ks[8], (DEPTH, GLA_RANK, GLA_QK), GLA_RANK ** -0.5),
        'b_gla': nrm(ks[9], (DEPTH, GLA_QK), 0.1),
        'gla_norm': 1.0 + nrm(ks[10], (DEPTH, GLA_DV), 0.02),
        'w_out': nrm(ks[11], (DEPTH, D_MODEL, D_MODEL), D_MODEL ** -0.5),
        'norm_ple': 1.0 + nrm(ks[12], (DEPTH, D_MODEL), 0.02),
        'w_ple_gate': nrm(ks[13], (DEPTH, D_MODEL, D_MODEL), D_MODEL ** -0.5),
        'w_ple_proj': nrm(ks[14], (DEPTH, PLE_DIM, D_MODEL), PLE_DIM ** -0.5),
        'norm_final': 1.0 + nrm(ks[15], (D_MODEL,), 0.02),
    }


def reference(x_prompt, x_sample, state_gla, state_ret, p_prompt, p_sample, norm_mix, w_in,
              w_gla_up, b_gla, gla_norm, w_out, norm_ple, w_ple_gate, w_ple_proj, norm_final):
    Bp, Lp, _ = x_prompt.shape
    Bs, Ls, _ = x_sample.shape
    pos_prompt = jnp.arange(Lp, dtype=jnp.float32)
    pos_sample = PAST_LEN + jnp.arange(Ls, dtype=jnp.float32)
    hp, hs = x_prompt, x_sample
    gla_p, ret_p, gla_s, ret_s = [], [], [], []
    for i in range(DEPTH):
        lw = (norm_mix[i], w_in[i], w_gla_up[i], b_gla[i], gla_norm[i], w_out[i],
              norm_ple[i], w_ple_gate[i], w_ple_proj[i])
        z_gla = jnp.zeros((Bp, GLA_HEADS, GLA_DK, GLA_DV), x_prompt.dtype)
        z_ret = jnp.zeros((Bp, RET_HEADS, RET_DK, RET_DV), x_prompt.dtype)
        hp, sg, sr = hybrid_layer(hp, p_prompt[i], pos_prompt, z_gla, z_ret, *lw)
        gla_p.append(sg)
        ret_p.append(sr)
        hs, sg, sr = hybrid_layer(hs, p_sample[i], pos_sample, state_gla[i], state_ret[i], *lw)
        gla_s.append(sg)
        ret_s.append(sr)
    y_prompt = rmsnorm(hp, norm_final)
    y_sample = rmsnorm(hs, norm_final)
    return (y_prompt, y_sample, jnp.stack(gla_p), jnp.stack(ret_p), jnp.stack(gla_s), jnp.stack(ret_s))
```

```python
import functools

import jax
import jax.numpy as jnp
import numpy as np
from jax import lax
from jax.experimental import pallas as pl
from jax.experimental.pallas import tpu as pltpu

F32 = jnp.float32
BF16 = jnp.bfloat16

D_MODEL = 2048
PAST_LEN = 16384
PLE_DIM = 256
GLA_HEADS = 4
GLA_DK = 256
GLA_DV = 512
GLA_RANK = 16
GLA_TAU = 16.0
RET_HEADS = 8
RET_DK = 256
RET_DV = 256
ROPE_BASE = 10000.0
EPS = 1e-6

GLA_QK = GLA_HEADS * GLA_DK
GLA_V = GLA_HEADS * GLA_DV
RET_QK = RET_HEADS * RET_DK
RET_V = RET_HEADS * RET_DV
IN_SPLITS = (GLA_QK, GLA_QK, GLA_V, GLA_V, GLA_RANK, RET_QK, RET_QK, RET_V, RET_V, D_MODEL, D_MODEL)
IN_OFFS = tuple(int(v) for v in np.concatenate([[0], np.cumsum(IN_SPLITS)[:-1]]))

N_GROUPS = GLA_HEADS
RET_PER_GROUP = RET_HEADS // N_GROUPS
GW = GLA_DV
LANE = 128
O_QA = 0
O_KA = O_QA + GLA_DK
O_VA = O_KA + GLA_DK
O_GA = O_VA + GLA_DV
O_MA = O_GA + GLA_DV
O_QB = O_MA + GW
O_KB = O_QB + RET_PER_GROUP * RET_DK
O_VB = O_KB + RET_PER_GROUP * RET_DK
O_GB = O_VB + RET_PER_GROUP * RET_DV
O_MB = O_GB + RET_PER_GROUP * RET_DV
O_R = O_MB + GW
SLAB = O_R + LANE
ZW = O_R

T_BLK = 256
C_GLA = 64
TM_OUT = 512
VMEM_LIMIT = 56 * 1024 * 1024


def _rms(x):
    return x * lax.rsqrt(jnp.mean(x * x, axis=-1, keepdims=True) + EPS)


def _sigmoid(x):
    return 1.0 / (1.0 + jnp.exp(-x))


def _log_sigmoid(x):
    return jnp.minimum(x, 0.0) - jnp.log(1.0 + jnp.exp(-jnp.abs(x)))


def _dot(a, b):
    return jnp.dot(a, b, preferred_element_type=F32)


def _dot_nt(a, b):
    return lax.dot_general(a, b, (((1,), (1,)), ((), ())), preferred_element_type=F32)


def _dot_tn(a, b):
    return lax.dot_general(a, b, (((0,), (0,)), ((), ())), preferred_element_type=F32)


def _rotary(x, cos, sin):
    half = x.shape[-1] // 2
    x1, x2 = x[:, :half], x[:, half:]
    return jnp.concatenate([x1 * cos - x2 * sin, x1 * sin + x2 * cos], axis=-1)


def _gla_log_alpha(r, wup_ref, bg_ref):
    pre = _dot(r.astype(BF16), wup_ref[0]) + bg_ref[0]
    return _log_sigmoid(pre) * (1.0 / GLA_TAU)


def _mix_prompt_kernel(lg_ref, x_ref, nmix_ref, w_ref, wup_ref, bg_ref, gn_ref, cos_ref, sin_ref,
                       merged_ref, sgla_ref, sret_ref, sgt_ref):
    g = pl.program_id(0)
    t = pl.program_id(2)
    T = x_ref.shape[1]

    @pl.when(t == 0)
    def _():
        sgt_ref[...] = jnp.zeros_like(sgt_ref)
        sret_ref[...] = jnp.zeros_like(sret_ref)

    u = (_rms(x_ref[0]) * nmix_ref[...]).astype(BF16)

    def proj(off, n):
        return _dot(u, w_ref[0, :, off:off + n])

    la = _gla_log_alpha(proj(O_R, LANE), wup_ref, bg_ref)
    q = proj(O_QA, GLA_DK) * (GLA_DK ** -0.5)
    k = proj(O_KA, GLA_DK)
    v = proj(O_VA, GLA_DV).astype(BF16)
    ri = lax.broadcasted_iota(jnp.int32, (C_GLA, C_GLA), 0)
    ci = lax.broadcasted_iota(jnp.int32, (C_GLA, C_GLA), 1)
    causal = ri >= ci
    tri = jnp.where(causal, 1.0, 0.0).astype(BF16)
    o_chunks = []
    for c in range(T // C_GLA):
        sl = slice(c * C_GLA, (c + 1) * C_GLA)
        la_c = la[sl]
        la_hi = la_c.astype(BF16)
        la_lo = (la_c - la_hi.astype(F32)).astype(BF16)
        b = _dot(tri, la_hi) + _dot(tri, la_lo)
        b_mid = b[C_GLA // 2 - 1:C_GLA // 2]
        b_last = b[C_GLA - 1:C_GLA]
        q_c, k_c, v_c = q[sl], k[sl], v[sl]
        q_x = (q_c * jnp.exp(b - b_mid)).astype(BF16)
        k_x = (k_c * jnp.exp(b_mid - b)).astype(BF16)
        a = jnp.where(causal, _dot_nt(q_x, k_x), 0.0).astype(BF16)
        st = sgt_ref[...]
        q_in = (q_c * jnp.exp(b)).astype(BF16)
        o_chunks.append(_dot_nt(q_in, st.astype(BF16)) + _dot(a, v_c))
        k_out = (k_c * jnp.exp(b_last - b)).astype(BF16)
        sgt_ref[...] = st * jnp.exp(b_last) + _dot_tn(v_c, k_out)
    o_a = jnp.concatenate(o_chunks, axis=0)
    o_a = _rms(o_a) * gn_ref[...]
    g_a = proj(O_GA, GLA_DV)
    part_a = _sigmoid(proj(O_MA, GW)) * (o_a * (g_a * _sigmoid(g_a)))

    cos = cos_ref[...]
    sin = sin_ref[...]
    tl = lax.broadcasted_iota(jnp.int32, (T, LANE), 0).astype(F32)
    rt = lax.broadcasted_iota(jnp.int32, (T, T), 0)
    ct = lax.broadcasted_iota(jnp.int32, (T, T), 1)
    parts_b = []
    for j in range(RET_PER_GROUP):
        lg = lg_ref[g * RET_PER_GROUP + j]
        q_b = _rotary(proj(O_QB + j * RET_DK, RET_DK), cos, sin)
        k_b = _rotary(proj(O_KB + j * RET_DK, RET_DK), cos, sin) * (RET_DK ** -0.5)
        v_b = proj(O_VB + j * RET_DV, RET_DV).astype(BF16)
        dec_in = jnp.exp((tl + 1.0) * lg)
        dec_out = jnp.exp((T - 1.0 - tl) * lg)
        q_in = (q_b * jnp.concatenate([dec_in, dec_in], axis=-1)).astype(BF16)
        k_out = (k_b * jnp.concatenate([dec_out, dec_out], axis=-1)).astype(BF16)
        dmat = jnp.where(rt >= ct, jnp.exp((rt - ct).astype(F32) * lg), 0.0)
        a = (_dot_nt(q_b.astype(BF16), k_b.astype(BF16)) * dmat).astype(BF16)
        s = sret_ref[0, j]
        o_b = _rms(_dot(q_in, s.astype(BF16)) + _dot(a, v_b))
        dec_all = jnp.exp(jnp.full((1, RET_DV), T * lg, F32))
        sret_ref[0, j] = s * dec_all + _dot_tn(k_out, v_b)
        g_b = proj(O_GB + j * RET_DV, RET_DV)
        m_b = proj(O_MB + j * RET_DV, RET_DV)
        parts_b.append(_sigmoid(m_b) * (o_b * (g_b * _sigmoid(g_b))))

    merged_ref[0] = (part_a + jnp.concatenate(parts_b, axis=-1)).astype(merged_ref.dtype)

    @pl.when(t == pl.num_programs(2) - 1)
    def _():
        sgla_ref[0, 0] = sgt_ref[...].T


def _mix_prompt(x, lg, nmix, w_pack, wup, bg, gn, cos, sin):
    B, L, D = x.shape
    nt = L // T_BLK
    grid = (N_GROUPS, B, nt)
    return pl.pallas_call(
        _mix_prompt_kernel,
        grid=grid,
        in_specs=[
            pl.BlockSpec(memory_space=pltpu.SMEM),
            pl.BlockSpec((1, T_BLK, D), lambda g, b, t: (b, t, 0)),
            pl.BlockSpec((1, D), lambda g, b, t: (0, 0)),
            pl.BlockSpec((1, D, SLAB), lambda g, b, t: (g, 0, 0), pipeline_mode=pl.Buffered(1)),
            pl.BlockSpec((1, LANE, GLA_DK), lambda g, b, t: (g, 0, 0)),
            pl.BlockSpec((1, 1, GLA_DK), lambda g, b, t: (g, 0, 0)),
            pl.BlockSpec((1, GLA_DV), lambda g, b, t: (0, 0)),
            pl.BlockSpec((T_BLK, LANE), lambda g, b, t: (t, 0)),
            pl.BlockSpec((T_BLK, LANE), lambda g, b, t: (t, 0)),
        ],
        out_specs=[
            pl.BlockSpec((1, T_BLK, GW), lambda g, b, t: (b, t, g)),
            pl.BlockSpec((1, 1, GLA_DK, GLA_DV), lambda g, b, t: (b, g, 0, 0)),
            pl.BlockSpec((1, RET_PER_GROUP, RET_DK, RET_DV), lambda g, b, t: (b, g, 0, 0)),
        ],
        out_shape=[
            jax.ShapeDtypeStruct((B, L, D_MODEL), BF16),
            jax.ShapeDtypeStruct((B, GLA_HEADS, GLA_DK, GLA_DV), F32),
            jax.ShapeDtypeStruct((B, RET_HEADS, RET_DK, RET_DV), F32),
        ],
        scratch_shapes=[pltpu.VMEM((GLA_DV, GLA_DK), F32)],
        compiler_params=pltpu.CompilerParams(
            dimension_semantics=("arbitrary", "arbitrary", "arbitrary"),
            vmem_limit_bytes=VMEM_LIMIT),
        name="mix_prompt",
    )(lg, x, nmix, w_pack, wup, bg, gn, cos, sin)


def _out_kernel(x_ref, mg_ref, p_ref, wout_ref, nple_ref, wpg_ref, wpp_ref, nfin_ref, y_ref, *,
                final_norm):
    h = x_ref[...] + _dot(mg_ref[...], wout_ref[...])
    hn = (_rms(h) * nple_ref[...]).astype(BF16)
    gate = _sigmoid(_dot(hn, wpg_ref[...]))
    h = h + gate * _dot(p_ref[...].astype(BF16), wpp_ref[...])
    if final_norm:
        h = _rms(h) * nfin_ref[...]
    y_ref[...] = h


def _out_proj(x, merged, p, w_out, nple, w_pg, w_pp, nfin, final_norm):
    n, D = x.shape
    tm = min(TM_OUT, n)
    const = lambda i: (0, 0)
    return pl.pallas_call(
        functools.partial(_out_kernel, final_norm=final_norm),
        grid=(n // tm,),
        in_specs=[
            pl.BlockSpec((tm, D), lambda i: (i, 0)),
            pl.BlockSpec((tm, D), lambda i: (i, 0)),
            pl.BlockSpec((tm, PLE_DIM), lambda i: (i, 0)),
            pl.BlockSpec((D, D), const, pipeline_mode=pl.Buffered(1)),
            pl.BlockSpec((1, D), const),
            pl.BlockSpec((D, D), const, pipeline_mode=pl.Buffered(1)),
            pl.BlockSpec((PLE_DIM, D), const, pipeline_mode=pl.Buffered(1)),
            pl.BlockSpec((1, D), const),
        ],
        out_specs=pl.BlockSpec((tm, D), lambda i: (i, 0)),
        out_shape=jax.ShapeDtypeStruct((n, D), F32),
        compiler_params=pltpu.CompilerParams(
            dimension_semantics=("arbitrary",), vmem_limit_bytes=VMEM_LIMIT),
        name="out_proj",
    )(x, merged, p, w_out, nple, w_pg, w_pp, nfin)


def _proj_sample_kernel(x_ref, nmix_ref, w_ref, wup_ref, bg_ref, cos_ref, sin_ref, z_ref, dec_ref):
    u = (_rms(x_ref[...]) * nmix_ref[...]).astype(BF16)

    def proj(off, n):
        return _dot(u, w_ref[0, :, off:off + n])

    dec_ref[0] = jnp.exp(_gla_log_alpha(proj(O_R, LANE), wup_ref, bg_ref))
    z_ref[0, :, O_QA:O_KA] = proj(O_QA, GLA_DK) * (GLA_DK ** -0.5)
    z_ref[0, :, O_KA:O_QB] = proj(O_KA, O_QB - O_KA)
    cos = cos_ref[...]
    sin = sin_ref[...]
    for j in range(RET_PER_GROUP):
        oq = O_QB + j * RET_DK
        ok = O_KB + j * RET_DK
        z_ref[0, :, oq:oq + RET_DK] = _rotary(proj(oq, RET_DK), cos, sin)
        z_ref[0, :, ok:ok + RET_DK] = _rotary(proj(ok, RET_DK), cos, sin) * (RET_DK ** -0.5)
    z_ref[0, :, O_VB:ZW] = proj(O_VB, ZW - O_VB)


def _proj_sample(x, nmix, w_pack, wup, bg, cos, sin):
    n, D = x.shape
    return pl.pallas_call(
        _proj_sample_kernel,
        grid=(N_GROUPS,),
        in_specs=[
            pl.BlockSpec((n, D), lambda g: (0, 0)),
            pl.BlockSpec((1, D), lambda g: (0, 0)),
            pl.BlockSpec((1, D, SLAB), lambda g: (g, 0, 0)),
            pl.BlockSpec((1, LANE, GLA_DK), lambda g: (g, 0, 0)),
            pl.BlockSpec((1, 1, GLA_DK), lambda g: (g, 0, 0)),
            pl.BlockSpec((1, LANE), lambda g: (0, 0)),
            pl.BlockSpec((1, LANE), lambda g: (0, 0)),
        ],
        out_specs=[
            pl.BlockSpec((1, n, ZW), lambda g: (g, 0, 0)),
            pl.BlockSpec((1, n, GLA_DK), lambda g: (g, 0, 0)),
        ],
        out_shape=[
            jax.ShapeDtypeStruct((N_GROUPS, n, ZW), F32),
            jax.ShapeDtypeStruct((N_GROUPS, n, GLA_DK), F32),
        ],
        compiler_params=pltpu.CompilerParams(
            dimension_semantics=("arbitrary",), vmem_limit_bytes=VMEM_LIMIT),
        name="proj_sample",
    )(x, nmix, w_pack, wup, bg, cos, sin)


VT_DEC = 0
VT_KA = VT_DEC + GLA_HEADS
VT_QA = VT_KA + GLA_HEADS
VT_KB = VT_QA + GLA_HEADS
VT_QB = VT_KB + RET_HEADS
VT_N = VT_QB + RET_HEADS


def _state_sample_kernel(lg_ref, sg_ref, sr_ref, vt_ref, va_ref, vb_ref, nsg_ref, nsr_ref, oa_ref, ob_ref):
    vt = vt_ref[0]

    def col(i):
        return vt[:, i:i + 1]

    for h in range(GLA_HEADS):
        v_row = va_ref[0, :, h * GLA_DV:(h + 1) * GLA_DV]
        s_new = col(VT_DEC + h) * sg_ref[0, h] + col(VT_KA + h) * v_row
        nsg_ref[0, h] = s_new
        oa_ref[0, :, h * GLA_DV:(h + 1) * GLA_DV] = jnp.sum(col(VT_QA + h) * s_new, axis=0, keepdims=True)
    for h in range(RET_HEADS):
        v_row = vb_ref[0, :, h * RET_DV:(h + 1) * RET_DV]
        gamma = jnp.exp(jnp.full((1, RET_DV), lg_ref[h], F32))
        s_new = gamma * sr_ref[0, h] + col(VT_KB + h) * v_row
        nsr_ref[0, h] = s_new
        ob_ref[0, :, h * RET_DV:(h + 1) * RET_DV] = jnp.sum(col(VT_QB + h) * s_new, axis=0, keepdims=True)


def _state_sample(lg, sg, sr, vt, va, vb):
    n = sg.shape[0]
    row = lambda b: (b, 0, 0)
    blk4 = lambda b: (b, 0, 0, 0)
    return pl.pallas_call(
        _state_sample_kernel,
        grid=(n,),
        in_specs=[
            pl.BlockSpec(memory_space=pltpu.SMEM),
            pl.BlockSpec((1, GLA_HEADS, GLA_DK, GLA_DV), blk4),
            pl.BlockSpec((1, RET_HEADS, RET_DK, RET_DV), blk4),
            pl.BlockSpec((1, GLA_DK, VT_N), row),
            pl.BlockSpec((1, 1, GLA_V), row),
            pl.BlockSpec((1, 1, RET_V), row),
        ],
        out_specs=[
            pl.BlockSpec((1, GLA_HEADS, GLA_DK, GLA_DV), blk4),
            pl.BlockSpec((1, RET_HEADS, RET_DK, RET_DV), blk4),
            pl.BlockSpec((1, 1, GLA_V), row),
            pl.BlockSpec((1, 1, RET_V), row),
        ],
        out_shape=[
            jax.ShapeDtypeStruct(sg.shape, F32),
            jax.ShapeDtypeStruct(sr.shape, F32),
            jax.ShapeDtypeStruct((n, 1, GLA_V), F32),
            jax.ShapeDtypeStruct((n, 1, RET_V), F32),
        ],
        compiler_params=pltpu.CompilerParams(
            dimension_semantics=("arbitrary",), vmem_limit_bytes=VMEM_LIMIT),
        name="state_sample",
    )(lg, sg, sr, vt, va, vb)


def _merge_sample_kernel(oa_ref, ob_ref, ga_ref, gb_ref, ma_ref, mb_ref, gn_ref, mg_ref):
    for h in range(GLA_HEADS):
        sl = slice(h * GLA_DV, (h + 1) * GLA_DV)
        g_a = ga_ref[:, sl]
        part_a = _sigmoid(ma_ref[:, sl]) * (_rms(oa_ref[:, sl]) * gn_ref[...] * (g_a * _sigmoid(g_a)))
        parts_b = []
        for j in range(RET_PER_GROUP):
            sb = slice(h * GW + j * RET_DV, h * GW + (j + 1) * RET_DV)
            g_b = gb_ref[:, sb]
            parts_b.append(_sigmoid(mb_ref[:, sb]) * (_rms(ob_ref[:, sb]) * (g_b * _sigmoid(g_b))))
        mg_ref[:, sl] = (part_a + jnp.concatenate(parts_b, axis=-1)).astype(mg_ref.dtype)


def _merge_sample(oa, ob, ga, gb, ma, mb, gn):
    n = oa.shape[0]
    return pl.pallas_call(
        _merge_sample_kernel,
        out_shape=jax.ShapeDtypeStruct((n, D_MODEL), BF16),
        name="merge_sample",
    )(oa, ob, ga, gb, ma, mb, gn)


def _pack_w_in(w_in):
    o = dict(zip(("qa", "ka", "va", "ga", "r", "qb", "kb", "vb", "gb", "ma", "mb"), IN_OFFS))
    r_pad = jnp.pad(w_in[:, o["r"]:o["r"] + GLA_RANK], ((0, 0), (0, LANE - GLA_RANK)))
    slabs = []
    for g in range(N_GROUPS):
        cols = [
            w_in[:, o["qa"] + g * GLA_DK:o["qa"] + (g + 1) * GLA_DK],
            w_in[:, o["ka"] + g * GLA_DK:o["ka"] + (g + 1) * GLA_DK],
            w_in[:, o["va"] + g * GLA_DV:o["va"] + (g + 1) * GLA_DV],
            w_in[:, o["ga"] + g * GLA_DV:o["ga"] + (g + 1) * GLA_DV],
            w_in[:, o["ma"] + g * GW:o["ma"] + (g + 1) * GW],
            w_in[:, o["qb"] + g * GW:o["qb"] + (g + 1) * GW],
            w_in[:, o["kb"] + g * GW:o["kb"] + (g + 1) * GW],
            w_in[:, o["vb"] + g * GW:o["vb"] + (g + 1) * GW],
            w_in[:, o["gb"] + g * GW:o["gb"] + (g + 1) * GW],
            w_in[:, o["mb"] + g * GW:o["mb"] + (g + 1) * GW],
            r_pad,
        ]
        slabs.append(jnp.concatenate(cols, axis=1))
    return jnp.stack(slabs).astype(BF16)


def _rope_tables(pos):
    half = RET_DK // 2
    inv = 1.0 / (ROPE_BASE ** jnp.linspace(0.0, 1.0, half, dtype=jnp.float32))
    ang = pos[:, None] * inv[None, :]
    return jnp.cos(ang), jnp.sin(ang)


def kernel(x_prompt, x_sample, state_gla, state_ret, p_prompt, p_sample, norm_mix, w_in, w_gla_up, b_gla,
           gla_norm, w_out, norm_ple, w_ple_gate, w_ple_proj, norm_final):
    depth = w_in.shape[0]
    Bp, Lp, D = x_prompt.shape
    Bs, Ls, _ = x_sample.shape
    assert Ls == 1 and Lp % T_BLK == 0 and T_BLK % C_GLA == 0
    cos_p, sin_p = _rope_tables(jnp.arange(Lp, dtype=jnp.float32))
    cos_s, sin_s = _rope_tables(PAST_LEN + jnp.arange(Ls, dtype=jnp.float32))
    log_gamma = jnp.log(1.0 - jnp.exp2(-5.0 - jnp.arange(RET_HEADS, dtype=jnp.float32)))
    nfin = norm_final.reshape(1, D)

    hp = x_prompt
    hs = x_sample.reshape(Bs, D)
    gla_p, ret_p, gla_s, ret_s = [], [], [], []
    for i in range(depth):
        last = i == depth - 1
        nmix = norm_mix[i].reshape(1, D)
        nple = norm_ple[i].reshape(1, D)
        gn = gla_norm[i].reshape(1, GLA_DV)
        w_pack = _pack_w_in(w_in[i])
        wup = jnp.pad(w_gla_up[i], ((0, LANE - GLA_RANK), (0, 0))).astype(BF16)
        wup = wup.reshape(LANE, GLA_HEADS, GLA_DK).transpose(1, 0, 2)
        bg = b_gla[i].reshape(GLA_HEADS, 1, GLA_DK)
        w_o = w_out[i].astype(BF16)
        w_pg = w_ple_gate[i].astype(BF16)
        w_pp = w_ple_proj[i].astype(BF16)

        merged, sg, sr = _mix_prompt(hp, log_gamma, nmix, w_pack, wup, bg, gn, cos_p, sin_p)
        hp = _out_proj(hp.reshape(Bp * Lp, D), merged.reshape(Bp * Lp, D), p_prompt[i].reshape(Bp * Lp, PLE_DIM),
                       w_o, nple, w_pg, w_pp, nfin, last).reshape(Bp, Lp, D)
        gla_p.append(sg)
        ret_p.append(sr)

        z, dec = _proj_sample(hs, nmix, w_pack, wup, bg, cos_s, sin_s)

        def cols(off, n):
            return z[:, :, off:off + n]

        def heads_t(a, per_group):
            return a.reshape(N_GROUPS, Bs, per_group, GLA_DK).transpose(1, 0, 2, 3).reshape(Bs, -1, GLA_DK)

        vecs = jnp.concatenate([
            heads_t(dec, 1), heads_t(cols(O_KA, GLA_DK), 1), heads_t(cols(O_QA, GLA_DK), 1),
            heads_t(cols(O_KB, GW), RET_PER_GROUP), heads_t(cols(O_QB, GW), RET_PER_GROUP)], axis=1)
        vt = vecs.transpose(0, 2, 1)

        def wide(off):
            return cols(off, GW).transpose(1, 0, 2).reshape(Bs, D)

        nsg, nsr, oa, ob = _state_sample(log_gamma, state_gla[i], state_ret[i], vt,
                                         wide(O_VA).reshape(Bs, 1, D), wide(O_VB).reshape(Bs, 1, D))
        merged_s = _merge_sample(oa.reshape(Bs, D), ob.reshape(Bs, D), wide(O_GA), wide(O_GB),
                                 wide(O_MA), wide(O_MB), gn)
        hs = _out_proj(hs, merged_s, p_sample[i].reshape(Bs, PLE_DIM), w_o, nple, w_pg, w_pp, nfin, last)
        gla_s.append(nsg)
        ret_s.append(nsr)

    return (hp, hs.reshape(Bs, Ls, D), jnp.stack(gla_p), jnp.stack(ret_p), jnp.stack(gla_s), jnp.stack(ret_s))
```

```python
import functools

import jax
import jax.numpy as jnp
import numpy as np
from jax import lax
from jax.experimental import pallas as pl
from jax.experimental.pallas import tpu as pltpu

F32 = jnp.float32
BF16 = jnp.bfloat16

D_MODEL = 2048
PAST_LEN = 16384
PLE_DIM = 256
GLA_HEADS = 4
GLA_DK = 256
GLA_DV = 512
GLA_RANK = 16
GLA_TAU = 16.0
RET_HEADS = 8
RET_DK = 256
RET_DV = 256
ROPE_BASE = 10000.0
EPS = 1e-6

GLA_QK = GLA_HEADS * GLA_DK
GLA_V = GLA_HEADS * GLA_DV
RET_QK = RET_HEADS * RET_DK
RET_V = RET_HEADS * RET_DV
IN_SPLITS = (GLA_QK, GLA_QK, GLA_V, GLA_V, GLA_RANK, RET_QK, RET_QK, RET_V, RET_V, D_MODEL, D_MODEL)
IN_OFFS = tuple(int(v) for v in np.concatenate([[0], np.cumsum(IN_SPLITS)[:-1]]))

N_GROUPS = GLA_HEADS
RET_PER_GROUP = RET_HEADS // N_GROUPS
GW = GLA_DV
LANE = 128

R_START = IN_OFFS[4]
HI_START = IN_OFFS[5]
W_PIECES = {
    "qa": ("lo", IN_OFFS[0], GLA_DK),
    "ka": ("lo", IN_OFFS[1], GLA_DK),
    "va": ("lo", IN_OFFS[2], GLA_DV),
    "ga": ("lo", IN_OFFS[3], GLA_DV),
    "qb": ("hi", IN_OFFS[5] - HI_START, GW),
    "kb": ("hi", IN_OFFS[6] - HI_START, GW),
    "vb": ("hi", IN_OFFS[7] - HI_START, GW),
    "gb": ("hi", IN_OFFS[8] - HI_START, GW),
    "ma": ("hi", IN_OFFS[9] - HI_START, GW),
    "mb": ("hi", IN_OFFS[10] - HI_START, GW),
}
W_NAMES = tuple(W_PIECES)

T_BLK = 256
C_GLA = 64
TM_OUT = 512
VMEM_LIMIT = 56 * 1024 * 1024


def _rms(x):
    return x * lax.rsqrt(jnp.mean(x * x, axis=-1, keepdims=True) + EPS)


def _sigmoid(x):
    return 1.0 / (1.0 + jnp.exp(-x))


def _log_sigmoid(x):
    return jnp.minimum(x, 0.0) - jnp.log(1.0 + jnp.exp(-jnp.abs(x)))


def _dot(a, b):
    return jnp.dot(a, b, preferred_element_type=F32)


def _dot_nt(a, b):
    return lax.dot_general(a, b, (((1,), (1,)), ((), ())), preferred_element_type=F32)


def _dot_tn(a, b):
    return lax.dot_general(a, b, (((0,), (0,)), ((), ())), preferred_element_type=F32)


def _rotary(x, cos, sin):
    half = x.shape[-1] // 2
    x1, x2 = x[:, :half], x[:, half:]
    return jnp.concatenate([x1 * cos - x2 * sin, x1 * sin + x2 * cos], axis=-1)


def _gla_log_alpha(r, wup_ref, bg_ref):
    pre = _dot(r.astype(BF16), wup_ref[0]) + bg_ref[0]
    return _log_sigmoid(pre) * (1.0 / GLA_TAU)


def _weight_operands(w_lo, w_hi):
    return [w_lo if W_PIECES[n][0] == "lo" else w_hi for n in W_NAMES]


def _weight_specs(group_of, **kw):
    specs = []
    for n in W_NAMES:
        _, start, width = W_PIECES[n]
        first = start // width

        def index_map(*idx, first=first):
            return (0, first + group_of(*idx))

        specs.append(pl.BlockSpec((D_MODEL, width), index_map, **kw))
    return specs


def _mix_prompt_kernel(lg_ref, x_ref, nmix_ref, *refs):
    w = dict(zip(W_NAMES, refs[:len(W_NAMES)]))
    (wr_ref, wup_ref, bg_ref, gn_ref, cos_ref, sin_ref,
     merged_ref, sgla_ref, sret_ref, sgt_ref) = refs[len(W_NAMES):]
    g = pl.program_id(0)
    t = pl.program_id(2)
    T = x_ref.shape[1]

    @pl.when(t == 0)
    def _():
        sgt_ref[...] = jnp.zeros_like(sgt_ref)
        sret_ref[...] = jnp.zeros_like(sret_ref)

    u = (_rms(x_ref[0]) * nmix_ref[...]).astype(BF16)

    def proj(name, j=0, n=None):
        ref = w[name]
        n = ref.shape[1] if n is None else n
        return _dot(u, ref[:, j * n:(j + 1) * n])

    la = _gla_log_alpha(_dot(u, wr_ref[...]), wup_ref, bg_ref)
    q = proj("qa") * (GLA_DK ** -0.5)
    k = proj("ka")
    v = proj("va").astype(BF16)
    ri = lax.broadcasted_iota(jnp.int32, (C_GLA, C_GLA), 0)
    ci = lax.broadcasted_iota(jnp.int32, (C_GLA, C_GLA), 1)
    causal = ri >= ci
    tri = jnp.where(causal, 1.0, 0.0).astype(BF16)
    o_chunks = []
    for c in range(T // C_GLA):
        sl = slice(c * C_GLA, (c + 1) * C_GLA)
        la_c = la[sl]
        la_hi = la_c.astype(BF16)
        la_lo = (la_c - la_hi.astype(F32)).astype(BF16)
        b = _dot(tri, la_hi) + _dot(tri, la_lo)
        b_mid = b[C_GLA // 2 - 1:C_GLA // 2]
        b_last = b[C_GLA - 1:C_GLA]
        q_c, k_c, v_c = q[sl], k[sl], v[sl]
        q_x = (q_c * jnp.exp(b - b_mid)).astype(BF16)
        k_x = (k_c * jnp.exp(b_mid - b)).astype(BF16)
        a = jnp.where(causal, _dot_nt(q_x, k_x), 0.0).astype(BF16)
        st = sgt_ref[...]
        q_in = (q_c * jnp.exp(b)).astype(BF16)
        o_chunks.append(_dot_nt(q_in, st.astype(BF16)) + _dot(a, v_c))
        k_out = (k_c * jnp.exp(b_last - b)).astype(BF16)
        sgt_ref[...] = st * jnp.exp(b_last) + _dot_tn(v_c, k_out)
    o_a = jnp.concatenate(o_chunks, axis=0)
    o_a = _rms(o_a) * gn_ref[...]
    g_a = proj("ga")
    part_a = _sigmoid(proj("ma")) * (o_a * (g_a * _sigmoid(g_a)))

    cos = cos_ref[...]
    sin = sin_ref[...]
    tl = lax.broadcasted_iota(jnp.int32, (T, LANE), 0).astype(F32)
    rt = lax.broadcasted_iota(jnp.int32, (T, T), 0)
    ct = lax.broadcasted_iota(jnp.int32, (T, T), 1)
    parts_b = []
    for j in range(RET_PER_GROUP):
        lg = lg_ref[g * RET_PER_GROUP + j]
        q_b = _rotary(proj("qb", j, RET_DK), cos, sin)
        k_b = _rotary(proj("kb", j, RET_DK), cos, sin) * (RET_DK ** -0.5)
        v_b = proj("vb", j, RET_DV).astype(BF16)
        dec_in = jnp.exp((tl + 1.0) * lg)
        dec_out = jnp.exp((T - 1.0 - tl) * lg)
        q_in = (q_b * jnp.concatenate([dec_in, dec_in], axis=-1)).astype(BF16)
        k_out = (k_b * jnp.concatenate([dec_out, dec_out], axis=-1)).astype(BF16)
        dmat = jnp.where(rt >= ct, jnp.exp((rt - ct).astype(F32) * lg), 0.0)
        a = (_dot_nt(q_b.astype(BF16), k_b.astype(BF16)) * dmat).astype(BF16)
        s = sret_ref[0, j]
        o_b = _rms(_dot(q_in, s.astype(BF16)) + _dot(a, v_b))
        dec_all = jnp.exp(jnp.full((1, RET_DV), T * lg, F32))
        sret_ref[0, j] = s * dec_all + _dot_tn(k_out, v_b)
        g_b = proj("gb", j, RET_DV)
        m_b = proj("mb", j, RET_DV)
        parts_b.append(_sigmoid(m_b) * (o_b * (g_b * _sigmoid(g_b))))

    merged_ref[0] = (part_a + jnp.concatenate(parts_b, axis=-1)).astype(merged_ref.dtype)

    @pl.when(t == pl.num_programs(2) - 1)
    def _():
        sgla_ref[0, 0] = sgt_ref[...].T


def _mix_prompt(x, lg, nmix, w_lo, w_hi, w_r, wup, bg, gn, cos, sin):
    B, L, D = x.shape
    nt = L // T_BLK
    grid = (N_GROUPS, B, nt)
    return pl.pallas_call(
        _mix_prompt_kernel,
        grid=grid,
        in_specs=[
            pl.BlockSpec(memory_space=pltpu.SMEM),
            pl.BlockSpec((1, T_BLK, D), lambda g, b, t: (b, t, 0)),
            pl.BlockSpec((1, D), lambda g, b, t: (0, 0)),
            *_weight_specs(lambda g, b, t: g, pipeline_mode=pl.Buffered(1)),
            pl.BlockSpec((D, LANE), lambda g, b, t: (0, 0)),
            pl.BlockSpec((1, LANE, GLA_DK), lambda g, b, t: (g, 0, 0)),
            pl.BlockSpec((1, 1, GLA_DK), lambda g, b, t: (g, 0, 0)),
            pl.BlockSpec((1, GLA_DV), lambda g, b, t: (0, 0)),
            pl.BlockSpec((T_BLK, LANE), lambda g, b, t: (t, 0)),
            pl.BlockSpec((T_BLK, LANE), lambda g, b, t: (t, 0)),
        ],
        out_specs=[
            pl.BlockSpec((1, T_BLK, GW), lambda g, b, t: (b, t, g)),
            pl.BlockSpec((1, 1, GLA_DK, GLA_DV), lambda g, b, t: (b, g, 0, 0)),
            pl.BlockSpec((1, RET_PER_GROUP, RET_DK, RET_DV), lambda g, b, t: (b, g, 0, 0)),
        ],
        out_shape=[
            jax.ShapeDtypeStruct((B, L, D_MODEL), BF16),
            jax.ShapeDtypeStruct((B, GLA_HEADS, GLA_DK, GLA_DV), F32),
            jax.ShapeDtypeStruct((B, RET_HEADS, RET_DK, RET_DV), F32),
        ],
        scratch_shapes=[pltpu.VMEM((GLA_DV, GLA_DK), F32)],
        compiler_params=pltpu.CompilerParams(
            dimension_semantics=("arbitrary", "arbitrary", "arbitrary"),
            vmem_limit_bytes=VMEM_LIMIT),
        name="mix_prompt",
    )(lg, x, nmix, *_weight_operands(w_lo, w_hi), w_r, wup, bg, gn, cos, sin)


def _out_kernel(x_ref, mg_ref, p_ref, wout_ref, nple_ref, wpg_ref, wpp_ref, nfin_ref, y_ref, *,
                final_norm):
    h = x_ref[...] + _dot(mg_ref[...], wout_ref[...])
    hn = (_rms(h) * nple_ref[...]).astype(BF16)
    gate = _sigmoid(_dot(hn, wpg_ref[...]))
    h = h + gate * _dot(p_ref[...].astype(BF16), wpp_ref[...])
    if final_norm:
        h = _rms(h) * nfin_ref[...]
    y_ref[...] = h


def _out_proj(x, merged, p, w_out, nple, w_pg, w_pp, nfin, final_norm):
    n, D = x.shape
    tm = min(TM_OUT, n)
    const = lambda i: (0, 0)
    return pl.pallas_call(
        functools.partial(_out_kernel, final_norm=final_norm),
        grid=(n // tm,),
        in_specs=[
            pl.BlockSpec((tm, D), lambda i: (i, 0)),
            pl.BlockSpec((tm, D), lambda i: (i, 0)),
            pl.BlockSpec((tm, PLE_DIM), lambda i: (i, 0)),
            pl.BlockSpec((D, D), const, pipeline_mode=pl.Buffered(1)),
            pl.BlockSpec((1, D), const),
            pl.BlockSpec((D, D), const, pipeline_mode=pl.Buffered(1)),
            pl.BlockSpec((PLE_DIM, D), const, pipeline_mode=pl.Buffered(1)),
            pl.BlockSpec((1, D), const),
        ],
        out_specs=pl.BlockSpec((tm, D), lambda i: (i, 0)),
        out_shape=jax.ShapeDtypeStruct((n, D), F32),
        compiler_params=pltpu.CompilerParams(
            dimension_semantics=("arbitrary",), vmem_limit_bytes=VMEM_LIMIT),
        name="out_proj",
    )(x, merged, p, w_out, nple, w_pg, w_pp, nfin)


Z_SAMPLE = (("dec", GLA_DK), ("qa", GLA_DK), ("ka", GLA_DK), ("va", GLA_DV), ("ga", GLA_DV), ("ma", GW),
            ("qb", GW), ("kb", GW), ("vb", GW), ("gb", GW), ("mb", GW))


def _proj_sample_kernel(x_ref, nmix_ref, *refs):
    w = dict(zip(W_NAMES, refs[:len(W_NAMES)]))
    wr_ref, wup_ref, bg_ref, cos_ref, sin_ref = refs[len(W_NAMES):len(W_NAMES) + 5]
    out = dict(zip([n for n, _ in Z_SAMPLE], refs[len(W_NAMES) + 5:]))
    u = (_rms(x_ref[...]) * nmix_ref[...]).astype(BF16)

    def proj(name):
        return _dot(u, w[name][...])

    out["dec"][...] = jnp.exp(_gla_log_alpha(_dot(u, wr_ref[...]), wup_ref, bg_ref))
    out["qa"][...] = proj("qa") * (GLA_DK ** -0.5)
    for name in ("ka", "va", "ga", "ma", "vb", "gb", "mb"):
        out[name][...] = proj(name)
    cos = cos_ref[...]
    sin = sin_ref[...]
    q_b = proj("qb")
    k_b = proj("kb")
    for j in range(RET_PER_GROUP):
        sl = slice(j * RET_DK, (j + 1) * RET_DK)
        out["qb"][:, sl] = _rotary(q_b[:, sl], cos, sin)
        out["kb"][:, sl] = _rotary(k_b[:, sl], cos, sin) * (RET_DK ** -0.5)


def _proj_sample(x, nmix, w_lo, w_hi, w_r, wup, bg, cos, sin):
    n, D = x.shape
    return pl.pallas_call(
        _proj_sample_kernel,
        grid=(N_GROUPS,),
        in_specs=[
            pl.BlockSpec((n, D), lambda g: (0, 0)),
            pl.BlockSpec((1, D), lambda g: (0, 0)),
            *_weight_specs(lambda g: g),
            pl.BlockSpec((D, LANE), lambda g: (0, 0)),
            pl.BlockSpec((1, LANE, GLA_DK), lambda g: (g, 0, 0)),
            pl.BlockSpec((1, 1, GLA_DK), lambda g: (g, 0, 0)),
            pl.BlockSpec((1, LANE), lambda g: (0, 0)),
            pl.BlockSpec((1, LANE), lambda g: (0, 0)),
        ],
        out_specs=[pl.BlockSpec((n, width), lambda g: (0, g)) for _, width in Z_SAMPLE],
        out_shape=[jax.ShapeDtypeStruct((n, N_GROUPS * width), F32) for _, width in Z_SAMPLE],
        compiler_params=pltpu.CompilerParams(
            dimension_semantics=("arbitrary",), vmem_limit_bytes=VMEM_LIMIT),
        name="proj_sample",
    )(x, nmix, *_weight_operands(w_lo, w_hi), w_r, wup, bg, cos, sin)


VT_DEC = 0
VT_KA = VT_DEC + GLA_HEADS
VT_QA = VT_KA + GLA_HEADS
VT_KB = VT_QA + GLA_HEADS
VT_QB = VT_KB + RET_HEADS
VT_N = VT_QB + RET_HEADS
VT_PAD = 32


def _state_sample_kernel(lg_ref, sg_ref, sr_ref, dec_ref, ka_ref, qa_ref, kb_ref, qb_ref, va_ref, vb_ref,
                         nsg_ref, nsr_ref, oa_ref, ob_ref):
    rows = []
    for ref, heads in ((dec_ref, GLA_HEADS), (ka_ref, GLA_HEADS), (qa_ref, GLA_HEADS),
                       (kb_ref, RET_HEADS), (qb_ref, RET_HEADS)):
        rows += [ref[0, :, h * GLA_DK:(h + 1) * GLA_DK] for h in range(heads)]
    rows.append(jnp.zeros((VT_PAD - VT_N, GLA_DK), F32))
    vt = jnp.concatenate(rows, axis=0).T

    def col(i):
        return vt[:, i:i + 1]

    for h in range(GLA_HEADS):
        v_row = va_ref[0, :, h * GLA_DV:(h + 1) * GLA_DV]
        s_new = col(VT_DEC + h) * sg_ref[0, h] + col(VT_KA + h) * v_row
        nsg_ref[0, h] = s_new
        oa_ref[0, :, h * GLA_DV:(h + 1) * GLA_DV] = jnp.sum(col(VT_QA + h) * s_new, axis=0, keepdims=True)
    for h in range(RET_HEADS):
        v_row = vb_ref[0, :, h * RET_DV:(h + 1) * RET_DV]
        gamma = jnp.exp(jnp.full((1, RET_DV), lg_ref[h], F32))
        s_new = gamma * sr_ref[0, h] + col(VT_KB + h) * v_row
        nsr_ref[0, h] = s_new
        ob_ref[0, :, h * RET_DV:(h + 1) * RET_DV] = jnp.sum(col(VT_QB + h) * s_new, axis=0, keepdims=True)


def _state_sample(lg, sg, sr, dec, ka, qa, kb, qb, va, vb):
    n = sg.shape[0]
    row = lambda b: (b, 0, 0)
    blk4 = lambda b: (b, 0, 0, 0)
    rows3 = lambda a: a.reshape(n, 1, a.shape[-1])
    vecs = [rows3(a) for a in (dec, ka, qa, kb, qb, va, vb)]
    return pl.pallas_call(
        _state_sample_kernel,
        grid=(n,),
        in_specs=[
            pl.BlockSpec(memory_space=pltpu.SMEM),
            pl.BlockSpec((1, GLA_HEADS, GLA_DK, GLA_DV), blk4),
            pl.BlockSpec((1, RET_HEADS, RET_DK, RET_DV), blk4),
            *[pl.BlockSpec((1, 1, a.shape[-1]), row) for a in vecs],
        ],
        out_specs=[
            pl.BlockSpec((1, GLA_HEADS, GLA_DK, GLA_DV), blk4),
            pl.BlockSpec((1, RET_HEADS, RET_DK, RET_DV), blk4),
            pl.BlockSpec((1, 1, GLA_V), row),
            pl.BlockSpec((1, 1, RET_V), row),
        ],
        out_shape=[
            jax.ShapeDtypeStruct(sg.shape, F32),
            jax.ShapeDtypeStruct(sr.shape, F32),
            jax.ShapeDtypeStruct((n, 1, GLA_V), F32),
            jax.ShapeDtypeStruct((n, 1, RET_V), F32),
        ],
        compiler_params=pltpu.CompilerParams(
            dimension_semantics=("arbitrary",), vmem_limit_bytes=VMEM_LIMIT),
        name="state_sample",
    )(lg, sg, sr, *vecs)


def _merge_sample_kernel(oa_ref, ob_ref, ga_ref, gb_ref, ma_ref, mb_ref, gn_ref, mg_ref):
    for h in range(GLA_HEADS):
        sl = slice(h * GLA_DV, (h + 1) * GLA_DV)
        g_a = ga_ref[:, sl]
        part_a = _sigmoid(ma_ref[:, sl]) * (_rms(oa_ref[:, sl]) * gn_ref[...] * (g_a * _sigmoid(g_a)))
        parts_b = []
        for j in range(RET_PER_GROUP):
            sb = slice(h * GW + j * RET_DV, h * GW + (j + 1) * RET_DV)
            g_b = gb_ref[:, sb]
            parts_b.append(_sigmoid(mb_ref[:, sb]) * (_rms(ob_ref[:, sb]) * (g_b * _sigmoid(g_b))))
        mg_ref[:, sl] = (part_a + jnp.concatenate(parts_b, axis=-1)).astype(mg_ref.dtype)


def _merge_sample(oa, ob, ga, gb, ma, mb, gn):
    n = oa.shape[0]
    return pl.pallas_call(
        _merge_sample_kernel,
        out_shape=jax.ShapeDtypeStruct((n, D_MODEL), BF16),
        name="merge_sample",
    )(oa, ob, ga, gb, ma, mb, gn)


def _rope_tables(pos):
    half = RET_DK // 2
    inv = 1.0 / (ROPE_BASE ** jnp.linspace(0.0, 1.0, half, dtype=jnp.float32))
    ang = pos[:, None] * inv[None, :]
    return jnp.cos(ang), jnp.sin(ang)


def kernel(x_prompt, x_sample, state_gla, state_ret, p_prompt, p_sample, norm_mix, w_in, w_gla_up, b_gla,
           gla_norm, w_out, norm_ple, w_ple_gate, w_ple_proj, norm_final):
    depth = w_in.shape[0]
    Bp, Lp, D = x_prompt.shape
    Bs, Ls, _ = x_sample.shape
    assert Ls == 1 and Lp % T_BLK == 0 and T_BLK % C_GLA == 0
    cos_p, sin_p = _rope_tables(jnp.arange(Lp, dtype=jnp.float32))
    cos_s, sin_s = _rope_tables(PAST_LEN + jnp.arange(Ls, dtype=jnp.float32))
    log_gamma = jnp.log(1.0 - jnp.exp2(-5.0 - jnp.arange(RET_HEADS, dtype=jnp.float32)))
    nfin = norm_final.reshape(1, D)

    hp = x_prompt
    hs = x_sample.reshape(Bs, D)
    gla_p, ret_p, gla_s, ret_s = [], [], [], []
    for i in range(depth):
        last = i == depth - 1
        nmix = norm_mix[i].reshape(1, D)
        nple = norm_ple[i].reshape(1, D)
        gn = gla_norm[i].reshape(1, GLA_DV)
        w_lo = w_in[i, :, :R_START].astype(BF16)
        w_hi = w_in[i, :, HI_START:].astype(BF16)
        w_r = jnp.pad(w_in[i, :, R_START:HI_START], ((0, 0), (0, LANE - GLA_RANK))).astype(BF16)
        wup = jnp.pad(w_gla_up[i], ((0, LANE - GLA_RANK), (0, 0))).astype(BF16)
        wup = wup.reshape(LANE, GLA_HEADS, GLA_DK).transpose(1, 0, 2)
        bg = b_gla[i].reshape(GLA_HEADS, 1, GLA_DK)
        w_o = w_out[i].astype(BF16)
        w_pg = w_ple_gate[i].astype(BF16)
        w_pp = w_ple_proj[i].astype(BF16)

        merged, sg, sr = _mix_prompt(hp, log_gamma, nmix, w_lo, w_hi, w_r, wup, bg, gn, cos_p, sin_p)
        hp = _out_proj(hp.reshape(Bp * Lp, D), merged.reshape(Bp * Lp, D), p_prompt[i].reshape(Bp * Lp, PLE_DIM),
                       w_o, nple, w_pg, w_pp, nfin, last).reshape(Bp, Lp, D)
        gla_p.append(sg)
        ret_p.append(sr)

        z = dict(zip([n for n, _ in Z_SAMPLE],
                     _proj_sample(hs, nmix, w_lo, w_hi, w_r, wup, bg, cos_s, sin_s)))
        nsg, nsr, oa, ob = _state_sample(log_gamma, state_gla[i], state_ret[i], z["dec"], z["ka"], z["qa"],
                                         z["kb"], z["qb"], z["va"], z["vb"])
        merged_s = _merge_sample(oa.reshape(Bs, D), ob.reshape(Bs, D), z["ga"], z["gb"], z["ma"], z["mb"], gn)
        hs = _out_proj(hs, merged_s, p_sample[i].reshape(Bs, PLE_DIM), w_o, nple, w_pg, w_pp, nfin, last)
        gla_s.append(nsg)
        ret_s.append(nsr)

    return (hp, hs.reshape(Bs, Ls, D), jnp.stack(gla_p), jnp.stack(ret_p), jnp.stack(gla_s), jnp.stack(ret_s))
```

```python
import functools

import jax
import jax.numpy as jnp
import numpy as np
from jax import lax
from jax.experimental import pallas as pl
from jax.experimental.pallas import tpu as pltpu

F32 = jnp.float32
BF16 = jnp.bfloat16

D_MODEL = 2048
PAST_LEN = 16384
PLE_DIM = 256
GLA_HEADS = 4
GLA_DK = 256
GLA_DV = 512
GLA_RANK = 16
GLA_TAU = 16.0
RET_HEADS = 8
RET_DK = 256
RET_DV = 256
ROPE_BASE = 10000.0
EPS = 1e-6

GLA_QK = GLA_HEADS * GLA_DK
GLA_V = GLA_HEADS * GLA_DV
RET_QK = RET_HEADS * RET_DK
RET_V = RET_HEADS * RET_DV
IN_SPLITS = (GLA_QK, GLA_QK, GLA_V, GLA_V, GLA_RANK, RET_QK, RET_QK, RET_V, RET_V, D_MODEL, D_MODEL)
IN_OFFS = tuple(int(v) for v in np.concatenate([[0], np.cumsum(IN_SPLITS)[:-1]]))

N_GROUPS = GLA_HEADS
RET_PER_GROUP = RET_HEADS // N_GROUPS
GW = GLA_DV
LANE = 128
SUBLANE = 8

R_START = IN_OFFS[4]
HI_START = IN_OFFS[5]
W_PIECES = {
    "qa": ("lo", IN_OFFS[0], GLA_DK),
    "ka": ("lo", IN_OFFS[1], GLA_DK),
    "va": ("lo", IN_OFFS[2], GLA_DV),
    "ga": ("lo", IN_OFFS[3], GLA_DV),
    "qb": ("hi", IN_OFFS[5] - HI_START, GW),
    "kb": ("hi", IN_OFFS[6] - HI_START, GW),
    "vb": ("hi", IN_OFFS[7] - HI_START, GW),
    "gb": ("hi", IN_OFFS[8] - HI_START, GW),
    "ma": ("hi", IN_OFFS[9] - HI_START, GW),
    "mb": ("hi", IN_OFFS[10] - HI_START, GW),
}
W_NAMES = tuple(W_PIECES)

T_BLK = 256
C_GLA = 64
N_CHUNKS = T_BLK // C_GLA
TM_OUT = 512
VMEM_LIMIT = 56 * 1024 * 1024


def _rms(x):
    return x * lax.rsqrt(jnp.mean(x * x, axis=-1, keepdims=True) + EPS)


def _sigmoid(x):
    return 1.0 / (1.0 + jnp.exp(-x))


def _log_sigmoid(x):
    return jnp.minimum(x, 0.0) - jnp.log(1.0 + jnp.exp(-jnp.abs(x)))


def _dot(a, b):
    return jnp.dot(a, b, preferred_element_type=F32)


def _dot_nt(a, b):
    return lax.dot_general(a, b, (((1,), (1,)), ((), ())), preferred_element_type=F32)


def _dot_tn(a, b):
    return lax.dot_general(a, b, (((0,), (0,)), ((), ())), preferred_element_type=F32)


def _rotary(x, cos, sin):
    half = x.shape[-1] // 2
    x1, x2 = x[:, :half], x[:, half:]
    return jnp.concatenate([x1 * cos - x2 * sin, x1 * sin + x2 * cos], axis=-1)


def _gla_log_alpha(r, wup_ref, bg_ref):
    pre = _dot(r.astype(BF16), wup_ref[0]) + bg_ref[0]
    return _log_sigmoid(pre) * (1.0 / GLA_TAU)


def _weight_operands(w_lo, w_hi):
    return [w_lo if W_PIECES[n][0] == "lo" else w_hi for n in W_NAMES]


def _weight_specs(group_of, **kw):
    specs = []
    for n in W_NAMES:
        _, start, width = W_PIECES[n]
        first = start // width

        def index_map(*idx, first=first):
            return (first + group_of(*idx), 0)

        specs.append(pl.BlockSpec((width, D_MODEL), index_map, **kw))
    return specs


Z_BUFFERS = (
    ("qx", (T_BLK, GLA_DK), BF16),
    ("kx", (T_BLK, GLA_DK), BF16),
    ("qi", (T_BLK, GLA_DK), BF16),
    ("ko", (T_BLK, GLA_DK), BF16),
    ("va", (T_BLK, GLA_DV), BF16),
    ("dl", (SUBLANE, GLA_DK), F32),
    ("gta", (T_BLK, GW), F32),
    ("qb", (RET_PER_GROUP, T_BLK, RET_DK), BF16),
    ("kb", (RET_PER_GROUP, T_BLK, RET_DK), BF16),
    ("qbi", (RET_PER_GROUP, T_BLK, RET_DK), BF16),
    ("kbo", (RET_PER_GROUP, T_BLK, RET_DK), BF16),
    ("vb", (RET_PER_GROUP, T_BLK, RET_DV), BF16),
    ("gtb", (T_BLK, GW), F32),
)
Z_NAMES = tuple(n for n, _, _ in Z_BUFFERS)
assert N_CHUNKS <= SUBLANE


def _project_block(x_ref, nmix_ref, w, wr_ref, wup_ref, bg_ref, cos_ref, sin_ref, lg_ref, g, z):
    T = T_BLK
    u = (_rms(x_ref[0]) * nmix_ref[...]).astype(BF16)

    def proj(name, j=0, n=None):
        ref = w[name]
        n = ref.shape[0] if n is None else n
        return _dot_nt(u, ref[j * n:(j + 1) * n, :])

    la = _gla_log_alpha(_dot_nt(u, wr_ref[...]), wup_ref, bg_ref)
    q = proj("qa") * (GLA_DK ** -0.5)
    k = proj("ka")
    z["va"][...] = proj("va").astype(BF16)
    ri = lax.broadcasted_iota(jnp.int32, (C_GLA, C_GLA), 0)
    ci = lax.broadcasted_iota(jnp.int32, (C_GLA, C_GLA), 1)
    tri = jnp.where(ri >= ci, 1.0, 0.0).astype(BF16)
    for c in range(N_CHUNKS):
        sl = slice(c * C_GLA, (c + 1) * C_GLA)
        la_c = la[sl]
        la_hi = la_c.astype(BF16)
        la_lo = (la_c - la_hi.astype(F32)).astype(BF16)
        b = _dot(tri, la_hi) + _dot(tri, la_lo)
        b_mid = b[C_GLA // 2 - 1:C_GLA // 2]
        b_last = b[C_GLA - 1:C_GLA]
        q_c, k_c = q[sl], k[sl]
        z["qx"][sl] = (q_c * jnp.exp(b - b_mid)).astype(BF16)
        z["kx"][sl] = (k_c * jnp.exp(b_mid - b)).astype(BF16)
        z["qi"][sl] = (q_c * jnp.exp(b)).astype(BF16)
        z["ko"][sl] = (k_c * jnp.exp(b_last - b)).astype(BF16)
        z["dl"][c:c + 1] = jnp.exp(b_last)
    g_a = proj("ga")
    z["gta"][...] = _sigmoid(proj("ma")) * (g_a * _sigmoid(g_a))

    cos = cos_ref[...]
    sin = sin_ref[...]
    tl = lax.broadcasted_iota(jnp.int32, (T, LANE), 0).astype(F32)
    for j in range(RET_PER_GROUP):
        lg = lg_ref[g * RET_PER_GROUP + j]
        q_b = _rotary(proj("qb", j, RET_DK), cos, sin)
        k_b = _rotary(proj("kb", j, RET_DK), cos, sin) * (RET_DK ** -0.5)
        dec_in = jnp.exp((tl + 1.0) * lg)
        dec_out = jnp.exp((T - 1.0 - tl) * lg)
        z["qb"][j] = q_b.astype(BF16)
        z["kb"][j] = k_b.astype(BF16)
        z["qbi"][j] = (q_b * jnp.concatenate([dec_in, dec_in], axis=-1)).astype(BF16)
        z["kbo"][j] = (k_b * jnp.concatenate([dec_out, dec_out], axis=-1)).astype(BF16)
        z["vb"][j] = proj("vb", j, RET_DV).astype(BF16)
        g_b = proj("gb", j, RET_DV)
        z["gtb"][:, j * RET_DV:(j + 1) * RET_DV] = _sigmoid(proj("mb", j, RET_DV)) * (g_b * _sigmoid(g_b))


def _recur_block(z, lg_ref, g, gn_ref, sgt_ref, sret_ref, merged_ref):
    T = T_BLK
    ri = lax.broadcasted_iota(jnp.int32, (C_GLA, C_GLA), 0)
    ci = lax.broadcasted_iota(jnp.int32, (C_GLA, C_GLA), 1)
    causal = ri >= ci
    st = sgt_ref[...]
    o_chunks = []
    for c in range(N_CHUNKS):
        sl = slice(c * C_GLA, (c + 1) * C_GLA)
        v_c = z["va"][sl]
        a = jnp.where(causal, _dot_nt(z["qx"][sl], z["kx"][sl]), 0.0).astype(BF16)
        o_chunks.append(_dot_nt(z["qi"][sl], st.astype(BF16)) + _dot(a, v_c))
        st = st * z["dl"][c:c + 1] + _dot_tn(v_c, z["ko"][sl])
    sgt_ref[...] = st
    part_a = z["gta"][...] * (_rms(jnp.concatenate(o_chunks, axis=0)) * gn_ref[...])

    rt = lax.broadcasted_iota(jnp.int32, (T, T), 0)
    ct = lax.broadcasted_iota(jnp.int32, (T, T), 1)
    parts_b = []
    for j in range(RET_PER_GROUP):
        lg = lg_ref[g * RET_PER_GROUP + j]
        v_b = z["vb"][j]
        dmat = jnp.where(rt >= ct, jnp.exp((rt - ct).astype(F32) * lg), 0.0)
        a = (_dot_nt(z["qb"][j], z["kb"][j]) * dmat).astype(BF16)
        s = sret_ref[0, j]
        o_b = _rms(_dot(z["qbi"][j], s.astype(BF16)) + _dot(a, v_b))
        dec_all = jnp.exp(jnp.full((1, RET_DV), T * lg, F32))
        sret_ref[0, j] = s * dec_all + _dot_tn(z["kbo"][j], v_b)
        parts_b.append(z["gtb"][:, j * RET_DV:(j + 1) * RET_DV] * o_b)
    merged_ref[0] = (part_a + jnp.concatenate(parts_b, axis=-1)).astype(merged_ref.dtype)


def _mix_prompt_kernel(lg_ref, x_ref, nmix_ref, *refs, blocks_per_group, blocks_per_seq):
    nw, nz = len(W_NAMES), len(Z_NAMES)
    w = dict(zip(W_NAMES, refs[:nw]))
    wr_ref, wup_ref, bg_ref, gn_ref, cos_ref, sin_ref, merged_ref, sgla_ref, sret_ref, sgt_ref = refs[nw:nw + 10]
    z_even = dict(zip(Z_NAMES, refs[nw + 10:nw + 10 + nz]))
    z_odd = dict(zip(Z_NAMES, refs[nw + 10 + nz:]))
    s = pl.program_id(0)
    n_blocks = pl.num_programs(0) - 1
    g_p = jnp.minimum(s, n_blocks - 1) // blocks_per_group
    r = jnp.maximum(s - 1, 0)
    g_r = r // blocks_per_group
    t_r = r % blocks_per_seq

    @pl.when(s == 0)
    def _():
        for ref in z_odd.values():
            ref[...] = jnp.zeros_like(ref)

    @pl.when(t_r == 0)
    def _():
        sgt_ref[...] = jnp.zeros_like(sgt_ref)
        sret_ref[...] = jnp.zeros_like(sret_ref)

    def step(z_write, z_read):
        _recur_block(z_read, lg_ref, g_r, gn_ref, sgt_ref, sret_ref, merged_ref)
        _project_block(x_ref, nmix_ref, w, wr_ref, wup_ref, bg_ref, cos_ref, sin_ref, lg_ref, g_p, z_write)

    @pl.when(s % 2 == 0)
    def _():
        step(z_even, z_odd)

    @pl.when(s % 2 == 1)
    def _():
        step(z_odd, z_even)

    @pl.when((t_r == blocks_per_seq - 1) & (s > 0))
    def _():
        sgla_ref[0, 0] = sgt_ref[...].T


def _mix_prompt(x, lg, nmix, w_lo, w_hi, w_r, wup, bg, gn, cos, sin):
    B, L, D = x.shape
    nt = L // T_BLK
    n_blocks = N_GROUPS * B * nt

    def proj_idx(s):
        p = jnp.minimum(s, n_blocks - 1)
        return p // (B * nt), (p // nt) % B, p % nt

    def recur_idx(s):
        r = jnp.maximum(s - 1, 0)
        return r // (B * nt), (r // nt) % B, r % nt

    def out_map(s):
        g, b, t = recur_idx(s)
        return (b, t, g)

    def state_map(s):
        g, b, _ = recur_idx(s)
        return (b, g, 0, 0)

    z_scratch = [pltpu.VMEM(shape, dtype) for _, shape, dtype in Z_BUFFERS]
    return pl.pallas_call(
        functools.partial(_mix_prompt_kernel, blocks_per_group=B * nt, blocks_per_seq=nt),
        grid=(n_blocks + 1,),
        in_specs=[
            pl.BlockSpec(memory_space=pltpu.SMEM),
            pl.BlockSpec((1, T_BLK, D), lambda s: (proj_idx(s)[1], proj_idx(s)[2], 0)),
            pl.BlockSpec((1, D), lambda s: (0, 0)),
            *_weight_specs(lambda s: proj_idx(s)[0], pipeline_mode=pl.Buffered(1)),
            pl.BlockSpec((LANE, D), lambda s: (0, 0)),
            pl.BlockSpec((1, LANE, GLA_DK), lambda s: (proj_idx(s)[0], 0, 0)),
            pl.BlockSpec((1, 1, GLA_DK), lambda s: (proj_idx(s)[0], 0, 0)),
            pl.BlockSpec((1, GLA_DV), lambda s: (0, 0)),
            pl.BlockSpec((T_BLK, LANE), lambda s: (proj_idx(s)[2], 0)),
            pl.BlockSpec((T_BLK, LANE), lambda s: (proj_idx(s)[2], 0)),
        ],
        out_specs=[
            pl.BlockSpec((1, T_BLK, GW), out_map),
            pl.BlockSpec((1, 1, GLA_DK, GLA_DV), state_map),
            pl.BlockSpec((1, RET_PER_GROUP, RET_DK, RET_DV), state_map),
        ],
        out_shape=[
            jax.ShapeDtypeStruct((B, L, D_MODEL), BF16),
            jax.ShapeDtypeStruct((B, GLA_HEADS, GLA_DK, GLA_DV), F32),
            jax.ShapeDtypeStruct((B, RET_HEADS, RET_DK, RET_DV), F32),
        ],
        scratch_shapes=[pltpu.VMEM((GLA_DV, GLA_DK), F32)] + z_scratch + z_scratch,
        compiler_params=pltpu.CompilerParams(
            dimension_semantics=("arbitrary",), vmem_limit_bytes=VMEM_LIMIT),
        name="mix_prompt",
    )(lg, x, nmix, *_weight_operands(w_lo, w_hi), w_r, wup, bg, gn, cos, sin)


def _out_kernel(x_ref, mg_ref, p_ref, wout_ref, nple_ref, wpg_ref, wpp_ref, nfin_ref, y_ref, *,
                final_norm):
    h = x_ref[...] + _dot(mg_ref[...], wout_ref[...])
    hn = (_rms(h) * nple_ref[...]).astype(BF16)
    gate = _sigmoid(_dot(hn, wpg_ref[...]))
    h = h + gate * _dot(p_ref[...].astype(BF16), wpp_ref[...])
    if final_norm:
        h = _rms(h) * nfin_ref[...]
    y_ref[...] = h


def _out_proj(x, merged, p, w_out, nple, w_pg, w_pp, nfin, final_norm):
    n, D = x.shape
    tm = min(TM_OUT, n)
    const = lambda i: (0, 0)
    return pl.pallas_call(
        functools.partial(_out_kernel, final_norm=final_norm),
        grid=(n // tm,),
        in_specs=[
            pl.BlockSpec((tm, D), lambda i: (i, 0)),
            pl.BlockSpec((tm, D), lambda i: (i, 0)),
            pl.BlockSpec((tm, PLE_DIM), lambda i: (i, 0)),
            pl.BlockSpec((D, D), const, pipeline_mode=pl.Buffered(1)),
            pl.BlockSpec((1, D), const),
            pl.BlockSpec((D, D), const, pipeline_mode=pl.Buffered(1)),
            pl.BlockSpec((PLE_DIM, D), const, pipeline_mode=pl.Buffered(1)),
            pl.BlockSpec((1, D), const),
        ],
        out_specs=pl.BlockSpec((tm, D), lambda i: (i, 0)),
        out_shape=jax.ShapeDtypeStruct((n, D), F32),
        compiler_params=pltpu.CompilerParams(
            dimension_semantics=("arbitrary",), vmem_limit_bytes=VMEM_LIMIT),
        name="out_proj",
    )(x, merged, p, w_out, nple, w_pg, w_pp, nfin)


Z_SAMPLE = (("dec", GLA_DK), ("qa", GLA_DK), ("ka", GLA_DK), ("va", GLA_DV), ("ga", GLA_DV), ("ma", GW),
            ("qb", GW), ("kb", GW), ("vb", GW), ("gb", GW), ("mb", GW))


def _proj_sample_kernel(x_ref, nmix_ref, *refs):
    w = dict(zip(W_NAMES, refs[:len(W_NAMES)]))
    wr_ref, wup_ref, bg_ref, cos_ref, sin_ref = refs[len(W_NAMES):len(W_NAMES) + 5]
    out = dict(zip([n for n, _ in Z_SAMPLE], refs[len(W_NAMES) + 5:]))
    u = (_rms(x_ref[...]) * nmix_ref[...]).astype(BF16)

    def proj(name):
        return _dot_nt(u, w[name][...])

    out["dec"][...] = jnp.exp(_gla_log_alpha(_dot_nt(u, wr_ref[...]), wup_ref, bg_ref))
    out["qa"][...] = proj("qa") * (GLA_DK ** -0.5)
    for name in ("ka", "va", "ga", "ma", "vb", "gb", "mb"):
        out[name][...] = proj(name)
    cos = cos_ref[...]
    sin = sin_ref[...]
    q_b = proj("qb")
    k_b = proj("kb")
    for j in range(RET_PER_GROUP):
        sl = slice(j * RET_DK, (j + 1) * RET_DK)
        out["qb"][:, sl] = _rotary(q_b[:, sl], cos, sin)
        out["kb"][:, sl] = _rotary(k_b[:, sl], cos, sin) * (RET_DK ** -0.5)


def _proj_sample(x, nmix, w_lo, w_hi, w_r, wup, bg, cos, sin):
    n, D = x.shape
    return pl.pallas_call(
        _proj_sample_kernel,
        grid=(N_GROUPS,),
        in_specs=[
            pl.BlockSpec((n, D), lambda g: (0, 0)),
            pl.BlockSpec((1, D), lambda g: (0, 0)),
            *_weight_specs(lambda g: g),
            pl.BlockSpec((LANE, D), lambda g: (0, 0)),
            pl.BlockSpec((1, LANE, GLA_DK), lambda g: (g, 0, 0)),
            pl.BlockSpec((1, 1, GLA_DK), lambda g: (g, 0, 0)),
            pl.BlockSpec((1, LANE), lambda g: (0, 0)),
            pl.BlockSpec((1, LANE), lambda g: (0, 0)),
        ],
        out_specs=[pl.BlockSpec((n, width), lambda g: (0, g)) for _, width in Z_SAMPLE],
        out_shape=[jax.ShapeDtypeStruct((n, N_GROUPS * width), F32) for _, width in Z_SAMPLE],
        compiler_params=pltpu.CompilerParams(
            dimension_semantics=("arbitrary",), vmem_limit_bytes=VMEM_LIMIT),
        name="proj_sample",
    )(x, nmix, *_weight_operands(w_lo, w_hi), w_r, wup, bg, cos, sin)


VT_DEC = 0
VT_KA = VT_DEC + GLA_HEADS
VT_QA = VT_KA + GLA_HEADS
VT_KB = VT_QA + GLA_HEADS
VT_QB = VT_KB + RET_HEADS
VT_N = VT_QB + RET_HEADS
VT_PAD = 32


def _state_sample_kernel(lg_ref, sg_ref, sr_ref, dec_ref, ka_ref, qa_ref, kb_ref, qb_ref, va_ref, vb_ref,
                         nsg_ref, nsr_ref, oa_ref, ob_ref):
    rows = []
    for ref, heads in ((dec_ref, GLA_HEADS), (ka_ref, GLA_HEADS), (qa_ref, GLA_HEADS),
                       (kb_ref, RET_HEADS), (qb_ref, RET_HEADS)):
        rows += [ref[0, :, h * GLA_DK:(h + 1) * GLA_DK] for h in range(heads)]
    rows.append(jnp.zeros((VT_PAD - VT_N, GLA_DK), F32))
    vt = jnp.concatenate(rows, axis=0).T

    def col(i):
        return vt[:, i:i + 1]

    for h in range(GLA_HEADS):
        v_row = va_ref[0, :, h * GLA_DV:(h + 1) * GLA_DV]
        s_new = col(VT_DEC + h) * sg_ref[0, h] + col(VT_KA + h) * v_row
        nsg_ref[0, h] = s_new
        oa_ref[0, :, h * GLA_DV:(h + 1) * GLA_DV] = jnp.sum(col(VT_QA + h) * s_new, axis=0, keepdims=True)
    for h in range(RET_HEADS):
        v_row = vb_ref[0, :, h * RET_DV:(h + 1) * RET_DV]
        gamma = jnp.exp(jnp.full((1, RET_DV), lg_ref[h], F32))
        s_new = gamma * sr_ref[0, h] + col(VT_KB + h) * v_row
        nsr_ref[0, h] = s_new
        ob_ref[0, :, h * RET_DV:(h + 1) * RET_DV] = jnp.sum(col(VT_QB + h) * s_new, axis=0, keepdims=True)


def _state_sample(lg, sg, sr, dec, ka, qa, kb, qb, va, vb):
    n = sg.shape[0]
    row = lambda b: (b, 0, 0)
    blk4 = lambda b: (b, 0, 0, 0)
    rows3 = lambda a: a.reshape(n, 1, a.shape[-1])
    vecs = [rows3(a) for a in (dec, ka, qa, kb, qb, va, vb)]
    return pl.pallas_call(
        _state_sample_kernel,
        grid=(n,),
        in_specs=[
            pl.BlockSpec(memory_space=pltpu.SMEM),
            pl.BlockSpec((1, GLA_HEADS, GLA_DK, GLA_DV), blk4),
            pl.BlockSpec((1, RET_HEADS, RET_DK, RET_DV), blk4),
            *[pl.BlockSpec((1, 1, a.shape[-1]), row) for a in vecs],
        ],
        out_specs=[
            pl.BlockSpec((1, GLA_HEADS, GLA_DK, GLA_DV), blk4),
            pl.BlockSpec((1, RET_HEADS, RET_DK, RET_DV), blk4),
            pl.BlockSpec((1, 1, GLA_V), row),
            pl.BlockSpec((1, 1, RET_V), row),
        ],
        out_shape=[
            jax.ShapeDtypeStruct(sg.shape, F32),
            jax.ShapeDtypeStruct(sr.shape, F32),
            jax.ShapeDtypeStruct((n, 1, GLA_V), F32),
            jax.ShapeDtypeStruct((n, 1, RET_V), F32),
        ],
        compiler_params=pltpu.CompilerParams(
            dimension_semantics=("arbitrary",), vmem_limit_bytes=VMEM_LIMIT),
        name="state_sample",
    )(lg, sg, sr, *vecs)


def _merge_sample_kernel(oa_ref, ob_ref, ga_ref, gb_ref, ma_ref, mb_ref, gn_ref, mg_ref):
    for h in range(GLA_HEADS):
        sl = slice(h * GLA_DV, (h + 1) * GLA_DV)
        g_a = ga_ref[:, sl]
        part_a = _sigmoid(ma_ref[:, sl]) * (_rms(oa_ref[:, sl]) * gn_ref[...] * (g_a * _sigmoid(g_a)))
        parts_b = []
        for j in range(RET_PER_GROUP):
            sb = slice(h * GW + j * RET_DV, h * GW + (j + 1) * RET_DV)
            g_b = gb_ref[:, sb]
            parts_b.append(_sigmoid(mb_ref[:, sb]) * (_rms(ob_ref[:, sb]) * (g_b * _sigmoid(g_b))))
        mg_ref[:, sl] = (part_a + jnp.concatenate(parts_b, axis=-1)).astype(mg_ref.dtype)


def _merge_sample(oa, ob, ga, gb, ma, mb, gn):
    n = oa.shape[0]
    return pl.pallas_call(
        _merge_sample_kernel,
        out_shape=jax.ShapeDtypeStruct((n, D_MODEL), BF16),
        name="merge_sample",
    )(oa, ob, ga, gb, ma, mb, gn)


def _rope_tables(pos):
    half = RET_DK // 2
    inv = 1.0 / (ROPE_BASE ** jnp.linspace(0.0, 1.0, half, dtype=jnp.float32))
    ang = pos[:, None] * inv[None, :]
    return jnp.cos(ang), jnp.sin(ang)


def kernel(x_prompt, x_sample, state_gla, state_ret, p_prompt, p_sample, norm_mix, w_in, w_gla_up, b_gla,
           gla_norm, w_out, norm_ple, w_ple_gate, w_ple_proj, norm_final):
    depth = w_in.shape[0]
    Bp, Lp, D = x_prompt.shape
    Bs, Ls, _ = x_sample.shape
    assert Ls == 1 and Lp % T_BLK == 0
    cos_p, sin_p = _rope_tables(jnp.arange(Lp, dtype=jnp.float32))
    cos_s, sin_s = _rope_tables(PAST_LEN + jnp.arange(Ls, dtype=jnp.float32))
    log_gamma = jnp.log(1.0 - jnp.exp2(-5.0 - jnp.arange(RET_HEADS, dtype=jnp.float32)))
    nfin = norm_final.reshape(1, D)

    hp = x_prompt
    hs = x_sample.reshape(Bs, D)
    gla_p, ret_p, gla_s, ret_s = [], [], [], []
    for i in range(depth):
        last = i == depth - 1
        nmix = norm_mix[i].reshape(1, D)
        nple = norm_ple[i].reshape(1, D)
        gn = gla_norm[i].reshape(1, GLA_DV)
        w_t = w_in[i].T
        w_lo = w_t[:R_START].astype(BF16)
        w_hi = w_t[HI_START:].astype(BF16)
        w_r = jnp.pad(w_t[R_START:HI_START], ((0, LANE - GLA_RANK), (0, 0))).astype(BF16)
        wup = jnp.pad(w_gla_up[i], ((0, LANE - GLA_RANK), (0, 0))).astype(BF16)
        wup = wup.reshape(LANE, GLA_HEADS, GLA_DK).transpose(1, 0, 2)
        bg = b_gla[i].reshape(GLA_HEADS, 1, GLA_DK)
        w_o = w_out[i].astype(BF16)
        w_pg = w_ple_gate[i].astype(BF16)
        w_pp = w_ple_proj[i].astype(BF16)

        merged, sg, sr = _mix_prompt(hp, log_gamma, nmix, w_lo, w_hi, w_r, wup, bg, gn, cos_p, sin_p)
        hp = _out_proj(hp.reshape(Bp * Lp, D), merged.reshape(Bp * Lp, D), p_prompt[i].reshape(Bp * Lp, PLE_DIM),
                       w_o, nple, w_pg, w_pp, nfin, last).reshape(Bp, Lp, D)
        gla_p.append(sg)
        ret_p.append(sr)

        z = dict(zip([n for n, _ in Z_SAMPLE],
                     _proj_sample(hs, nmix, w_lo, w_hi, w_r, wup, bg, cos_s, sin_s)))
        nsg, nsr, oa, ob = _state_sample(log_gamma, state_gla[i], state_ret[i], z["dec"], z["ka"], z["qa"],
                                         z["kb"], z["qb"], z["va"], z["vb"])
        merged_s = _merge_sample(oa.reshape(Bs, D), ob.reshape(Bs, D), z["ga"], z["gb"], z["ma"], z["mb"], gn)
        hs = _out_proj(hs, merged_s, p_sample[i].reshape(Bs, PLE_DIM), w_o, nple, w_pg, w_pp, nfin, last)
        gla_s.append(nsg)
        ret_s.append(nsr)

    return (hp, hs.reshape(Bs, Ls, D), jnp.stack(gla_p), jnp.stack(ret_p), jnp.stack(gla_s), jnp.stack(ret_s))
```

```python
import functools
import itertools

import jax
import jax.numpy as jnp
import numpy as np
from jax import lax
from jax.experimental import pallas as pl
from jax.experimental.pallas import tpu as pltpu

F32 = jnp.float32
BF16 = jnp.bfloat16

D_MODEL = 2048
PAST_LEN = 16384
PLE_DIM = 256
GLA_HEADS = 4
GLA_DK = 256
GLA_DV = 512
GLA_RANK = 16
GLA_TAU = 16.0
RET_HEADS = 8
RET_DK = 256
RET_DV = 256
ROPE_BASE = 10000.0
EPS = 1e-6

GLA_QK = GLA_HEADS * GLA_DK
GLA_V = GLA_HEADS * GLA_DV
RET_QK = RET_HEADS * RET_DK
RET_V = RET_HEADS * RET_DV
IN_SPLITS = (GLA_QK, GLA_QK, GLA_V, GLA_V, GLA_RANK, RET_QK, RET_QK, RET_V, RET_V, D_MODEL, D_MODEL)
IN_OFFS = tuple(int(v) for v in np.concatenate([[0], np.cumsum(IN_SPLITS)[:-1]]))

N_GROUPS = GLA_HEADS
RET_PER_GROUP = RET_HEADS // N_GROUPS
GW = GLA_DV
LANE = 128
SUBLANE = 8
ROW_TILE = 16

R_START = IN_OFFS[4]
W_PIECES = {
    "qa": (IN_OFFS[0], GLA_DK),
    "ka": (IN_OFFS[1], GLA_DK),
    "va": (IN_OFFS[2], GLA_DV),
    "ga": (IN_OFFS[3], GLA_DV),
    "qb": (IN_OFFS[5], GW),
    "kb": (IN_OFFS[6], GW),
    "vb": (IN_OFFS[7], GW),
    "gb": (IN_OFFS[8], GW),
    "ma": (IN_OFFS[9], GW),
    "mb": (IN_OFFS[10], GW),
}
W_NAMES = tuple(W_PIECES)

T_BLK = 256
C_GLA = 64
N_CHUNKS = T_BLK // C_GLA
TM_OUT = 512
TM_NORM = 1024
VMEM_LIMIT = 56 * 1024 * 1024


def _rms(x):
    return x * lax.rsqrt(jnp.mean(x * x, axis=-1, keepdims=True) + EPS)


def _sigmoid(x):
    return 1.0 / (1.0 + jnp.exp(-x))


def _log_sigmoid(x):
    return jnp.minimum(x, 0.0) - jnp.log(1.0 + jnp.exp(-jnp.abs(x)))


def _dot(a, b):
    return jnp.dot(a, b, preferred_element_type=F32)


def _dot_nt(a, b):
    return lax.dot_general(a, b, (((1,), (1,)), ((), ())), preferred_element_type=F32)


def _dot_tn(a, b):
    return lax.dot_general(a, b, (((0,), (0,)), ((), ())), preferred_element_type=F32)


def _rotary(x, cos, sin):
    half = x.shape[-1] // 2
    x1, x2 = x[:, :half], x[:, half:]
    return jnp.concatenate([x1 * cos - x2 * sin, x1 * sin + x2 * cos], axis=-1)


def _gla_log_alpha(r, wup_ref, bg_ref):
    pre = _dot(r.astype(BF16), wup_ref[0]) + bg_ref[0]
    return _log_sigmoid(pre) * (1.0 / GLA_TAU)


def _weight_specs(group_of, **kw):
    specs = []
    for n in W_NAMES:
        start, width = W_PIECES[n]

        def index_map(*idx, start=start, width=width):
            return (ROW_TILE * (start // ROW_TILE + (width // ROW_TILE) * group_of(*idx)), 0)

        assert start % ROW_TILE == 0 and width % ROW_TILE == 0
        specs.append(pl.BlockSpec((pl.Element(width), pl.Element(D_MODEL)), index_map, **kw))
    return specs


def _gate_code_spec():
    return pl.BlockSpec((pl.Element(GLA_RANK), pl.Element(D_MODEL)), lambda *idx: (R_START, 0))


def _gate_code(u, wr_ref):
    wr = jnp.concatenate([wr_ref[...], jnp.zeros((LANE - GLA_RANK, D_MODEL), BF16)], axis=0)
    return _dot_nt(u, wr)


N_W = len(W_NAMES)


Z_BUFFERS = (
    ("qx", (T_BLK, GLA_DK), BF16),
    ("kx", (T_BLK, GLA_DK), BF16),
    ("qi", (T_BLK, GLA_DK), BF16),
    ("ko", (T_BLK, GLA_DK), BF16),
    ("va", (T_BLK, GLA_DV), BF16),
    ("dl", (SUBLANE, GLA_DK), F32),
    ("gta", (T_BLK, GW), F32),
    ("qb", (RET_PER_GROUP, T_BLK, RET_DK), BF16),
    ("kb", (RET_PER_GROUP, T_BLK, RET_DK), BF16),
    ("qbi", (RET_PER_GROUP, T_BLK, RET_DK), BF16),
    ("kbo", (RET_PER_GROUP, T_BLK, RET_DK), BF16),
    ("vb", (RET_PER_GROUP, T_BLK, RET_DV), BF16),
    ("gtb", (T_BLK, GW), F32),
)
Z_NAMES = tuple(n for n, _, _ in Z_BUFFERS)
assert N_CHUNKS <= SUBLANE


def _project_block(u_ref, r_ref, w, wup_ref, bg_ref, cos_ref, sin_ref, lg_ref, g, z):
    T = T_BLK
    u = u_ref[0]

    def proj(name, j=0, n=None):
        ref = w[name]
        n = ref.shape[0] if n is None else n
        return _dot_nt(u, ref[j * n:(j + 1) * n, :])

    la = _gla_log_alpha(r_ref[0], wup_ref, bg_ref)
    yield
    q = proj("qa") * (GLA_DK ** -0.5)
    yield
    k = proj("ka")
    ri = lax.broadcasted_iota(jnp.int32, (C_GLA, C_GLA), 0)
    ci = lax.broadcasted_iota(jnp.int32, (C_GLA, C_GLA), 1)
    tri = jnp.where(ri >= ci, 1.0, 0.0).astype(BF16)
    for c in range(N_CHUNKS):
        sl = slice(c * C_GLA, (c + 1) * C_GLA)
        la_c = la[sl]
        la_hi = la_c.astype(BF16)
        la_lo = (la_c - la_hi.astype(F32)).astype(BF16)
        b = _dot(tri, la_hi) + _dot(tri, la_lo)
        b_mid = b[C_GLA // 2 - 1:C_GLA // 2]
        b_last = b[C_GLA - 1:C_GLA]
        q_c, k_c = q[sl], k[sl]
        z["qx"][sl] = (q_c * jnp.exp(b - b_mid)).astype(BF16)
        z["kx"][sl] = (k_c * jnp.exp(b_mid - b)).astype(BF16)
        z["qi"][sl] = (q_c * jnp.exp(b)).astype(BF16)
        z["ko"][sl] = (k_c * jnp.exp(b_last - b)).astype(BF16)
        z["dl"][c:c + 1] = jnp.exp(b_last)
    yield
    z["va"][...] = proj("va").astype(BF16)
    yield
    g_a = proj("ga")
    silu_a = g_a * _sigmoid(g_a)
    yield
    z["gta"][...] = _sigmoid(proj("ma")) * silu_a
    yield

    cos = cos_ref[...]
    sin = sin_ref[...]
    tl = lax.broadcasted_iota(jnp.int32, (T, LANE), 0).astype(F32)
    for j in range(RET_PER_GROUP):
        lg = lg_ref[g * RET_PER_GROUP + j]
        q_b = _rotary(proj("qb", j, RET_DK), cos, sin)
        k_b = _rotary(proj("kb", j, RET_DK), cos, sin) * (RET_DK ** -0.5)
        dec_in = jnp.exp((tl + 1.0) * lg)
        dec_out = jnp.exp((T - 1.0 - tl) * lg)
        z["qb"][j] = q_b.astype(BF16)
        z["kb"][j] = k_b.astype(BF16)
        z["qbi"][j] = (q_b * jnp.concatenate([dec_in, dec_in], axis=-1)).astype(BF16)
        z["kbo"][j] = (k_b * jnp.concatenate([dec_out, dec_out], axis=-1)).astype(BF16)
        yield
        z["vb"][j] = proj("vb", j, RET_DV).astype(BF16)
        g_b = proj("gb", j, RET_DV)
        yield
        z["gtb"][:, j * RET_DV:(j + 1) * RET_DV] = _sigmoid(proj("mb", j, RET_DV)) * (g_b * _sigmoid(g_b))
        yield


def _recur_block(z, lg_ref, g, gn_ref, sgt_ref, sret_ref, merged_ref):
    T = T_BLK
    ri = lax.broadcasted_iota(jnp.int32, (C_GLA, C_GLA), 0)
    ci = lax.broadcasted_iota(jnp.int32, (C_GLA, C_GLA), 1)
    causal = ri >= ci
    st = sgt_ref[...]
    o_chunks = []
    for c in range(N_CHUNKS):
        sl = slice(c * C_GLA, (c + 1) * C_GLA)
        v_c = z["va"][sl]
        a = jnp.where(causal, _dot_nt(z["qx"][sl], z["kx"][sl]), 0.0).astype(BF16)
        o_chunks.append(_dot_nt(z["qi"][sl], st.astype(BF16)) + _dot(a, v_c))
        st = st * z["dl"][c:c + 1] + _dot_tn(v_c, z["ko"][sl])
        yield
    sgt_ref[...] = st
    part_a = z["gta"][...] * (_rms(jnp.concatenate(o_chunks, axis=0)) * gn_ref[...])
    yield

    rt = lax.broadcasted_iota(jnp.int32, (T, T), 0)
    ct = lax.broadcasted_iota(jnp.int32, (T, T), 1)
    parts_b = []
    for j in range(RET_PER_GROUP):
        lg = lg_ref[g * RET_PER_GROUP + j]
        v_b = z["vb"][j]
        dmat = jnp.where(rt >= ct, jnp.exp((rt - ct).astype(F32) * lg), 0.0)
        a = (_dot_nt(z["qb"][j], z["kb"][j]) * dmat).astype(BF16)
        s = sret_ref[0, j]
        o_b = _rms(_dot(z["qbi"][j], s.astype(BF16)) + _dot(a, v_b))
        dec_all = jnp.exp(jnp.full((1, RET_DV), T * lg, F32))
        sret_ref[0, j] = s * dec_all + _dot_tn(z["kbo"][j], v_b)
        parts_b.append(z["gtb"][:, j * RET_DV:(j + 1) * RET_DV] * o_b)
        yield
    merged_ref[0] = (part_a + jnp.concatenate(parts_b, axis=-1)).astype(merged_ref.dtype)


def _mix_prompt_kernel(lg_ref, u_ref, r_ref, *refs, blocks_per_group, blocks_per_seq):
    nw, nz = len(W_NAMES), len(Z_NAMES)
    w = dict(zip(W_NAMES, refs[:nw]))
    wup_ref, bg_ref, gn_ref, cos_ref, sin_ref, merged_ref, sgla_ref, sret_ref, sgt_ref = refs[nw:nw + 9]
    z_even = dict(zip(Z_NAMES, refs[nw + 9:nw + 9 + nz]))
    z_odd = dict(zip(Z_NAMES, refs[nw + 9 + nz:]))
    s = pl.program_id(0)
    n_blocks = pl.num_programs(0) - 1
    g_p = jnp.minimum(s, n_blocks - 1) // blocks_per_group
    r = jnp.maximum(s - 1, 0)
    g_r = r // blocks_per_group
    t_r = r % blocks_per_seq

    @pl.when(s == 0)
    def _():
        for ref in z_odd.values():
            ref[...] = jnp.zeros_like(ref)

    @pl.when(t_r == 0)
    def _():
        sgt_ref[...] = jnp.zeros_like(sgt_ref)
        sret_ref[...] = jnp.zeros_like(sret_ref)

    def step(z_write, z_read):
        rec = _recur_block(z_read, lg_ref, g_r, gn_ref, sgt_ref, sret_ref, merged_ref)
        prj = _project_block(u_ref, r_ref, w, wup_ref, bg_ref, cos_ref, sin_ref, lg_ref, g_p, z_write)
        for _ in itertools.zip_longest(prj, rec):
            pass

    @pl.when(s % 2 == 0)
    def _():
        step(z_even, z_odd)

    @pl.when(s % 2 == 1)
    def _():
        step(z_odd, z_even)

    @pl.when((t_r == blocks_per_seq - 1) & (s > 0))
    def _():
        sgla_ref[0, 0] = sgt_ref[...].T


def _norm_kernel(x_ref, g_ref, wr_ref, u_ref, r_ref):
    u = (_rms(x_ref[...]) * g_ref[...]).astype(u_ref.dtype)
    u_ref[...] = u
    r_ref[...] = _gate_code(u, wr_ref)


def _norm_prompt(x, gain, w_t):
    n, D = x.shape
    tm = min(TM_NORM, n)
    return pl.pallas_call(
        _norm_kernel,
        grid=(n // tm,),
        in_specs=[pl.BlockSpec((tm, D), lambda i: (i, 0)), pl.BlockSpec((1, D), lambda i: (0, 0)),
                  _gate_code_spec()],
        out_specs=[pl.BlockSpec((tm, D), lambda i: (i, 0)), pl.BlockSpec((tm, LANE), lambda i: (i, 0))],
        out_shape=[jax.ShapeDtypeStruct((n, D), BF16), jax.ShapeDtypeStruct((n, LANE), F32)],
        compiler_params=pltpu.CompilerParams(
            dimension_semantics=("arbitrary",), vmem_limit_bytes=VMEM_LIMIT),
        name="norm_prompt",
    )(x, gain, w_t)


def _mix_prompt(u, r, lg, w_t, wup, bg, gn, cos, sin):
    B, L, D = u.shape
    nt = L // T_BLK
    n_blocks = N_GROUPS * B * nt

    def proj_idx(s):
        p = jnp.minimum(s, n_blocks - 1)
        return p // (B * nt), (p // nt) % B, p % nt

    def recur_idx(s):
        r = jnp.maximum(s - 1, 0)
        return r // (B * nt), (r // nt) % B, r % nt

    def out_map(s):
        g, b, t = recur_idx(s)
        return (b, t, g)

    def state_map(s):
        g, b, _ = recur_idx(s)
        return (b, g, 0, 0)

    z_scratch = [pltpu.VMEM(shape, dtype) for _, shape, dtype in Z_BUFFERS]
    return pl.pallas_call(
        functools.partial(_mix_prompt_kernel, blocks_per_group=B * nt, blocks_per_seq=nt),
        grid=(n_blocks + 1,),
        in_specs=[
            pl.BlockSpec(memory_space=pltpu.SMEM),
            pl.BlockSpec((1, T_BLK, D), lambda s: (proj_idx(s)[1], proj_idx(s)[2], 0)),
            pl.BlockSpec((1, T_BLK, LANE), lambda s: (proj_idx(s)[1], proj_idx(s)[2], 0)),
            *_weight_specs(lambda s: proj_idx(s)[0], pipeline_mode=pl.Buffered(1)),
            pl.BlockSpec((1, LANE, GLA_DK), lambda s: (proj_idx(s)[0], 0, 0)),
            pl.BlockSpec((1, 1, GLA_DK), lambda s: (proj_idx(s)[0], 0, 0)),
            pl.BlockSpec((1, GLA_DV), lambda s: (0, 0)),
            pl.BlockSpec((T_BLK, LANE), lambda s: (proj_idx(s)[2], 0)),
            pl.BlockSpec((T_BLK, LANE), lambda s: (proj_idx(s)[2], 0)),
        ],
        out_specs=[
            pl.BlockSpec((1, T_BLK, GW), out_map),
            pl.BlockSpec((1, 1, GLA_DK, GLA_DV), state_map),
            pl.BlockSpec((1, RET_PER_GROUP, RET_DK, RET_DV), state_map),
        ],
        out_shape=[
            jax.ShapeDtypeStruct((B, L, D_MODEL), BF16),
            jax.ShapeDtypeStruct((B, GLA_HEADS, GLA_DK, GLA_DV), F32),
            jax.ShapeDtypeStruct((B, RET_HEADS, RET_DK, RET_DV), F32),
        ],
        scratch_shapes=[pltpu.VMEM((GLA_DV, GLA_DK), F32)] + z_scratch + z_scratch,
        compiler_params=pltpu.CompilerParams(
            dimension_semantics=("arbitrary",), vmem_limit_bytes=VMEM_LIMIT),
        name="mix_prompt",
    )(lg, u, r, *([w_t] * N_W), wup, bg, gn, cos, sin)


def _out_kernel(x_ref, mg_ref, p_ref, wout_ref, nple_ref, wpg_ref, wpp_ref, nfin_ref, y_ref, *,
                final_norm):
    h = x_ref[...] + _dot(mg_ref[...], wout_ref[...])
    hn = (_rms(h) * nple_ref[...]).astype(BF16)
    gate = _sigmoid(_dot(hn, wpg_ref[...]))
    h = h + gate * _dot(p_ref[...].astype(BF16), wpp_ref[...])
    if final_norm:
        h = _rms(h) * nfin_ref[...]
    y_ref[...] = h


def _out_proj(x, merged, p, w_out, nple, w_pg, w_pp, nfin, final_norm):
    n, D = x.shape
    tm = min(TM_OUT, n)
    const = lambda i: (0, 0)
    return pl.pallas_call(
        functools.partial(_out_kernel, final_norm=final_norm),
        grid=(n // tm,),
        in_specs=[
            pl.BlockSpec((tm, D), lambda i: (i, 0)),
            pl.BlockSpec((tm, D), lambda i: (i, 0)),
            pl.BlockSpec((tm, PLE_DIM), lambda i: (i, 0)),
            pl.BlockSpec((D, D), const, pipeline_mode=pl.Buffered(1)),
            pl.BlockSpec((1, D), const),
            pl.BlockSpec((D, D), const, pipeline_mode=pl.Buffered(1)),
            pl.BlockSpec((PLE_DIM, D), const, pipeline_mode=pl.Buffered(1)),
            pl.BlockSpec((1, D), const),
        ],
        out_specs=pl.BlockSpec((tm, D), lambda i: (i, 0)),
        out_shape=jax.ShapeDtypeStruct((n, D), F32),
        compiler_params=pltpu.CompilerParams(
            dimension_semantics=("arbitrary",), vmem_limit_bytes=VMEM_LIMIT),
        name="out_proj",
    )(x, merged, p, w_out, nple, w_pg, w_pp, nfin)


Z_SAMPLE = (("dec", GLA_DK), ("qa", GLA_DK), ("ka", GLA_DK), ("va", GLA_DV), ("ga", GLA_DV), ("ma", GW),
            ("qb", GW), ("kb", GW), ("vb", GW), ("gb", GW), ("mb", GW))


def _proj_sample_kernel(x_ref, nmix_ref, *refs):
    w = dict(zip(W_NAMES, refs[:len(W_NAMES)]))
    wr_ref, wup_ref, bg_ref, cos_ref, sin_ref = refs[len(W_NAMES):len(W_NAMES) + 5]
    out = dict(zip([n for n, _ in Z_SAMPLE], refs[len(W_NAMES) + 5:]))
    u = (_rms(x_ref[...]) * nmix_ref[...]).astype(BF16)

    def proj(name):
        return _dot_nt(u, w[name][...])

    out["dec"][...] = jnp.exp(_gla_log_alpha(_gate_code(u, wr_ref), wup_ref, bg_ref))
    out["qa"][...] = proj("qa") * (GLA_DK ** -0.5)
    for name in ("ka", "va", "ga", "ma", "vb", "gb", "mb"):
        out[name][...] = proj(name)
    cos = cos_ref[...]
    sin = sin_ref[...]
    q_b = proj("qb")
    k_b = proj("kb")
    for j in range(RET_PER_GROUP):
        sl = slice(j * RET_DK, (j + 1) * RET_DK)
        out["qb"][:, sl] = _rotary(q_b[:, sl], cos, sin)
        out["kb"][:, sl] = _rotary(k_b[:, sl], cos, sin) * (RET_DK ** -0.5)


def _proj_sample(x, nmix, w_t, wup, bg, cos, sin):
    n, D = x.shape
    return pl.pallas_call(
        _proj_sample_kernel,
        grid=(N_GROUPS,),
        in_specs=[
            pl.BlockSpec((n, D), lambda g: (0, 0)),
            pl.BlockSpec((1, D), lambda g: (0, 0)),
            *_weight_specs(lambda g: g),
            _gate_code_spec(),
            pl.BlockSpec((1, LANE, GLA_DK), lambda g: (g, 0, 0)),
            pl.BlockSpec((1, 1, GLA_DK), lambda g: (g, 0, 0)),
            pl.BlockSpec((1, LANE), lambda g: (0, 0)),
            pl.BlockSpec((1, LANE), lambda g: (0, 0)),
        ],
        out_specs=[pl.BlockSpec((n, width), lambda g: (0, g)) for _, width in Z_SAMPLE],
        out_shape=[jax.ShapeDtypeStruct((n, N_GROUPS * width), F32) for _, width in Z_SAMPLE],
        compiler_params=pltpu.CompilerParams(
            dimension_semantics=("arbitrary",), vmem_limit_bytes=VMEM_LIMIT),
        name="proj_sample",
    )(x, nmix, *([w_t] * (N_W + 1)), wup, bg, cos, sin)


VT_DEC = 0
VT_KA = VT_DEC + GLA_HEADS
VT_QA = VT_KA + GLA_HEADS
VT_KB = VT_QA + GLA_HEADS
VT_QB = VT_KB + RET_HEADS
VT_N = VT_QB + RET_HEADS
VT_PAD = 32


def _state_sample_kernel(lg_ref, sg_ref, sr_ref, dec_ref, ka_ref, qa_ref, kb_ref, qb_ref, va_ref, vb_ref,
                         nsg_ref, nsr_ref, oa_ref, ob_ref):
    rows = []
    for ref, heads in ((dec_ref, GLA_HEADS), (ka_ref, GLA_HEADS), (qa_ref, GLA_HEADS),
                       (kb_ref, RET_HEADS), (qb_ref, RET_HEADS)):
        rows += [ref[0, :, h * GLA_DK:(h + 1) * GLA_DK] for h in range(heads)]
    rows.append(jnp.zeros((VT_PAD - VT_N, GLA_DK), F32))
    vt = jnp.concatenate(rows, axis=0).T

    def col(i):
        return vt[:, i:i + 1]

    for h in range(GLA_HEADS):
        v_row = va_ref[0, :, h * GLA_DV:(h + 1) * GLA_DV]
        s_new = col(VT_DEC + h) * sg_ref[0, h] + col(VT_KA + h) * v_row
        nsg_ref[0, h] = s_new
        oa_ref[0, :, h * GLA_DV:(h + 1) * GLA_DV] = jnp.sum(col(VT_QA + h) * s_new, axis=0, keepdims=True)
    for h in range(RET_HEADS):
        v_row = vb_ref[0, :, h * RET_DV:(h + 1) * RET_DV]
        gamma = jnp.exp(jnp.full((1, RET_DV), lg_ref[h], F32))
        s_new = gamma * sr_ref[0, h] + col(VT_KB + h) * v_row
        nsr_ref[0, h] = s_new
        ob_ref[0, :, h * RET_DV:(h + 1) * RET_DV] = jnp.sum(col(VT_QB + h) * s_new, axis=0, keepdims=True)


def _state_sample(lg, sg, sr, dec, ka, qa, kb, qb, va, vb):
    n = sg.shape[0]
    row = lambda b: (b, 0, 0)
    blk4 = lambda b: (b, 0, 0, 0)
    rows3 = lambda a: a.reshape(n, 1, a.shape[-1])
    vecs = [rows3(a) for a in (dec, ka, qa, kb, qb, va, vb)]
    return pl.pallas_call(
        _state_sample_kernel,
        grid=(n,),
        in_specs=[
            pl.BlockSpec(memory_space=pltpu.SMEM),
            pl.BlockSpec((1, GLA_HEADS, GLA_DK, GLA_DV), blk4),
            pl.BlockSpec((1, RET_HEADS, RET_DK, RET_DV), blk4),
            *[pl.BlockSpec((1, 1, a.shape[-1]), row) for a in vecs],
        ],
        out_specs=[
            pl.BlockSpec((1, GLA_HEADS, GLA_DK, GLA_DV), blk4),
            pl.BlockSpec((1, RET_HEADS, RET_DK, RET_DV), blk4),
            pl.BlockSpec((1, 1, GLA_V), row),
            pl.BlockSpec((1, 1, RET_V), row),
        ],
        out_shape=[
            jax.ShapeDtypeStruct(sg.shape, F32),
            jax.ShapeDtypeStruct(sr.shape, F32),
            jax.ShapeDtypeStruct((n, 1, GLA_V), F32),
            jax.ShapeDtypeStruct((n, 1, RET_V), F32),
        ],
        compiler_params=pltpu.CompilerParams(
            dimension_semantics=("arbitrary",), vmem_limit_bytes=VMEM_LIMIT),
        name="state_sample",
    )(lg, sg, sr, *vecs)


def _merge_sample_kernel(oa_ref, ob_ref, ga_ref, gb_ref, ma_ref, mb_ref, gn_ref, mg_ref):
    for h in range(GLA_HEADS):
        sl = slice(h * GLA_DV, (h + 1) * GLA_DV)
        g_a = ga_ref[:, sl]
        part_a = _sigmoid(ma_ref[:, sl]) * (_rms(oa_ref[:, sl]) * gn_ref[...] * (g_a * _sigmoid(g_a)))
        parts_b = []
        for j in range(RET_PER_GROUP):
            sb = slice(h * GW + j * RET_DV, h * GW + (j + 1) * RET_DV)
            g_b = gb_ref[:, sb]
            parts_b.append(_sigmoid(mb_ref[:, sb]) * (_rms(ob_ref[:, sb]) * (g_b * _sigmoid(g_b))))
        mg_ref[:, sl] = (part_a + jnp.concatenate(parts_b, axis=-1)).astype(mg_ref.dtype)


def _merge_sample(oa, ob, ga, gb, ma, mb, gn):
    n = oa.shape[0]
    return pl.pallas_call(
        _merge_sample_kernel,
        out_shape=jax.ShapeDtypeStruct((n, D_MODEL), BF16),
        name="merge_sample",
    )(oa, ob, ga, gb, ma, mb, gn)


def _rope_tables(pos):
    half = RET_DK // 2
    inv = 1.0 / (ROPE_BASE ** jnp.linspace(0.0, 1.0, half, dtype=jnp.float32))
    ang = pos[:, None] * inv[None, :]
    return jnp.cos(ang), jnp.sin(ang)


def kernel(x_prompt, x_sample, state_gla, state_ret, p_prompt, p_sample, norm_mix, w_in, w_gla_up, b_gla,
           gla_norm, w_out, norm_ple, w_ple_gate, w_ple_proj, norm_final):
    depth = w_in.shape[0]
    Bp, Lp, D = x_prompt.shape
    Bs, Ls, _ = x_sample.shape
    assert Ls == 1 and Lp % T_BLK == 0
    cos_p, sin_p = _rope_tables(jnp.arange(Lp, dtype=jnp.float32))
    cos_s, sin_s = _rope_tables(PAST_LEN + jnp.arange(Ls, dtype=jnp.float32))
    log_gamma = jnp.log(1.0 - jnp.exp2(-5.0 - jnp.arange(RET_HEADS, dtype=jnp.float32)))
    nfin = norm_final.reshape(1, D)

    hp = x_prompt
    hs = x_sample.reshape(Bs, D)
    gla_p, ret_p, gla_s, ret_s = [], [], [], []
    for i in range(depth):
        last = i == depth - 1
        nmix = norm_mix[i].reshape(1, D)
        nple = norm_ple[i].reshape(1, D)
        gn = gla_norm[i].reshape(1, GLA_DV)
        w_t = w_in[i].T.astype(BF16)
        wup = jnp.pad(w_gla_up[i], ((0, LANE - GLA_RANK), (0, 0))).astype(BF16)
        wup = wup.reshape(LANE, GLA_HEADS, GLA_DK).transpose(1, 0, 2)
        bg = b_gla[i].reshape(GLA_HEADS, 1, GLA_DK)
        w_o = w_out[i].astype(BF16)
        w_pg = w_ple_gate[i].astype(BF16)
        w_pp = w_ple_proj[i].astype(BF16)

        u, r = _norm_prompt(hp.reshape(Bp * Lp, D), nmix, w_t)
        merged, sg, sr = _mix_prompt(u.reshape(Bp, Lp, D), r.reshape(Bp, Lp, LANE), log_gamma, w_t, wup, bg, gn,
                                     cos_p, sin_p)
        hp = _out_proj(hp.reshape(Bp * Lp, D), merged.reshape(Bp * Lp, D), p_prompt[i].reshape(Bp * Lp, PLE_DIM),
                       w_o, nple, w_pg, w_pp, nfin, last).reshape(Bp, Lp, D)
        gla_p.append(sg)
        ret_p.append(sr)

        z = dict(zip([n for n, _ in Z_SAMPLE],
                     _proj_sample(hs, nmix, w_t, wup, bg, cos_s, sin_s)))
        nsg, nsr, oa, ob = _state_sample(log_gamma, state_gla[i], state_ret[i], z["dec"], z["ka"], z["qa"],
                                         z["kb"], z["qb"], z["va"], z["vb"])
        merged_s = _merge_sample(oa.reshape(Bs, D), ob.reshape(Bs, D), z["ga"], z["gb"], z["ma"], z["mb"], gn)
        hs = _out_proj(hs, merged_s, p_sample[i].reshape(Bs, PLE_DIM), w_o, nple, w_pg, w_pp, nfin, last)
        gla_s.append(nsg)
        ret_s.append(nsr)

    return (hp, hs.reshape(Bs, Ls, D), jnp.stack(gla_p), jnp.stack(ret_p), jnp.stack(gla_s), jnp.stack(ret_s))
```

```python
import functools
import itertools

import jax
import jax.numpy as jnp
import numpy as np
from jax import lax
from jax.experimental import pallas as pl
from jax.experimental.pallas import tpu as pltpu

F32 = jnp.float32
BF16 = jnp.bfloat16

D_MODEL = 2048
PAST_LEN = 16384
PLE_DIM = 256
GLA_HEADS = 4
GLA_DK = 256
GLA_DV = 512
GLA_RANK = 16
GLA_TAU = 16.0
RET_HEADS = 8
RET_DK = 256
RET_DV = 256
ROPE_BASE = 10000.0
EPS = 1e-6

GLA_QK = GLA_HEADS * GLA_DK
GLA_V = GLA_HEADS * GLA_DV
RET_QK = RET_HEADS * RET_DK
RET_V = RET_HEADS * RET_DV
IN_SPLITS = (GLA_QK, GLA_QK, GLA_V, GLA_V, GLA_RANK, RET_QK, RET_QK, RET_V, RET_V, D_MODEL, D_MODEL)
IN_OFFS = tuple(int(v) for v in np.concatenate([[0], np.cumsum(IN_SPLITS)[:-1]]))

N_GROUPS = GLA_HEADS
RET_PER_GROUP = RET_HEADS // N_GROUPS
GW = GLA_DV
LANE = 128
SUBLANE = 8
ROW_TILE = 16

R_START = IN_OFFS[4]
W_PIECES = {
    "qa": (IN_OFFS[0], GLA_DK),
    "ka": (IN_OFFS[1], GLA_DK),
    "va": (IN_OFFS[2], GLA_DV),
    "ga": (IN_OFFS[3], GLA_DV),
    "qb": (IN_OFFS[5], GW),
    "kb": (IN_OFFS[6], GW),
    "vb": (IN_OFFS[7], GW),
    "gb": (IN_OFFS[8], GW),
    "ma": (IN_OFFS[9], GW),
    "mb": (IN_OFFS[10], GW),
}
W_NAMES = tuple(W_PIECES)

T_BLK = 256
C_GLA = 64
N_CHUNKS = T_BLK // C_GLA
TM_OUT = 512
TM_NORM = 1024
VMEM_LIMIT = 56 * 1024 * 1024


def _rms(x):
    return x * lax.rsqrt(jnp.mean(x * x, axis=-1, keepdims=True) + EPS)


def _sigmoid(x):
    return 1.0 / (1.0 + jnp.exp(-x))


def _log_sigmoid(x):
    return jnp.minimum(x, 0.0) - jnp.log(1.0 + jnp.exp(-jnp.abs(x)))


def _dot(a, b):
    return jnp.dot(a, b, preferred_element_type=F32)


def _dot_nt(a, b):
    return lax.dot_general(a, b, (((1,), (1,)), ((), ())), preferred_element_type=F32)


def _dot_tn(a, b):
    return lax.dot_general(a, b, (((0,), (0,)), ((), ())), preferred_element_type=F32)


def _rotary(x, cos, sin):
    half = x.shape[-1] // 2
    x1, x2 = x[:, :half], x[:, half:]
    return jnp.concatenate([x1 * cos - x2 * sin, x1 * sin + x2 * cos], axis=-1)


def _gla_log_alpha(r, wup_ref, bg_ref):
    pre = _dot(r.astype(BF16), wup_ref[0]) + bg_ref[0]
    return _log_sigmoid(pre) * (1.0 / GLA_TAU)


def _weight_specs(group_of, **kw):
    specs = []
    for n in W_NAMES:
        start, width = W_PIECES[n]

        def index_map(*idx, start=start, width=width):
            return (ROW_TILE * (start // ROW_TILE + (width // ROW_TILE) * group_of(*idx)), 0)

        assert start % ROW_TILE == 0 and width % ROW_TILE == 0
        specs.append(pl.BlockSpec((pl.Element(width), pl.Element(D_MODEL)), index_map, **kw))
    return specs


def _gate_code_spec():
    return pl.BlockSpec((pl.Element(GLA_RANK), pl.Element(D_MODEL)), lambda *idx: (R_START, 0))


def _gate_code(u, wr_ref):
    wr = jnp.concatenate([wr_ref[...], jnp.zeros((LANE - GLA_RANK, D_MODEL), BF16)], axis=0)
    return _dot_nt(u, wr)


N_W = len(W_NAMES)


Z_BUFFERS = (
    ("qx", (T_BLK, GLA_DK), BF16),
    ("kx", (T_BLK, GLA_DK), BF16),
    ("qi", (T_BLK, GLA_DK), BF16),
    ("ko", (T_BLK, GLA_DK), BF16),
    ("va", (T_BLK, GLA_DV), BF16),
    ("dl", (SUBLANE, GLA_DK), F32),
    ("gta", (T_BLK, GW), F32),
    ("qb", (RET_PER_GROUP, T_BLK, RET_DK), BF16),
    ("kb", (RET_PER_GROUP, T_BLK, RET_DK), BF16),
    ("qbi", (RET_PER_GROUP, T_BLK, RET_DK), BF16),
    ("kbo", (RET_PER_GROUP, T_BLK, RET_DK), BF16),
    ("vb", (RET_PER_GROUP, T_BLK, RET_DV), BF16),
    ("gtb", (T_BLK, GW), F32),
)
Z_NAMES = tuple(n for n, _, _ in Z_BUFFERS)
assert N_CHUNKS <= SUBLANE


VT_DEC = 0
VT_KA = VT_DEC + GLA_HEADS
VT_QA = VT_KA + GLA_HEADS
VT_KB = VT_QA + GLA_HEADS
VT_QB = VT_KB + RET_HEADS
VT_N = VT_QB + RET_HEADS
VT_PAD = 32


def _state_update(lg_ref, sg_ref, sr_ref, dec_ref, ka_ref, qa_ref, kb_ref, qb_ref, va_ref, vb_ref,
                  nsg_ref, nsr_ref, oa_ref, ob_ref):
    rows = []
    for ref, heads in ((dec_ref, GLA_HEADS), (ka_ref, GLA_HEADS), (qa_ref, GLA_HEADS),
                       (kb_ref, RET_HEADS), (qb_ref, RET_HEADS)):
        rows += [ref[0, :, h * GLA_DK:(h + 1) * GLA_DK] for h in range(heads)]
    rows.append(jnp.zeros((VT_PAD - VT_N, GLA_DK), F32))
    vt = jnp.concatenate(rows, axis=0).T

    def col(i):
        return vt[:, i:i + 1]

    for h in range(GLA_HEADS):
        v_row = va_ref[0, :, h * GLA_DV:(h + 1) * GLA_DV]
        s_new = col(VT_DEC + h) * sg_ref[0, h] + col(VT_KA + h) * v_row
        nsg_ref[0, h] = s_new
        oa_ref[0, :, h * GLA_DV:(h + 1) * GLA_DV] = jnp.sum(col(VT_QA + h) * s_new, axis=0, keepdims=True)
        yield
    for h in range(RET_HEADS):
        v_row = vb_ref[0, :, h * RET_DV:(h + 1) * RET_DV]
        gamma = jnp.exp(jnp.full((1, RET_DV), lg_ref[h], F32))
        s_new = gamma * sr_ref[0, h] + col(VT_KB + h) * v_row
        nsr_ref[0, h] = s_new
        ob_ref[0, :, h * RET_DV:(h + 1) * RET_DV] = jnp.sum(col(VT_QB + h) * s_new, axis=0, keepdims=True)
        yield


N_SAMPLE_IN = 9
N_SAMPLE_OUT = 4


def _project_block(u_ref, r_ref, w, wup_ref, bg_ref, cos_ref, sin_ref, lg_ref, g, z):
    T = T_BLK
    u = u_ref[0]

    def proj(name, j=0, n=None):
        ref = w[name]
        n = ref.shape[0] if n is None else n
        return _dot_nt(u, ref[j * n:(j + 1) * n, :])

    la = _gla_log_alpha(r_ref[0], wup_ref, bg_ref)
    yield
    q = proj("qa") * (GLA_DK ** -0.5)
    yield
    k = proj("ka")
    ri = lax.broadcasted_iota(jnp.int32, (C_GLA, C_GLA), 0)
    ci = lax.broadcasted_iota(jnp.int32, (C_GLA, C_GLA), 1)
    tri = jnp.where(ri >= ci, 1.0, 0.0).astype(BF16)
    for c in range(N_CHUNKS):
        sl = slice(c * C_GLA, (c + 1) * C_GLA)
        la_c = la[sl]
        la_hi = la_c.astype(BF16)
        la_lo = (la_c - la_hi.astype(F32)).astype(BF16)
        b = _dot(tri, la_hi) + _dot(tri, la_lo)
        b_mid = b[C_GLA // 2 - 1:C_GLA // 2]
        b_last = b[C_GLA - 1:C_GLA]
        q_c, k_c = q[sl], k[sl]
        z["qx"][sl] = (q_c * jnp.exp(b - b_mid)).astype(BF16)
        z["kx"][sl] = (k_c * jnp.exp(b_mid - b)).astype(BF16)
        z["qi"][sl] = (q_c * jnp.exp(b)).astype(BF16)
        z["ko"][sl] = (k_c * jnp.exp(b_last - b)).astype(BF16)
        z["dl"][c:c + 1] = jnp.exp(b_last)
    yield
    z["va"][...] = proj("va").astype(BF16)
    yield
    g_a = proj("ga")
    silu_a = g_a * _sigmoid(g_a)
    yield
    z["gta"][...] = _sigmoid(proj("ma")) * silu_a
    yield

    cos = cos_ref[...]
    sin = sin_ref[...]
    tl = lax.broadcasted_iota(jnp.int32, (T, LANE), 0).astype(F32)
    for j in range(RET_PER_GROUP):
        lg = lg_ref[g * RET_PER_GROUP + j]
        q_b = _rotary(proj("qb", j, RET_DK), cos, sin)
        k_b = _rotary(proj("kb", j, RET_DK), cos, sin) * (RET_DK ** -0.5)
        dec_in = jnp.exp((tl + 1.0) * lg)
        dec_out = jnp.exp((T - 1.0 - tl) * lg)
        z["qb"][j] = q_b.astype(BF16)
        z["kb"][j] = k_b.astype(BF16)
        z["qbi"][j] = (q_b * jnp.concatenate([dec_in, dec_in], axis=-1)).astype(BF16)
        z["kbo"][j] = (k_b * jnp.concatenate([dec_out, dec_out], axis=-1)).astype(BF16)
        yield
        z["vb"][j] = proj("vb", j, RET_DV).astype(BF16)
        g_b = proj("gb", j, RET_DV)
        yield
        z["gtb"][:, j * RET_DV:(j + 1) * RET_DV] = _sigmoid(proj("mb", j, RET_DV)) * (g_b * _sigmoid(g_b))
        yield


def _recur_block(z, lg_ref, g, gn_ref, sgt_ref, sret_ref, merged_ref):
    T = T_BLK
    ri = lax.broadcasted_iota(jnp.int32, (C_GLA, C_GLA), 0)
    ci = lax.broadcasted_iota(jnp.int32, (C_GLA, C_GLA), 1)
    causal = ri >= ci
    st = sgt_ref[...]
    o_chunks = []
    for c in range(N_CHUNKS):
        sl = slice(c * C_GLA, (c + 1) * C_GLA)
        v_c = z["va"][sl]
        a = jnp.where(causal, _dot_nt(z["qx"][sl], z["kx"][sl]), 0.0).astype(BF16)
        o_chunks.append(_dot_nt(z["qi"][sl], st.astype(BF16)) + _dot(a, v_c))
        st = st * z["dl"][c:c + 1] + _dot_tn(v_c, z["ko"][sl])
        yield
    sgt_ref[...] = st
    part_a = z["gta"][...] * (_rms(jnp.concatenate(o_chunks, axis=0)) * gn_ref[...])
    yield

    rt = lax.broadcasted_iota(jnp.int32, (T, T), 0)
    ct = lax.broadcasted_iota(jnp.int32, (T, T), 1)
    parts_b = []
    for j in range(RET_PER_GROUP):
        lg = lg_ref[g * RET_PER_GROUP + j]
        v_b = z["vb"][j]
        dmat = jnp.where(rt >= ct, jnp.exp((rt - ct).astype(F32) * lg), 0.0)
        a = (_dot_nt(z["qb"][j], z["kb"][j]) * dmat).astype(BF16)
        s = sret_ref[0, j]
        o_b = _rms(_dot(z["qbi"][j], s.astype(BF16)) + _dot(a, v_b))
        dec_all = jnp.exp(jnp.full((1, RET_DV), T * lg, F32))
        sret_ref[0, j] = s * dec_all + _dot_tn(z["kbo"][j], v_b)
        parts_b.append(z["gtb"][:, j * RET_DV:(j + 1) * RET_DV] * o_b)
        yield
    merged_ref[0] = (part_a + jnp.concatenate(parts_b, axis=-1)).astype(merged_ref.dtype)


def _mix_prompt_kernel(lg_ref, u_ref, r_ref, *refs, blocks_per_group, blocks_per_seq):
    nw, nz = len(W_NAMES), len(Z_NAMES)
    w = dict(zip(W_NAMES, refs[:nw]))
    refs = list(refs[nw:])
    wup_ref, bg_ref, gn_ref, cos_ref, sin_ref = refs[:5]
    sample_in = refs[5:5 + N_SAMPLE_IN]
    refs = refs[5 + N_SAMPLE_IN:]
    merged_ref, sgla_ref, sret_ref = refs[:3]
    sample_out = refs[3:3 + N_SAMPLE_OUT]
    sgt_ref = refs[3 + N_SAMPLE_OUT]
    refs = refs[4 + N_SAMPLE_OUT:]
    z_even = dict(zip(Z_NAMES, refs[:nz]))
    z_odd = dict(zip(Z_NAMES, refs[nz:]))
    s = pl.program_id(0)
    n_blocks = pl.num_programs(0) - 1
    g_p = jnp.minimum(s, n_blocks - 1) // blocks_per_group
    r = jnp.maximum(s - 1, 0)
    g_r = r // blocks_per_group
    t_r = r % blocks_per_seq

    @pl.when(s == 0)
    def _():
        for ref in z_odd.values():
            ref[...] = jnp.zeros_like(ref)

    @pl.when(t_r == 0)
    def _():
        sgt_ref[...] = jnp.zeros_like(sgt_ref)
        sret_ref[...] = jnp.zeros_like(sret_ref)

    def step(z_write, z_read):
        rec = _recur_block(z_read, lg_ref, g_r, gn_ref, sgt_ref, sret_ref, merged_ref)
        prj = _project_block(u_ref, r_ref, w, wup_ref, bg_ref, cos_ref, sin_ref, lg_ref, g_p, z_write)
        upd = _state_update(lg_ref, *sample_in, *sample_out)
        for _ in itertools.zip_longest(prj, rec, upd):
            pass

    @pl.when(s % 2 == 0)
    def _():
        step(z_even, z_odd)

    @pl.when(s % 2 == 1)
    def _():
        step(z_odd, z_even)

    @pl.when((t_r == blocks_per_seq - 1) & (s > 0))
    def _():
        sgla_ref[0, 0] = sgt_ref[...].T


def _norm_kernel(x_ref, g_ref, wr_ref, u_ref, r_ref):
    u = (_rms(x_ref[...]) * g_ref[...]).astype(u_ref.dtype)
    u_ref[...] = u
    r_ref[...] = _gate_code(u, wr_ref)


def _norm_prompt(x, gain, w_t):
    n, D = x.shape
    tm = min(TM_NORM, n)
    return pl.pallas_call(
        _norm_kernel,
        grid=(n // tm,),
        in_specs=[pl.BlockSpec((tm, D), lambda i: (i, 0)), pl.BlockSpec((1, D), lambda i: (0, 0)),
                  _gate_code_spec()],
        out_specs=[pl.BlockSpec((tm, D), lambda i: (i, 0)), pl.BlockSpec((tm, LANE), lambda i: (i, 0))],
        out_shape=[jax.ShapeDtypeStruct((n, D), BF16), jax.ShapeDtypeStruct((n, LANE), F32)],
        compiler_params=pltpu.CompilerParams(
            dimension_semantics=("arbitrary",), vmem_limit_bytes=VMEM_LIMIT),
        name="norm_prompt",
    )(x, gain, w_t)


def _mix_prompt(u, r, lg, w_t, wup, bg, gn, cos, sin, sg, sr, sample_rows):
    B, L, D = u.shape
    nt = L // T_BLK
    n_blocks = N_GROUPS * B * nt
    n_req = sg.shape[0]
    assert n_req <= n_blocks + 1
    sample_rows = [a.reshape(n_req, 1, a.shape[-1]) for a in sample_rows]
    assert 2 + len(sample_rows) == N_SAMPLE_IN

    def req_row(s):
        return (jnp.minimum(s, n_req - 1), 0, 0)

    def req_blk(s):
        return (jnp.minimum(s, n_req - 1), 0, 0, 0)

    def proj_idx(s):
        p = jnp.minimum(s, n_blocks - 1)
        return p // (B * nt), (p // nt) % B, p % nt

    def recur_idx(s):
        r = jnp.maximum(s - 1, 0)
        return r // (B * nt), (r // nt) % B, r % nt

    def out_map(s):
        g, b, t = recur_idx(s)
        return (b, t, g)

    def state_map(s):
        g, b, _ = recur_idx(s)
        return (b, g, 0, 0)

    z_scratch = [pltpu.VMEM(shape, dtype) for _, shape, dtype in Z_BUFFERS]
    return pl.pallas_call(
        functools.partial(_mix_prompt_kernel, blocks_per_group=B * nt, blocks_per_seq=nt),
        grid=(n_blocks + 1,),
        in_specs=[
            pl.BlockSpec(memory_space=pltpu.SMEM),
            pl.BlockSpec((1, T_BLK, D), lambda s: (proj_idx(s)[1], proj_idx(s)[2], 0)),
            pl.BlockSpec((1, T_BLK, LANE), lambda s: (proj_idx(s)[1], proj_idx(s)[2], 0)),
            *_weight_specs(lambda s: proj_idx(s)[0], pipeline_mode=pl.Buffered(1)),
            pl.BlockSpec((1, LANE, GLA_DK), lambda s: (proj_idx(s)[0], 0, 0)),
            pl.BlockSpec((1, 1, GLA_DK), lambda s: (proj_idx(s)[0], 0, 0)),
            pl.BlockSpec((1, GLA_DV), lambda s: (0, 0)),
            pl.BlockSpec((T_BLK, LANE), lambda s: (proj_idx(s)[2], 0)),
            pl.BlockSpec((T_BLK, LANE), lambda s: (proj_idx(s)[2], 0)),
            pl.BlockSpec((1, GLA_HEADS, GLA_DK, GLA_DV), req_blk),
            pl.BlockSpec((1, RET_HEADS, RET_DK, RET_DV), req_blk),
            *[pl.BlockSpec((1, 1, a.shape[-1]), req_row) for a in sample_rows],
        ],
        out_specs=[
            pl.BlockSpec((1, T_BLK, GW), out_map),
            pl.BlockSpec((1, 1, GLA_DK, GLA_DV), state_map),
            pl.BlockSpec((1, RET_PER_GROUP, RET_DK, RET_DV), state_map),
            pl.BlockSpec((1, GLA_HEADS, GLA_DK, GLA_DV), req_blk),
            pl.BlockSpec((1, RET_HEADS, RET_DK, RET_DV), req_blk),
            pl.BlockSpec((1, 1, GLA_V), req_row),
            pl.BlockSpec((1, 1, RET_V), req_row),
        ],
        out_shape=[
            jax.ShapeDtypeStruct((B, L, D_MODEL), BF16),
            jax.ShapeDtypeStruct((B, GLA_HEADS, GLA_DK, GLA_DV), F32),
            jax.ShapeDtypeStruct((B, RET_HEADS, RET_DK, RET_DV), F32),
            jax.ShapeDtypeStruct(sg.shape, F32),
            jax.ShapeDtypeStruct(sr.shape, F32),
            jax.ShapeDtypeStruct((n_req, 1, GLA_V), F32),
            jax.ShapeDtypeStruct((n_req, 1, RET_V), F32),
        ],
        scratch_shapes=[pltpu.VMEM((GLA_DV, GLA_DK), F32)] + z_scratch + z_scratch,
        compiler_params=pltpu.CompilerParams(
            dimension_semantics=("arbitrary",), vmem_limit_bytes=VMEM_LIMIT),
        name="mix_prompt",
    )(lg, u, r, *([w_t] * N_W), wup, bg, gn, cos, sin, sg, sr, *sample_rows)


def _out_kernel(x_ref, mg_ref, p_ref, wout_ref, nple_ref, wpg_ref, wpp_ref, nfin_ref, y_ref, *,
                final_norm):
    h = x_ref[...] + _dot(mg_ref[...], wout_ref[...])
    hn = (_rms(h) * nple_ref[...]).astype(BF16)
    gate = _sigmoid(_dot(hn, wpg_ref[...]))
    h = h + gate * _dot(p_ref[...].astype(BF16), wpp_ref[...])
    if final_norm:
        h = _rms(h) * nfin_ref[...]
    y_ref[...] = h


def _out_proj(x, merged, p, w_out, nple, w_pg, w_pp, nfin, final_norm):
    n, D = x.shape
    tm = min(TM_OUT, n)
    const = lambda i: (0, 0)
    return pl.pallas_call(
        functools.partial(_out_kernel, final_norm=final_norm),
        grid=(n // tm,),
        in_specs=[
            pl.BlockSpec((tm, D), lambda i: (i, 0)),
            pl.BlockSpec((tm, D), lambda i: (i, 0)),
            pl.BlockSpec((tm, PLE_DIM), lambda i: (i, 0)),
            pl.BlockSpec((D, D), const, pipeline_mode=pl.Buffered(1)),
            pl.BlockSpec((1, D), const),
            pl.BlockSpec((D, D), const, pipeline_mode=pl.Buffered(1)),
            pl.BlockSpec((PLE_DIM, D), const, pipeline_mode=pl.Buffered(1)),
            pl.BlockSpec((1, D), const),
        ],
        out_specs=pl.BlockSpec((tm, D), lambda i: (i, 0)),
        out_shape=jax.ShapeDtypeStruct((n, D), F32),
        compiler_params=pltpu.CompilerParams(
            dimension_semantics=("arbitrary",), vmem_limit_bytes=VMEM_LIMIT),
        name="out_proj",
    )(x, merged, p, w_out, nple, w_pg, w_pp, nfin)


Z_SAMPLE = (("dec", GLA_DK), ("qa", GLA_DK), ("ka", GLA_DK), ("va", GLA_DV), ("ga", GLA_DV), ("ma", GW),
            ("qb", GW), ("kb", GW), ("vb", GW), ("gb", GW), ("mb", GW))


def _proj_sample_kernel(x_ref, nmix_ref, *refs):
    w = dict(zip(W_NAMES, refs[:len(W_NAMES)]))
    wr_ref, wup_ref, bg_ref, cos_ref, sin_ref = refs[len(W_NAMES):len(W_NAMES) + 5]
    out = dict(zip([n for n, _ in Z_SAMPLE], refs[len(W_NAMES) + 5:]))
    u = (_rms(x_ref[...]) * nmix_ref[...]).astype(BF16)

    def proj(name):
        return _dot_nt(u, w[name][...])

    out["dec"][...] = jnp.exp(_gla_log_alpha(_gate_code(u, wr_ref), wup_ref, bg_ref))
    out["qa"][...] = proj("qa") * (GLA_DK ** -0.5)
    for name in ("ka", "va", "ga", "ma", "vb", "gb", "mb"):
        out[name][...] = proj(name)
    cos = cos_ref[...]
    sin = sin_ref[...]
    q_b = proj("qb")
    k_b = proj("kb")
    for j in range(RET_PER_GROUP):
        sl = slice(j * RET_DK, (j + 1) * RET_DK)
        out["qb"][:, sl] = _rotary(q_b[:, sl], cos, sin)
        out["kb"][:, sl] = _rotary(k_b[:, sl], cos, sin) * (RET_DK ** -0.5)


def _proj_sample(x, nmix, w_t, wup, bg, cos, sin):
    n, D = x.shape
    return pl.pallas_call(
        _proj_sample_kernel,
        grid=(N_GROUPS,),
        in_specs=[
            pl.BlockSpec((n, D), lambda g: (0, 0)),
            pl.BlockSpec((1, D), lambda g: (0, 0)),
            *_weight_specs(lambda g: g),
            _gate_code_spec(),
            pl.BlockSpec((1, LANE, GLA_DK), lambda g: (g, 0, 0)),
            pl.BlockSpec((1, 1, GLA_DK), lambda g: (g, 0, 0)),
            pl.BlockSpec((1, LANE), lambda g: (0, 0)),
            pl.BlockSpec((1, LANE), lambda g: (0, 0)),
        ],
        out_specs=[pl.BlockSpec((n, width), lambda g: (0, g)) for _, width in Z_SAMPLE],
        out_shape=[jax.ShapeDtypeStruct((n, N_GROUPS * width), F32) for _, width in Z_SAMPLE],
        compiler_params=pltpu.CompilerParams(
            dimension_semantics=("arbitrary",), vmem_limit_bytes=VMEM_LIMIT),
        name="proj_sample",
    )(x, nmix, *([w_t] * (N_W + 1)), wup, bg, cos, sin)


def _merge_sample_kernel(oa_ref, ob_ref, ga_ref, gb_ref, ma_ref, mb_ref, gn_ref, mg_ref):
    for h in range(GLA_HEADS):
        sl = slice(h * GLA_DV, (h + 1) * GLA_DV)
        g_a = ga_ref[:, sl]
        part_a = _sigmoid(ma_ref[:, sl]) * (_rms(oa_ref[:, sl]) * gn_ref[...] * (g_a * _sigmoid(g_a)))
        parts_b = []
        for j in range(RET_PER_GROUP):
            sb = slice(h * GW + j * RET_DV, h * GW + (j + 1) * RET_DV)
            g_b = gb_ref[:, sb]
            parts_b.append(_sigmoid(mb_ref[:, sb]) * (_rms(ob_ref[:, sb]) * (g_b * _sigmoid(g_b))))
        mg_ref[:, sl] = (part_a + jnp.concatenate(parts_b, axis=-1)).astype(mg_ref.dtype)


def _merge_sample(oa, ob, ga, gb, ma, mb, gn):
    n = oa.shape[0]
    return pl.pallas_call(
        _merge_sample_kernel,
        out_shape=jax.ShapeDtypeStruct((n, D_MODEL), BF16),
        name="merge_sample",
    )(oa, ob, ga, gb, ma, mb, gn)


def _rope_tables(pos):
    half = RET_DK // 2
    inv = 1.0 / (ROPE_BASE ** jnp.linspace(0.0, 1.0, half, dtype=jnp.float32))
    ang = pos[:, None] * inv[None, :]
    return jnp.cos(ang), jnp.sin(ang)


def kernel(x_prompt, x_sample, state_gla, state_ret, p_prompt, p_sample, norm_mix, w_in, w_gla_up, b_gla,
           gla_norm, w_out, norm_ple, w_ple_gate, w_ple_proj, norm_final):
    depth = w_in.shape[0]
    Bp, Lp, D = x_prompt.shape
    Bs, Ls, _ = x_sample.shape
    assert Ls == 1 and Lp % T_BLK == 0
    cos_p, sin_p = _rope_tables(jnp.arange(Lp, dtype=jnp.float32))
    cos_s, sin_s = _rope_tables(PAST_LEN + jnp.arange(Ls, dtype=jnp.float32))
    log_gamma = jnp.log(1.0 - jnp.exp2(-5.0 - jnp.arange(RET_HEADS, dtype=jnp.float32)))
    nfin = norm_final.reshape(1, D)

    hp = x_prompt
    hs = x_sample.reshape(Bs, D)
    gla_p, ret_p, gla_s, ret_s = [], [], [], []
    for i in range(depth):
        last = i == depth - 1
        nmix = norm_mix[i].reshape(1, D)
        nple = norm_ple[i].reshape(1, D)
        gn = gla_norm[i].reshape(1, GLA_DV)
        w_t = w_in[i].T.astype(BF16)
        wup = jnp.pad(w_gla_up[i], ((0, LANE - GLA_RANK), (0, 0))).astype(BF16)
        wup = wup.reshape(LANE, GLA_HEADS, GLA_DK).transpose(1, 0, 2)
        bg = b_gla[i].reshape(GLA_HEADS, 1, GLA_DK)
        w_o = w_out[i].astype(BF16)
        w_pg = w_ple_gate[i].astype(BF16)
        w_pp = w_ple_proj[i].astype(BF16)

        z = dict(zip([n for n, _ in Z_SAMPLE],
                     _proj_sample(hs, nmix, w_t, wup, bg, cos_s, sin_s)))
        u, r = _norm_prompt(hp.reshape(Bp * Lp, D), nmix, w_t)
        merged, sg, sr, nsg, nsr, oa, ob = _mix_prompt(
            u.reshape(Bp, Lp, D), r.reshape(Bp, Lp, LANE), log_gamma, w_t, wup, bg, gn, cos_p, sin_p,
            state_gla[i], state_ret[i], [z[n] for n in ("dec", "ka", "qa", "kb", "qb", "va", "vb")])
        hp = _out_proj(hp.reshape(Bp * Lp, D), merged.reshape(Bp * Lp, D), p_prompt[i].reshape(Bp * Lp, PLE_DIM),
                       w_o, nple, w_pg, w_pp, nfin, last).reshape(Bp, Lp, D)
        gla_p.append(sg)
        ret_p.append(sr)

        merged_s = _merge_sample(oa.reshape(Bs, D), ob.reshape(Bs, D), z["ga"], z["gb"], z["ma"], z["mb"], gn)
        hs = _out_proj(hs, merged_s, p_sample[i].reshape(Bs, PLE_DIM), w_o, nple, w_pg, w_pp, nfin, last)
        gla_s.append(nsg)
        ret_s.append(nsr)

    return (hp, hs.reshape(Bs, Ls, D), jnp.stack(gla_p), jnp.stack(ret_p), jnp.stack(gla_s), jnp.stack(ret_s))
```

```python
import functools
import itertools

import jax
import jax.numpy as jnp
import numpy as np
from jax import lax
from jax.experimental import pallas as pl
from jax.experimental.pallas import tpu as pltpu

F32 = jnp.float32
BF16 = jnp.bfloat16

D_MODEL = 2048
PAST_LEN = 16384
PLE_DIM = 256
GLA_HEADS = 4
GLA_DK = 256
GLA_DV = 512
GLA_RANK = 16
GLA_TAU = 16.0
RET_HEADS = 8
RET_DK = 256
RET_DV = 256
ROPE_BASE = 10000.0
EPS = 1e-6

GLA_QK = GLA_HEADS * GLA_DK
GLA_V = GLA_HEADS * GLA_DV
RET_QK = RET_HEADS * RET_DK
RET_V = RET_HEADS * RET_DV
IN_SPLITS = (GLA_QK, GLA_QK, GLA_V, GLA_V, GLA_RANK, RET_QK, RET_QK, RET_V, RET_V, D_MODEL, D_MODEL)
IN_OFFS = tuple(int(v) for v in np.concatenate([[0], np.cumsum(IN_SPLITS)[:-1]]))

N_GROUPS = GLA_HEADS
RET_PER_GROUP = RET_HEADS // N_GROUPS
GW = GLA_DV
LANE = 128
SUBLANE = 8

R_START = IN_OFFS[4]
HI_START = IN_OFFS[5]
N_PACK = HI_START - GLA_RANK + sum(IN_SPLITS[5:])
W_PIECES = {
    "qa": (IN_OFFS[0], GLA_DK),
    "ka": (IN_OFFS[1], GLA_DK),
    "va": (IN_OFFS[2], GLA_DV),
    "ga": (IN_OFFS[3], GLA_DV),
    "qb": (IN_OFFS[5] - GLA_RANK, GW),
    "kb": (IN_OFFS[6] - GLA_RANK, GW),
    "vb": (IN_OFFS[7] - GLA_RANK, GW),
    "gb": (IN_OFFS[8] - GLA_RANK, GW),
    "ma": (IN_OFFS[9] - GLA_RANK, GW),
    "mb": (IN_OFFS[10] - GLA_RANK, GW),
}
W_NAMES = tuple(W_PIECES)

T_BLK = 256
C_GLA = 64
N_CHUNKS = T_BLK // C_GLA
TM_OUT = 512
TM_NORM = 1024
PREP_ROWS = 512
VMEM_LIMIT = 56 * 1024 * 1024


def _rms(x):
    return x * lax.rsqrt(jnp.mean(x * x, axis=-1, keepdims=True) + EPS)


def _sigmoid(x):
    return 1.0 / (1.0 + jnp.exp(-x))


def _log_sigmoid(x):
    return jnp.minimum(x, 0.0) - jnp.log(1.0 + jnp.exp(-jnp.abs(x)))


def _dot(a, b):
    return jnp.dot(a, b, preferred_element_type=F32)


def _dot_nt(a, b):
    return lax.dot_general(a, b, (((1,), (1,)), ((), ())), preferred_element_type=F32)


def _dot_tn(a, b):
    return lax.dot_general(a, b, (((0,), (0,)), ((), ())), preferred_element_type=F32)


def _rotary(x, cos, sin):
    half = x.shape[-1] // 2
    x1, x2 = x[:, :half], x[:, half:]
    return jnp.concatenate([x1 * cos - x2 * sin, x1 * sin + x2 * cos], axis=-1)


def _gla_log_alpha(r, wup, bg):
    pre = _dot(r.astype(BF16), wup) + bg
    return _log_sigmoid(pre) * (1.0 / GLA_TAU)


def _weight_specs(group_of, **kw):
    specs = []
    for n in W_NAMES:
        start, width = W_PIECES[n]
        assert start % width == 0

        def index_map(*idx, first=start // width):
            return (first + group_of(*idx), 0)

        specs.append(pl.BlockSpec((width, D_MODEL), index_map, **kw))
    return specs


def _gate_code_spec():
    return pl.BlockSpec((pl.Element(GLA_RANK), pl.Element(D_MODEL)), lambda *idx: (R_START, 0))


def _gate_code(u, wr_ref):
    wr = jnp.concatenate([wr_ref[...].astype(BF16), jnp.zeros((LANE - GLA_RANK, D_MODEL), BF16)], axis=0)
    return _dot_nt(u, wr)


N_W = len(W_NAMES)


Z_BUFFERS = (
    ("qx", (T_BLK, GLA_DK), BF16),
    ("kx", (T_BLK, GLA_DK), BF16),
    ("qi", (T_BLK, GLA_DK), BF16),
    ("ko", (T_BLK, GLA_DK), BF16),
    ("va", (T_BLK, GLA_DV), BF16),
    ("dl", (SUBLANE, GLA_DK), F32),
    ("gta", (T_BLK, GW), F32),
    ("qb", (RET_PER_GROUP, T_BLK, RET_DK), BF16),
    ("kb", (RET_PER_GROUP, T_BLK, RET_DK), BF16),
    ("qbi", (RET_PER_GROUP, T_BLK, RET_DK), BF16),
    ("kbo", (RET_PER_GROUP, T_BLK, RET_DK), BF16),
    ("vb", (RET_PER_GROUP, T_BLK, RET_DV), BF16),
    ("gtb", (T_BLK, GW), F32),
)
Z_NAMES = tuple(n for n, _, _ in Z_BUFFERS)
assert N_CHUNKS <= SUBLANE


VT_DEC = 0
VT_KA = VT_DEC + GLA_HEADS
VT_QA = VT_KA + GLA_HEADS
VT_KB = VT_QA + GLA_HEADS
VT_QB = VT_KB + RET_HEADS
VT_N = VT_QB + RET_HEADS
VT_PAD = 32


def _state_update(lg_ref, sg_ref, sr_ref, dec_ref, ka_ref, qa_ref, kb_ref, qb_ref, va_ref, vb_ref,
                  nsg_ref, nsr_ref, oa_ref, ob_ref):
    rows = []
    for ref, heads in ((dec_ref, GLA_HEADS), (ka_ref, GLA_HEADS), (qa_ref, GLA_HEADS),
                       (kb_ref, RET_HEADS), (qb_ref, RET_HEADS)):
        rows += [ref[0, :, h * GLA_DK:(h + 1) * GLA_DK] for h in range(heads)]
    rows.append(jnp.zeros((VT_PAD - VT_N, GLA_DK), F32))
    vt = jnp.concatenate(rows, axis=0).T

    def col(i):
        return vt[:, i:i + 1]

    for h in range(GLA_HEADS):
        v_row = va_ref[0, :, h * GLA_DV:(h + 1) * GLA_DV]
        s_new = col(VT_DEC + h) * sg_ref[0, h] + col(VT_KA + h) * v_row
        nsg_ref[0, h] = s_new
        oa_ref[0, :, h * GLA_DV:(h + 1) * GLA_DV] = jnp.sum(col(VT_QA + h) * s_new, axis=0, keepdims=True)
        yield
    for h in range(RET_HEADS):
        v_row = vb_ref[0, :, h * RET_DV:(h + 1) * RET_DV]
        gamma = jnp.exp(jnp.full((1, RET_DV), lg_ref[h], F32))
        s_new = gamma * sr_ref[0, h] + col(VT_KB + h) * v_row
        nsr_ref[0, h] = s_new
        ob_ref[0, :, h * RET_DV:(h + 1) * RET_DV] = jnp.sum(col(VT_QB + h) * s_new, axis=0, keepdims=True)
        yield


N_SAMPLE_IN = 9
N_SAMPLE_OUT = 4


def _project_block(u_ref, r_ref, w, wup_ref, bg_ref, cos_ref, sin_ref, lg_ref, g, z):
    T = T_BLK
    u = u_ref[0]

    def proj(name, j=0, n=None):
        ref = w[name]
        n = ref.shape[0] if n is None else n
        return _dot_nt(u, ref[j * n:(j + 1) * n, :])

    la = _gla_log_alpha(r_ref[0], wup_ref[0], bg_ref[0])
    yield
    q = proj("qa") * (GLA_DK ** -0.5)
    yield
    k = proj("ka")
    ri = lax.broadcasted_iota(jnp.int32, (C_GLA, C_GLA), 0)
    ci = lax.broadcasted_iota(jnp.int32, (C_GLA, C_GLA), 1)
    tri = jnp.where(ri >= ci, 1.0, 0.0).astype(BF16)
    for c in range(N_CHUNKS):
        sl = slice(c * C_GLA, (c + 1) * C_GLA)
        la_c = la[sl]
        la_hi = la_c.astype(BF16)
        la_lo = (la_c - la_hi.astype(F32)).astype(BF16)
        b = _dot(tri, la_hi) + _dot(tri, la_lo)
        b_mid = b[C_GLA // 2 - 1:C_GLA // 2]
        b_last = b[C_GLA - 1:C_GLA]
        q_c, k_c = q[sl], k[sl]
        z["qx"][sl] = (q_c * jnp.exp(b - b_mid)).astype(BF16)
        z["kx"][sl] = (k_c * jnp.exp(b_mid - b)).astype(BF16)
        z["qi"][sl] = (q_c * jnp.exp(b)).astype(BF16)
        z["ko"][sl] = (k_c * jnp.exp(b_last - b)).astype(BF16)
        z["dl"][c:c + 1] = jnp.exp(b_last)
    yield
    z["va"][...] = proj("va").astype(BF16)
    yield
    g_a = proj("ga")
    silu_a = g_a * _sigmoid(g_a)
    yield
    z["gta"][...] = _sigmoid(proj("ma")) * silu_a
    yield

    cos = cos_ref[...]
    sin = sin_ref[...]
    tl = lax.broadcasted_iota(jnp.int32, (T, LANE), 0).astype(F32)
    for j in range(RET_PER_GROUP):
        lg = lg_ref[g * RET_PER_GROUP + j]
        q_b = _rotary(proj("qb", j, RET_DK), cos, sin)
        k_b = _rotary(proj("kb", j, RET_DK), cos, sin) * (RET_DK ** -0.5)
        dec_in = jnp.exp((tl + 1.0) * lg)
        dec_out = jnp.exp((T - 1.0 - tl) * lg)
        z["qb"][j] = q_b.astype(BF16)
        z["kb"][j] = k_b.astype(BF16)
        z["qbi"][j] = (q_b * jnp.concatenate([dec_in, dec_in], axis=-1)).astype(BF16)
        z["kbo"][j] = (k_b * jnp.concatenate([dec_out, dec_out], axis=-1)).astype(BF16)
        yield
        z["vb"][j] = proj("vb", j, RET_DV).astype(BF16)
        g_b = proj("gb", j, RET_DV)
        yield
        z["gtb"][:, j * RET_DV:(j + 1) * RET_DV] = _sigmoid(proj("mb", j, RET_DV)) * (g_b * _sigmoid(g_b))
        yield


def _recur_block(z, lg_ref, g, gn_ref, sgt_ref, sret_ref, merged_ref):
    T = T_BLK
    ri = lax.broadcasted_iota(jnp.int32, (C_GLA, C_GLA), 0)
    ci = lax.broadcasted_iota(jnp.int32, (C_GLA, C_GLA), 1)
    causal = ri >= ci
    st = sgt_ref[...]
    o_chunks = []
    for c in range(N_CHUNKS):
        sl = slice(c * C_GLA, (c + 1) * C_GLA)
        v_c = z["va"][sl]
        a = jnp.where(causal, _dot_nt(z["qx"][sl], z["kx"][sl]), 0.0).astype(BF16)
        o_chunks.append(_dot_nt(z["qi"][sl], st.astype(BF16)) + _dot(a, v_c))
        st = st * z["dl"][c:c + 1] + _dot_tn(v_c, z["ko"][sl])
        yield
    sgt_ref[...] = st
    part_a = z["gta"][...] * (_rms(jnp.concatenate(o_chunks, axis=0)) * gn_ref[...])
    yield

    rt = lax.broadcasted_iota(jnp.int32, (T, T), 0)
    ct = lax.broadcasted_iota(jnp.int32, (T, T), 1)
    parts_b = []
    for j in range(RET_PER_GROUP):
        lg = lg_ref[g * RET_PER_GROUP + j]
        v_b = z["vb"][j]
        dmat = jnp.where(rt >= ct, jnp.exp((rt - ct).astype(F32) * lg), 0.0)
        a = (_dot_nt(z["qb"][j], z["kb"][j]) * dmat).astype(BF16)
        s = sret_ref[0, j]
        o_b = _rms(_dot(z["qbi"][j], s.astype(BF16)) + _dot(a, v_b))
        dec_all = jnp.exp(jnp.full((1, RET_DV), T * lg, F32))
        sret_ref[0, j] = s * dec_all + _dot_tn(z["kbo"][j], v_b)
        parts_b.append(z["gtb"][:, j * RET_DV:(j + 1) * RET_DV] * o_b)
        yield
    merged_ref[0] = (part_a + jnp.concatenate(parts_b, axis=-1)).astype(merged_ref.dtype)


def _mix_prompt_kernel(lg_ref, u_ref, r_ref, *refs, blocks_per_group, blocks_per_seq):
    nw, nz = len(W_NAMES), len(Z_NAMES)
    w = dict(zip(W_NAMES, refs[:nw]))
    refs = list(refs[nw:])
    wup_ref, bg_ref, gn_ref, cos_ref, sin_ref = refs[:5]
    sample_in = refs[5:5 + N_SAMPLE_IN]
    refs = refs[5 + N_SAMPLE_IN:]
    merged_ref, sgla_ref, sret_ref = refs[:3]
    sample_out = refs[3:3 + N_SAMPLE_OUT]
    sgt_ref = refs[3 + N_SAMPLE_OUT]
    refs = refs[4 + N_SAMPLE_OUT:]
    z_even = dict(zip(Z_NAMES, refs[:nz]))
    z_odd = dict(zip(Z_NAMES, refs[nz:]))
    s = pl.program_id(0)
    n_blocks = pl.num_programs(0) - 1
    g_p = jnp.minimum(s, n_blocks - 1) // blocks_per_group
    r = jnp.maximum(s - 1, 0)
    g_r = r // blocks_per_group
    t_r = r % blocks_per_seq

    @pl.when(s == 0)
    def _():
        for ref in z_odd.values():
            ref[...] = jnp.zeros_like(ref)

    @pl.when(t_r == 0)
    def _():
        sgt_ref[...] = jnp.zeros_like(sgt_ref)
        sret_ref[...] = jnp.zeros_like(sret_ref)

    def step(z_write, z_read):
        rec = _recur_block(z_read, lg_ref, g_r, gn_ref, sgt_ref, sret_ref, merged_ref)
        prj = _project_block(u_ref, r_ref, w, wup_ref, bg_ref, cos_ref, sin_ref, lg_ref, g_p, z_write)
        upd = _state_update(lg_ref, *sample_in, *sample_out)
        for _ in itertools.zip_longest(prj, rec, upd):
            pass

    @pl.when(s % 2 == 0)
    def _():
        step(z_even, z_odd)

    @pl.when(s % 2 == 1)
    def _():
        step(z_odd, z_even)

    @pl.when((t_r == blocks_per_seq - 1) & (s > 0))
    def _():
        sgla_ref[0, 0] = sgt_ref[...].T


def _norm_kernel(x_ref, g_ref, wr_ref, u_ref, r_ref):
    u = (_rms(x_ref[...]) * g_ref[...]).astype(u_ref.dtype)
    u_ref[...] = u
    r_ref[...] = _gate_code(u, wr_ref)


def _norm_prompt(x, gain, w_f32):
    n, D = x.shape
    tm = min(TM_NORM, n)
    return pl.pallas_call(
        _norm_kernel,
        grid=(n // tm,),
        in_specs=[pl.BlockSpec((tm, D), lambda i: (i, 0)), pl.BlockSpec((1, D), lambda i: (0, 0)),
                  _gate_code_spec()],
        out_specs=[pl.BlockSpec((tm, D), lambda i: (i, 0)), pl.BlockSpec((tm, LANE), lambda i: (i, 0))],
        out_shape=[jax.ShapeDtypeStruct((n, D), BF16), jax.ShapeDtypeStruct((n, LANE), F32)],
        compiler_params=pltpu.CompilerParams(
            dimension_semantics=("arbitrary",), vmem_limit_bytes=VMEM_LIMIT),
        name="norm_prompt",
    )(x, gain, w_f32)


def _mix_prompt(u, r, lg, w_t, wup, bg, gn, cos, sin, sg, sr, sample_rows):
    B, L, D = u.shape
    nt = L // T_BLK
    n_blocks = N_GROUPS * B * nt
    n_req = sg.shape[0]
    assert n_req <= n_blocks + 1
    sample_rows = [a.reshape(n_req, 1, a.shape[-1]) for a in sample_rows]
    assert 2 + len(sample_rows) == N_SAMPLE_IN

    def req_row(s):
        return (jnp.minimum(s, n_req - 1), 0, 0)

    def req_blk(s):
        return (jnp.minimum(s, n_req - 1), 0, 0, 0)

    def proj_idx(s):
        p = jnp.minimum(s, n_blocks - 1)
        return p // (B * nt), (p // nt) % B, p % nt

    def recur_idx(s):
        r = jnp.maximum(s - 1, 0)
        return r // (B * nt), (r // nt) % B, r % nt

    def out_map(s):
        g, b, t = recur_idx(s)
        return (b, t, g)

    def state_map(s):
        g, b, _ = recur_idx(s)
        return (b, g, 0, 0)

    z_scratch = [pltpu.VMEM(shape, dtype) for _, shape, dtype in Z_BUFFERS]
    return pl.pallas_call(
        functools.partial(_mix_prompt_kernel, blocks_per_group=B * nt, blocks_per_seq=nt),
        grid=(n_blocks + 1,),
        in_specs=[
            pl.BlockSpec(memory_space=pltpu.SMEM),
            pl.BlockSpec((1, T_BLK, D), lambda s: (proj_idx(s)[1], proj_idx(s)[2], 0)),
            pl.BlockSpec((1, T_BLK, LANE), lambda s: (proj_idx(s)[1], proj_idx(s)[2], 0)),
            *_weight_specs(lambda s: proj_idx(s)[0], pipeline_mode=pl.Buffered(1)),
            pl.BlockSpec((1, LANE, GLA_DK), lambda s: (proj_idx(s)[0], 0, 0)),
            pl.BlockSpec((1, 1, GLA_DK), lambda s: (proj_idx(s)[0], 0, 0)),
            pl.BlockSpec((1, GLA_DV), lambda s: (0, 0)),
            pl.BlockSpec((T_BLK, LANE), lambda s: (proj_idx(s)[2], 0)),
            pl.BlockSpec((T_BLK, LANE), lambda s: (proj_idx(s)[2], 0)),
            pl.BlockSpec((1, GLA_HEADS, GLA_DK, GLA_DV), req_blk),
            pl.BlockSpec((1, RET_HEADS, RET_DK, RET_DV), req_blk),
            *[pl.BlockSpec((1, 1, a.shape[-1]), req_row) for a in sample_rows],
        ],
        out_specs=[
            pl.BlockSpec((1, T_BLK, GW), out_map),
            pl.BlockSpec((1, 1, GLA_DK, GLA_DV), state_map),
            pl.BlockSpec((1, RET_PER_GROUP, RET_DK, RET_DV), state_map),
            pl.BlockSpec((1, GLA_HEADS, GLA_DK, GLA_DV), req_blk),
            pl.BlockSpec((1, RET_HEADS, RET_DK, RET_DV), req_blk),
            pl.BlockSpec((1, 1, GLA_V), req_row),
            pl.BlockSpec((1, 1, RET_V), req_row),
        ],
        out_shape=[
            jax.ShapeDtypeStruct((B, L, D_MODEL), BF16),
            jax.ShapeDtypeStruct((B, GLA_HEADS, GLA_DK, GLA_DV), F32),
            jax.ShapeDtypeStruct((B, RET_HEADS, RET_DK, RET_DV), F32),
            jax.ShapeDtypeStruct(sg.shape, F32),
            jax.ShapeDtypeStruct(sr.shape, F32),
            jax.ShapeDtypeStruct((n_req, 1, GLA_V), F32),
            jax.ShapeDtypeStruct((n_req, 1, RET_V), F32),
        ],
        scratch_shapes=[pltpu.VMEM((GLA_DV, GLA_DK), F32)] + z_scratch + z_scratch,
        compiler_params=pltpu.CompilerParams(
            dimension_semantics=("arbitrary",), vmem_limit_bytes=VMEM_LIMIT),
        name="mix_prompt",
    )(lg, u, r, *([w_t] * N_W), wup, bg, gn, cos, sin, sg, sr, *sample_rows)


def _out_kernel(x_ref, mg_ref, p_ref, wout_ref, nple_ref, wpg_ref, wpp_ref, nfin_ref, y_ref, *,
                final_norm):
    h = x_ref[...] + _dot(mg_ref[...], wout_ref[...])
    hn = (_rms(h) * nple_ref[...]).astype(BF16)
    gate = _sigmoid(_dot(hn, wpg_ref[...]))
    h = h + gate * _dot(p_ref[...].astype(BF16), wpp_ref[...])
    if final_norm:
        h = _rms(h) * nfin_ref[...]
    y_ref[...] = h


def _out_proj(x, merged, p, w_out, nple, w_pg, w_pp, nfin, final_norm):
    n, D = x.shape
    tm = min(TM_OUT, n)
    const = lambda i: (0, 0)
    return pl.pallas_call(
        functools.partial(_out_kernel, final_norm=final_norm),
        grid=(n // tm,),
        in_specs=[
            pl.BlockSpec((tm, D), lambda i: (i, 0)),
            pl.BlockSpec((tm, D), lambda i: (i, 0)),
            pl.BlockSpec((tm, PLE_DIM), lambda i: (i, 0)),
            pl.BlockSpec((D, D), const, pipeline_mode=pl.Buffered(1)),
            pl.BlockSpec((1, D), const),
            pl.BlockSpec((D, D), const, pipeline_mode=pl.Buffered(1)),
            pl.BlockSpec((PLE_DIM, D), const, pipeline_mode=pl.Buffered(1)),
            pl.BlockSpec((1, D), const),
        ],
        out_specs=pl.BlockSpec((tm, D), lambda i: (i, 0)),
        out_shape=jax.ShapeDtypeStruct((n, D), F32),
        compiler_params=pltpu.CompilerParams(
            dimension_semantics=("arbitrary",), vmem_limit_bytes=VMEM_LIMIT),
        name="out_proj",
    )(x, merged, p, w_out, nple, w_pg, w_pp, nfin)


def _prep_kernel(x_ref, nmix_ref, w_ref, wr_ref, wb_ref, z_ref, r_ref):
    u = (_rms(x_ref[...]) * nmix_ref[...]).astype(BF16)
    wb = w_ref[...].astype(BF16)
    wb_ref[...] = wb
    z_ref[...] = _dot_nt(u, wb)

    @pl.when(pl.program_id(0) == 0)
    def _():
        r_ref[...] = _gate_code(u, wr_ref)


def _prep_weights(x, nmix, w_f32):
    n, D = x.shape
    lo_chunks = R_START // PREP_ROWS
    assert R_START % PREP_ROWS == 0 and N_PACK % PREP_ROWS == 0 and HI_START % SUBLANE == 0

    def rows(k):
        skip = jnp.where(k >= lo_chunks, GLA_RANK // SUBLANE, 0)
        return (SUBLANE * ((PREP_ROWS // SUBLANE) * k + skip), 0)

    return pl.pallas_call(
        _prep_kernel,
        grid=(N_PACK // PREP_ROWS,),
        in_specs=[
            pl.BlockSpec((n, D), lambda k: (0, 0)),
            pl.BlockSpec((1, D), lambda k: (0, 0)),
            pl.BlockSpec((pl.Element(PREP_ROWS), pl.Element(D)), rows),
            _gate_code_spec(),
        ],
        out_specs=[
            pl.BlockSpec((PREP_ROWS, D), lambda k: (k, 0)),
            pl.BlockSpec((n, PREP_ROWS), lambda k: (0, k)),
            pl.BlockSpec((n, LANE), lambda k: (0, 0)),
        ],
        out_shape=[
            jax.ShapeDtypeStruct((N_PACK, D), BF16),
            jax.ShapeDtypeStruct((n, N_PACK), F32),
            jax.ShapeDtypeStruct((n, LANE), F32),
        ],
        compiler_params=pltpu.CompilerParams(
            dimension_semantics=("arbitrary",), vmem_limit_bytes=VMEM_LIMIT),
        name="prep_weights",
    )(x, nmix, w_f32, w_f32)


Z_SAMPLE = (("dec", GLA_DK), ("qa", GLA_DK), ("ka", GLA_DK), ("va", GLA_DV), ("ga", GLA_DV), ("ma", GW),
            ("qb", GW), ("kb", GW), ("vb", GW), ("gb", GW), ("mb", GW))


def _sample_transform_kernel(z_ref, r_ref, wup_ref, bg_ref, cos_ref, sin_ref, *outs):
    out = dict(zip([n for n, _ in Z_SAMPLE], outs))

    def piece(name):
        start, width = W_PIECES[name]
        return z_ref[:, start:start + N_GROUPS * width]

    r = r_ref[...]
    for g in range(N_GROUPS):
        out["dec"][:, g * GLA_DK:(g + 1) * GLA_DK] = jnp.exp(_gla_log_alpha(r, wup_ref[g], bg_ref[g]))
    out["qa"][...] = piece("qa") * (GLA_DK ** -0.5)
    for name in ("ka", "va", "ga", "ma", "vb", "gb", "mb"):
        out[name][...] = piece(name)
    cos = cos_ref[...]
    sin = sin_ref[...]
    q_b = piece("qb")
    k_b = piece("kb")
    for h in range(RET_HEADS):
        sl = slice(h * RET_DK, (h + 1) * RET_DK)
        out["qb"][:, sl] = _rotary(q_b[:, sl], cos, sin)
        out["kb"][:, sl] = _rotary(k_b[:, sl], cos, sin) * (RET_DK ** -0.5)


def _sample_transform(z, r, wup, bg, cos, sin):
    n = z.shape[0]
    return pl.pallas_call(
        _sample_transform_kernel,
        out_shape=[jax.ShapeDtypeStruct((n, N_GROUPS * width), F32) for _, width in Z_SAMPLE],
        compiler_params=pltpu.CompilerParams(vmem_limit_bytes=VMEM_LIMIT),
        name="sample_transform",
    )(z, r, wup, bg, cos, sin)


def _merge_sample_kernel(oa_ref, ob_ref, ga_ref, gb_ref, ma_ref, mb_ref, gn_ref, mg_ref):
    for h in range(GLA_HEADS):
        sl = slice(h * GLA_DV, (h + 1) * GLA_DV)
        g_a = ga_ref[:, sl]
        part_a = _sigmoid(ma_ref[:, sl]) * (_rms(oa_ref[:, sl]) * gn_ref[...] * (g_a * _sigmoid(g_a)))
        parts_b = []
        for j in range(RET_PER_GROUP):
            sb = slice(h * GW + j * RET_DV, h * GW + (j + 1) * RET_DV)
            g_b = gb_ref[:, sb]
            parts_b.append(_sigmoid(mb_ref[:, sb]) * (_rms(ob_ref[:, sb]) * (g_b * _sigmoid(g_b))))
        mg_ref[:, sl] = (part_a + jnp.concatenate(parts_b, axis=-1)).astype(mg_ref.dtype)


def _merge_sample(oa, ob, ga, gb, ma, mb, gn):
    n = oa.shape[0]
    return pl.pallas_call(
        _merge_sample_kernel,
        out_shape=jax.ShapeDtypeStruct((n, D_MODEL), BF16),
        name="merge_sample",
    )(oa, ob, ga, gb, ma, mb, gn)


def _rope_tables(pos):
    half = RET_DK // 2
    inv = 1.0 / (ROPE_BASE ** jnp.linspace(0.0, 1.0, half, dtype=jnp.float32))
    ang = pos[:, None] * inv[None, :]
    return jnp.cos(ang), jnp.sin(ang)


def kernel(x_prompt, x_sample, state_gla, state_ret, p_prompt, p_sample, norm_mix, w_in, w_gla_up, b_gla,
           gla_norm, w_out, norm_ple, w_ple_gate, w_ple_proj, norm_final):
    depth = w_in.shape[0]
    Bp, Lp, D = x_prompt.shape
    Bs, Ls, _ = x_sample.shape
    assert Ls == 1 and Lp % T_BLK == 0
    cos_p, sin_p = _rope_tables(jnp.arange(Lp, dtype=jnp.float32))
    cos_s, sin_s = _rope_tables(PAST_LEN + jnp.arange(Ls, dtype=jnp.float32))
    log_gamma = jnp.log(1.0 - jnp.exp2(-5.0 - jnp.arange(RET_HEADS, dtype=jnp.float32)))
    nfin = norm_final.reshape(1, D)

    hp = x_prompt
    hs = x_sample.reshape(Bs, D)
    gla_p, ret_p, gla_s, ret_s = [], [], [], []
    for i in range(depth):
        last = i == depth - 1
        nmix = norm_mix[i].reshape(1, D)
        nple = norm_ple[i].reshape(1, D)
        gn = gla_norm[i].reshape(1, GLA_DV)
        w_f32 = w_in[i].T
        wup = jnp.pad(w_gla_up[i], ((0, LANE - GLA_RANK), (0, 0))).astype(BF16)
        wup = wup.reshape(LANE, GLA_HEADS, GLA_DK).transpose(1, 0, 2)
        bg = b_gla[i].reshape(GLA_HEADS, 1, GLA_DK)
        w_o = w_out[i].astype(BF16)
        w_pg = w_ple_gate[i].astype(BF16)
        w_pp = w_ple_proj[i].astype(BF16)

        w_t, z_raw, r_s = _prep_weights(hs, nmix, w_f32)
        z = dict(zip([n for n, _ in Z_SAMPLE], _sample_transform(z_raw, r_s, wup, bg, cos_s, sin_s)))
        u, r = _norm_prompt(hp.reshape(Bp * Lp, D), nmix, w_f32)
        merged, sg, sr, nsg, nsr, oa, ob = _mix_prompt(
            u.reshape(Bp, Lp, D), r.reshape(Bp, Lp, LANE), log_gamma, w_t, wup, bg, gn, cos_p, sin_p,
            state_gla[i], state_ret[i], [z[n] for n in ("dec", "ka", "qa", "kb", "qb", "va", "vb")])
        hp = _out_proj(hp.reshape(Bp * Lp, D), merged.reshape(Bp * Lp, D), p_prompt[i].reshape(Bp * Lp, PLE_DIM),
                       w_o, nple, w_pg, w_pp, nfin, last).reshape(Bp, Lp, D)
        gla_p.append(sg)
        ret_p.append(sr)

        merged_s = _merge_sample(oa.reshape(Bs, D), ob.reshape(Bs, D), z["ga"], z["gb"], z["ma"], z["mb"], gn)
        hs = _out_proj(hs, merged_s, p_sample[i].reshape(Bs, PLE_DIM), w_o, nple, w_pg, w_pp, nfin, last)
        gla_s.append(nsg)
        ret_s.append(nsr)

    return (hp, hs.reshape(Bs, Ls, D), jnp.stack(gla_p), jnp.stack(ret_p), jnp.stack(gla_s), jnp.stack(ret_s))
```

```python
import functools
import itertools

import jax
import jax.numpy as jnp
import numpy as np
from jax import lax
from jax.experimental import pallas as pl
from jax.experimental.pallas import tpu as pltpu

F32 = jnp.float32
BF16 = jnp.bfloat16

D_MODEL = 2048
PAST_LEN = 16384
PLE_DIM = 256
GLA_HEADS = 4
GLA_DK = 256
GLA_DV = 512
GLA_RANK = 16
GLA_TAU = 16.0
RET_HEADS = 8
RET_DK = 256
RET_DV = 256
ROPE_BASE = 10000.0
EPS = 1e-6

GLA_QK = GLA_HEADS * GLA_DK
GLA_V = GLA_HEADS * GLA_DV
RET_QK = RET_HEADS * RET_DK
RET_V = RET_HEADS * RET_DV
IN_SPLITS = (GLA_QK, GLA_QK, GLA_V, GLA_V, GLA_RANK, RET_QK, RET_QK, RET_V, RET_V, D_MODEL, D_MODEL)
IN_OFFS = tuple(int(v) for v in np.concatenate([[0], np.cumsum(IN_SPLITS)[:-1]]))

N_GROUPS = GLA_HEADS
RET_PER_GROUP = RET_HEADS // N_GROUPS
GW = GLA_DV
LANE = 128
SUBLANE = 8

R_START = IN_OFFS[4]
HI_START = IN_OFFS[5]
N_PACK = HI_START - GLA_RANK + sum(IN_SPLITS[5:])
W_PIECES = {
    "qa": (IN_OFFS[0], GLA_DK),
    "ka": (IN_OFFS[1], GLA_DK),
    "va": (IN_OFFS[2], GLA_DV),
    "ga": (IN_OFFS[3], GLA_DV),
    "qb": (IN_OFFS[5] - GLA_RANK, GW),
    "kb": (IN_OFFS[6] - GLA_RANK, GW),
    "vb": (IN_OFFS[7] - GLA_RANK, GW),
    "gb": (IN_OFFS[8] - GLA_RANK, GW),
    "ma": (IN_OFFS[9] - GLA_RANK, GW),
    "mb": (IN_OFFS[10] - GLA_RANK, GW),
}
W_NAMES = tuple(W_PIECES)

T_BLK = 256
C_GLA = 64
N_CHUNKS = T_BLK // C_GLA
TM_OUT = 512
TM_NORM = 1024
PREP_ROWS = 1024
VMEM_LIMIT = 56 * 1024 * 1024


def _rms(x):
    return x * lax.rsqrt(jnp.mean(x * x, axis=-1, keepdims=True) + EPS)


def _sigmoid(x):
    return 1.0 / (1.0 + jnp.exp(-x))


def _log_sigmoid(x):
    return jnp.minimum(x, 0.0) - jnp.log(1.0 + jnp.exp(-jnp.abs(x)))


def _dot(a, b):
    return jnp.dot(a, b, preferred_element_type=F32)


def _dot_nt(a, b):
    return lax.dot_general(a, b, (((1,), (1,)), ((), ())), preferred_element_type=F32)


def _dot_tn(a, b):
    return lax.dot_general(a, b, (((0,), (0,)), ((), ())), preferred_element_type=F32)


def _rotary(x, cos, sin):
    half = x.shape[-1] // 2
    x1, x2 = x[:, :half], x[:, half:]
    return jnp.concatenate([x1 * cos - x2 * sin, x1 * sin + x2 * cos], axis=-1)


def _gla_log_alpha(r, wup, bg):
    pre = _dot(r.astype(BF16), wup) + bg
    return _log_sigmoid(pre) * (1.0 / GLA_TAU)


def _weight_specs(group_of, **kw):
    specs = []
    for n in W_NAMES:
        start, width = W_PIECES[n]
        assert start % width == 0

        def index_map(*idx, first=start // width):
            return (0, first + group_of(*idx))

        specs.append(pl.BlockSpec((D_MODEL, width), index_map, **kw))
    return specs


def _gate_code_spec():
    return pl.BlockSpec((pl.Element(GLA_RANK), pl.Element(D_MODEL)), lambda *idx: (R_START, 0))


def _gate_code(u, wr_ref):
    wr = jnp.concatenate([wr_ref[...].astype(BF16), jnp.zeros((LANE - GLA_RANK, D_MODEL), BF16)], axis=0)
    return _dot_nt(u, wr)


N_W = len(W_NAMES)


Z_BUFFERS = (
    ("qx", (T_BLK, GLA_DK), BF16),
    ("kx", (T_BLK, GLA_DK), BF16),
    ("qi", (T_BLK, GLA_DK), BF16),
    ("ko", (T_BLK, GLA_DK), BF16),
    ("va", (T_BLK, GLA_DV), BF16),
    ("dl", (SUBLANE, GLA_DK), F32),
    ("gta", (T_BLK, GW), F32),
    ("qb", (RET_PER_GROUP, T_BLK, RET_DK), BF16),
    ("kb", (RET_PER_GROUP, T_BLK, RET_DK), BF16),
    ("qbi", (RET_PER_GROUP, T_BLK, RET_DK), BF16),
    ("kbo", (RET_PER_GROUP, T_BLK, RET_DK), BF16),
    ("vb", (RET_PER_GROUP, T_BLK, RET_DV), BF16),
    ("gtb", (T_BLK, GW), F32),
)
Z_NAMES = tuple(n for n, _, _ in Z_BUFFERS)
assert N_CHUNKS <= SUBLANE


VT_DEC = 0
VT_KA = VT_DEC + GLA_HEADS
VT_QA = VT_KA + GLA_HEADS
VT_KB = VT_QA + GLA_HEADS
VT_QB = VT_KB + RET_HEADS
VT_N = VT_QB + RET_HEADS
VT_PAD = 32


ROW_PIECES = (("dec", GLA_HEADS, GLA_DK), ("ka", GLA_HEADS, GLA_DK), ("qa", GLA_HEADS, GLA_DK),
              ("kb", RET_HEADS, RET_DK), ("qb", RET_HEADS, RET_DK), ("va", GLA_HEADS, GLA_DV),
              ("vb", RET_HEADS, RET_DV))
ROW_OFFS = dict(zip([n for n, _, _ in ROW_PIECES],
                    np.concatenate([[0], np.cumsum([h * w for _, h, w in ROW_PIECES])[:-1]]).tolist()))
ROW_W = sum(h * w for _, h, w in ROW_PIECES)
O_W = GLA_V + RET_V


def _state_update(lg_ref, sg_ref, sr_ref, row_ref, nsg_ref, nsr_ref, o_ref):
    def vec(name, h, width):
        off = ROW_OFFS[name] + h * width
        return row_ref[0, :, off:off + width]

    rows = [vec(name, h, GLA_DK) for name, heads, _ in ROW_PIECES[:5] for h in range(heads)]
    rows.append(jnp.zeros((VT_PAD - VT_N, GLA_DK), F32))
    vt = jnp.concatenate(rows, axis=0).T

    def col(i):
        return vt[:, i:i + 1]

    for h in range(GLA_HEADS):
        s_new = col(VT_DEC + h) * sg_ref[0, h] + col(VT_KA + h) * vec("va", h, GLA_DV)
        nsg_ref[0, h] = s_new
        o_ref[0, :, h * GLA_DV:(h + 1) * GLA_DV] = jnp.sum(col(VT_QA + h) * s_new, axis=0, keepdims=True)
        yield
    for h in range(RET_HEADS):
        gamma = jnp.exp(jnp.full((1, RET_DV), lg_ref[h], F32))
        s_new = gamma * sr_ref[0, h] + col(VT_KB + h) * vec("vb", h, RET_DV)
        nsr_ref[0, h] = s_new
        o_ref[0, :, GLA_V + h * RET_DV:GLA_V + (h + 1) * RET_DV] = jnp.sum(
            col(VT_QB + h) * s_new, axis=0, keepdims=True)
        yield


N_SAMPLE_IN = 3
N_SAMPLE_OUT = 3


def _project_block(u_ref, r_ref, w, wup_ref, bg_ref, cos_ref, sin_ref, lg_ref, g, z):
    T = T_BLK
    u = u_ref[0]

    def proj(name, j=0, n=None):
        ref = w[name]
        n = ref.shape[1] if n is None else n
        return _dot(u, ref[:, j * n:(j + 1) * n])

    la = _gla_log_alpha(r_ref[0], wup_ref[0], bg_ref[0])
    yield
    q = proj("qa") * (GLA_DK ** -0.5)
    yield
    k = proj("ka")
    ri = lax.broadcasted_iota(jnp.int32, (C_GLA, C_GLA), 0)
    ci = lax.broadcasted_iota(jnp.int32, (C_GLA, C_GLA), 1)
    tri = jnp.where(ri >= ci, 1.0, 0.0).astype(BF16)
    for c in range(N_CHUNKS):
        sl = slice(c * C_GLA, (c + 1) * C_GLA)
        la_c = la[sl]
        la_hi = la_c.astype(BF16)
        la_lo = (la_c - la_hi.astype(F32)).astype(BF16)
        b = _dot(tri, la_hi) + _dot(tri, la_lo)
        b_mid = b[C_GLA // 2 - 1:C_GLA // 2]
        b_last = b[C_GLA - 1:C_GLA]
        q_c, k_c = q[sl], k[sl]
        z["qx"][sl] = (q_c * jnp.exp(b - b_mid)).astype(BF16)
        z["kx"][sl] = (k_c * jnp.exp(b_mid - b)).astype(BF16)
        z["qi"][sl] = (q_c * jnp.exp(b)).astype(BF16)
        z["ko"][sl] = (k_c * jnp.exp(b_last - b)).astype(BF16)
        z["dl"][c:c + 1] = jnp.exp(b_last)
    yield
    z["va"][...] = proj("va").astype(BF16)
    yield
    g_a = proj("ga")
    silu_a = g_a * _sigmoid(g_a)
    yield
    z["gta"][...] = _sigmoid(proj("ma")) * silu_a
    yield

    cos = cos_ref[...]
    sin = sin_ref[...]
    tl = lax.broadcasted_iota(jnp.int32, (T, LANE), 0).astype(F32)
    for j in range(RET_PER_GROUP):
        lg = lg_ref[g * RET_PER_GROUP + j]
        q_b = _rotary(proj("qb", j, RET_DK), cos, sin)
        k_b = _rotary(proj("kb", j, RET_DK), cos, sin) * (RET_DK ** -0.5)
        dec_in = jnp.exp((tl + 1.0) * lg)
        dec_out = jnp.exp((T - 1.0 - tl) * lg)
        z["qb"][j] = q_b.astype(BF16)
        z["kb"][j] = k_b.astype(BF16)
        z["qbi"][j] = (q_b * jnp.concatenate([dec_in, dec_in], axis=-1)).astype(BF16)
        z["kbo"][j] = (k_b * jnp.concatenate([dec_out, dec_out], axis=-1)).astype(BF16)
        yield
        z["vb"][j] = proj("vb", j, RET_DV).astype(BF16)
        g_b = proj("gb", j, RET_DV)
        yield
        z["gtb"][:, j * RET_DV:(j + 1) * RET_DV] = _sigmoid(proj("mb", j, RET_DV)) * (g_b * _sigmoid(g_b))
        yield


def _recur_block(z, lg_ref, g, gn_ref, sgt_ref, sret_ref, merged_ref):
    T = T_BLK
    ri = lax.broadcasted_iota(jnp.int32, (C_GLA, C_GLA), 0)
    ci = lax.broadcasted_iota(jnp.int32, (C_GLA, C_GLA), 1)
    causal = ri >= ci
    st = sgt_ref[...]
    o_chunks = []
    for c in range(N_CHUNKS):
        sl = slice(c * C_GLA, (c + 1) * C_GLA)
        v_c = z["va"][sl]
        a = jnp.where(causal, _dot_nt(z["qx"][sl], z["kx"][sl]), 0.0).astype(BF16)
        o_chunks.append(_dot_nt(z["qi"][sl], st.astype(BF16)) + _dot(a, v_c))
        st = st * z["dl"][c:c + 1] + _dot_tn(v_c, z["ko"][sl])
        yield
    sgt_ref[...] = st
    part_a = z["gta"][...] * (_rms(jnp.concatenate(o_chunks, axis=0)) * gn_ref[...])
    yield

    rt = lax.broadcasted_iota(jnp.int32, (T, T), 0)
    ct = lax.broadcasted_iota(jnp.int32, (T, T), 1)
    parts_b = []
    for j in range(RET_PER_GROUP):
        lg = lg_ref[g * RET_PER_GROUP + j]
        v_b = z["vb"][j]
        dmat = jnp.where(rt >= ct, jnp.exp((rt - ct).astype(F32) * lg), 0.0)
        a = (_dot_nt(z["qb"][j], z["kb"][j]) * dmat).astype(BF16)
        s = sret_ref[0, j]
        o_b = _rms(_dot(z["qbi"][j], s.astype(BF16)) + _dot(a, v_b))
        dec_all = jnp.exp(jnp.full((1, RET_DV), T * lg, F32))
        sret_ref[0, j] = s * dec_all + _dot_tn(z["kbo"][j], v_b)
        parts_b.append(z["gtb"][:, j * RET_DV:(j + 1) * RET_DV] * o_b)
        yield
    merged_ref[0] = (part_a + jnp.concatenate(parts_b, axis=-1)).astype(merged_ref.dtype)


def _mix_prompt_kernel(lg_ref, u_ref, r_ref, *refs, blocks_per_group, blocks_per_seq):
    nw, nz = len(W_NAMES), len(Z_NAMES)
    w = dict(zip(W_NAMES, refs[:nw]))
    refs = list(refs[nw:])
    wup_ref, bg_ref, gn_ref, cos_ref, sin_ref = refs[:5]
    sample_in = refs[5:5 + N_SAMPLE_IN]
    refs = refs[5 + N_SAMPLE_IN:]
    merged_ref, sgla_ref, sret_ref = refs[:3]
    sample_out = refs[3:3 + N_SAMPLE_OUT]
    sgt_ref = refs[3 + N_SAMPLE_OUT]
    refs = refs[4 + N_SAMPLE_OUT:]
    z_even = dict(zip(Z_NAMES, refs[:nz]))
    z_odd = dict(zip(Z_NAMES, refs[nz:]))
    s = pl.program_id(0)
    n_blocks = pl.num_programs(0) - 1
    g_p = jnp.minimum(s, n_blocks - 1) // blocks_per_group
    r = jnp.maximum(s - 1, 0)
    g_r = r // blocks_per_group
    t_r = r % blocks_per_seq

    @pl.when(s == 0)
    def _():
        for ref in z_odd.values():
            ref[...] = jnp.zeros_like(ref)

    @pl.when(t_r == 0)
    def _():
        sgt_ref[...] = jnp.zeros_like(sgt_ref)
        sret_ref[...] = jnp.zeros_like(sret_ref)

    def step(z_write, z_read):
        rec = _recur_block(z_read, lg_ref, g_r, gn_ref, sgt_ref, sret_ref, merged_ref)
        prj = _project_block(u_ref, r_ref, w, wup_ref, bg_ref, cos_ref, sin_ref, lg_ref, g_p, z_write)
        upd = _state_update(lg_ref, *sample_in, *sample_out)
        for _ in itertools.zip_longest(prj, rec, upd):
            pass

    @pl.when(s % 2 == 0)
    def _():
        step(z_even, z_odd)

    @pl.when(s % 2 == 1)
    def _():
        step(z_odd, z_even)

    @pl.when((t_r == blocks_per_seq - 1) & (s > 0))
    def _():
        sgla_ref[0, 0] = sgt_ref[...].T


def _norm_kernel(x_ref, g_ref, wr_ref, u_ref, r_ref):
    u = (_rms(x_ref[...]) * g_ref[...]).astype(u_ref.dtype)
    u_ref[...] = u
    r_ref[...] = _gate_code(u, wr_ref)


def _norm_prompt(x, gain, w_f32):
    n, D = x.shape
    tm = min(TM_NORM, n)
    return pl.pallas_call(
        _norm_kernel,
        grid=(n // tm,),
        in_specs=[pl.BlockSpec((tm, D), lambda i: (i, 0)), pl.BlockSpec((1, D), lambda i: (0, 0)),
                  _gate_code_spec()],
        out_specs=[pl.BlockSpec((tm, D), lambda i: (i, 0)), pl.BlockSpec((tm, LANE), lambda i: (i, 0))],
        out_shape=[jax.ShapeDtypeStruct((n, D), BF16), jax.ShapeDtypeStruct((n, LANE), F32)],
        compiler_params=pltpu.CompilerParams(
            dimension_semantics=("arbitrary",), vmem_limit_bytes=VMEM_LIMIT),
        name="norm_prompt",
    )(x, gain, w_f32)


def _mix_prompt(u, r, lg, w_t, wup, bg, gn, cos, sin, sg, sr, rows):
    B, L, D = u.shape
    nt = L // T_BLK
    n_blocks = N_GROUPS * B * nt
    n_req = sg.shape[0]
    assert n_req <= n_blocks + 1
    assert rows.shape == (n_req, 1, ROW_W)

    def req_row(s):
        return (jnp.minimum(s, n_req - 1), 0, 0)

    def req_blk(s):
        return (jnp.minimum(s, n_req - 1), 0, 0, 0)

    def proj_idx(s):
        p = jnp.minimum(s, n_blocks - 1)
        return p // (B * nt), (p // nt) % B, p % nt

    def recur_idx(s):
        r = jnp.maximum(s - 1, 0)
        return r // (B * nt), (r // nt) % B, r % nt

    def out_map(s):
        g, b, t = recur_idx(s)
        return (b, t, g)

    def state_map(s):
        g, b, _ = recur_idx(s)
        return (b, g, 0, 0)

    z_scratch = [pltpu.VMEM(shape, dtype) for _, shape, dtype in Z_BUFFERS]
    return pl.pallas_call(
        functools.partial(_mix_prompt_kernel, blocks_per_group=B * nt, blocks_per_seq=nt),
        grid=(n_blocks + 1,),
        in_specs=[
            pl.BlockSpec(memory_space=pltpu.SMEM),
            pl.BlockSpec((1, T_BLK, D), lambda s: (proj_idx(s)[1], proj_idx(s)[2], 0)),
            pl.BlockSpec((1, T_BLK, LANE), lambda s: (proj_idx(s)[1], proj_idx(s)[2], 0)),
            *_weight_specs(lambda s: proj_idx(s)[0], pipeline_mode=pl.Buffered(1)),
            pl.BlockSpec((1, LANE, GLA_DK), lambda s: (proj_idx(s)[0], 0, 0)),
            pl.BlockSpec((1, 1, GLA_DK), lambda s: (proj_idx(s)[0], 0, 0)),
            pl.BlockSpec((1, GLA_DV), lambda s: (0, 0)),
            pl.BlockSpec((T_BLK, LANE), lambda s: (proj_idx(s)[2], 0)),
            pl.BlockSpec((T_BLK, LANE), lambda s: (proj_idx(s)[2], 0)),
            pl.BlockSpec((1, GLA_HEADS, GLA_DK, GLA_DV), req_blk),
            pl.BlockSpec((1, RET_HEADS, RET_DK, RET_DV), req_blk),
            pl.BlockSpec((1, 1, ROW_W), req_row),
        ],
        out_specs=[
            pl.BlockSpec((1, T_BLK, GW), out_map),
            pl.BlockSpec((1, 1, GLA_DK, GLA_DV), state_map),
            pl.BlockSpec((1, RET_PER_GROUP, RET_DK, RET_DV), state_map),
            pl.BlockSpec((1, GLA_HEADS, GLA_DK, GLA_DV), req_blk),
            pl.BlockSpec((1, RET_HEADS, RET_DK, RET_DV), req_blk),
            pl.BlockSpec((1, 1, O_W), req_row),
        ],
        out_shape=[
            jax.ShapeDtypeStruct((B, L, D_MODEL), BF16),
            jax.ShapeDtypeStruct((B, GLA_HEADS, GLA_DK, GLA_DV), F32),
            jax.ShapeDtypeStruct((B, RET_HEADS, RET_DK, RET_DV), F32),
            jax.ShapeDtypeStruct(sg.shape, F32),
            jax.ShapeDtypeStruct(sr.shape, F32),
            jax.ShapeDtypeStruct((n_req, 1, O_W), F32),
        ],
        scratch_shapes=[pltpu.VMEM((GLA_DV, GLA_DK), F32)] + z_scratch + z_scratch,
        compiler_params=pltpu.CompilerParams(
            dimension_semantics=("arbitrary",), vmem_limit_bytes=VMEM_LIMIT),
        name="mix_prompt",
    )(lg, u, r, *([w_t] * N_W), wup, bg, gn, cos, sin, sg, sr, rows)


def _out_kernel(x_ref, mg_ref, p_ref, wout_ref, nple_ref, wpg_ref, wpp_ref, nfin_ref, y_ref, *,
                final_norm):
    h = x_ref[...] + _dot(mg_ref[...], wout_ref[...])
    hn = (_rms(h) * nple_ref[...]).astype(BF16)
    gate = _sigmoid(_dot(hn, wpg_ref[...]))
    h = h + gate * _dot(p_ref[...].astype(BF16), wpp_ref[...])
    if final_norm:
        h = _rms(h) * nfin_ref[...]
    y_ref[...] = h


def _out_proj(x, merged, p, w_out, nple, w_pg, w_pp, nfin, final_norm):
    n, D = x.shape
    tm = min(TM_OUT, n)
    const = lambda i: (0, 0)
    return pl.pallas_call(
        functools.partial(_out_kernel, final_norm=final_norm),
        grid=(n // tm,),
        in_specs=[
            pl.BlockSpec((tm, D), lambda i: (i, 0)),
            pl.BlockSpec((tm, D), lambda i: (i, 0)),
            pl.BlockSpec((tm, PLE_DIM), lambda i: (i, 0)),
            pl.BlockSpec((D, D), const, pipeline_mode=pl.Buffered(1)),
            pl.BlockSpec((1, D), const),
            pl.BlockSpec((D, D), const, pipeline_mode=pl.Buffered(1)),
            pl.BlockSpec((PLE_DIM, D), const, pipeline_mode=pl.Buffered(1)),
            pl.BlockSpec((1, D), const),
        ],
        out_specs=pl.BlockSpec((tm, D), lambda i: (i, 0)),
        out_shape=jax.ShapeDtypeStruct((n, D), F32),
        compiler_params=pltpu.CompilerParams(
            dimension_semantics=("arbitrary",), vmem_limit_bytes=VMEM_LIMIT),
        name="out_proj",
    )(x, merged, p, w_out, nple, w_pg, w_pp, nfin)


def _prep_kernel(x_ref, nmix_ref, w_ref, wr_ref, wb_ref, z_ref, r_ref):
    u = (_rms(x_ref[...]) * nmix_ref[...]).astype(BF16)
    wb = w_ref[...].astype(BF16)
    wb_ref[...] = wb.T
    z_ref[...] = _dot_nt(u, wb)

    @pl.when(pl.program_id(0) == 0)
    def _():
        r_ref[...] = _gate_code(u, wr_ref)


def _prep_weights(x, nmix, w_f32):
    n, D = x.shape
    lo_chunks = R_START // PREP_ROWS
    assert R_START % PREP_ROWS == 0 and N_PACK % PREP_ROWS == 0 and HI_START % SUBLANE == 0

    def rows(k):
        skip = jnp.where(k >= lo_chunks, GLA_RANK // SUBLANE, 0)
        return (SUBLANE * ((PREP_ROWS // SUBLANE) * k + skip), 0)

    return pl.pallas_call(
        _prep_kernel,
        grid=(N_PACK // PREP_ROWS,),
        in_specs=[
            pl.BlockSpec((n, D), lambda k: (0, 0)),
            pl.BlockSpec((1, D), lambda k: (0, 0)),
            pl.BlockSpec((pl.Element(PREP_ROWS), pl.Element(D)), rows),
            _gate_code_spec(),
        ],
        out_specs=[
            pl.BlockSpec((D, PREP_ROWS), lambda k: (0, k)),
            pl.BlockSpec((n, PREP_ROWS), lambda k: (0, k)),
            pl.BlockSpec((n, LANE), lambda k: (0, 0)),
        ],
        out_shape=[
            jax.ShapeDtypeStruct((D, N_PACK), BF16),
            jax.ShapeDtypeStruct((n, N_PACK), F32),
            jax.ShapeDtypeStruct((n, LANE), F32),
        ],
        compiler_params=pltpu.CompilerParams(
            dimension_semantics=("arbitrary",), vmem_limit_bytes=VMEM_LIMIT),
        name="prep_weights",
    )(x, nmix, w_f32, w_f32)


def _sample_transform_kernel(z_ref, r_ref, wup_ref, bg_ref, cos_ref, sin_ref, row_ref):
    def piece(name):
        start, width = W_PIECES[name]
        return z_ref[:, start:start + N_GROUPS * width]

    def put(name, value, h=0):
        off = ROW_OFFS[name] + h * value.shape[-1]
        row_ref[:, off:off + value.shape[-1]] = value

    r = r_ref[...]
    for g in range(N_GROUPS):
        put("dec", jnp.exp(_gla_log_alpha(r, wup_ref[g], bg_ref[g])), g)
    put("qa", piece("qa") * (GLA_DK ** -0.5))
    for name in ("ka", "va", "vb"):
        put(name, piece(name))
    cos = cos_ref[...]
    sin = sin_ref[...]
    q_b = piece("qb")
    k_b = piece("kb")
    for h in range(RET_HEADS):
        sl = slice(h * RET_DK, (h + 1) * RET_DK)
        put("qb", _rotary(q_b[:, sl], cos, sin), h)
        put("kb", _rotary(k_b[:, sl], cos, sin) * (RET_DK ** -0.5), h)


def _sample_transform(z, r, wup, bg, cos, sin):
    n = z.shape[0]
    return pl.pallas_call(
        _sample_transform_kernel,
        out_shape=jax.ShapeDtypeStruct((n, ROW_W), F32),
        compiler_params=pltpu.CompilerParams(vmem_limit_bytes=VMEM_LIMIT),
        name="sample_transform",
    )(z, r, wup, bg, cos, sin)


def _merge_sample_kernel(o_ref, z_ref, gn_ref, mg_ref):
    def gate(name, lo, n):
        start = W_PIECES[name][0] + lo
        return z_ref[:, start:start + n]

    for h in range(GLA_HEADS):
        sl = slice(h * GLA_DV, (h + 1) * GLA_DV)
        g_a = gate("ga", h * GLA_DV, GLA_DV)
        part_a = _sigmoid(gate("ma", h * GW, GW)) * (_rms(o_ref[:, sl]) * gn_ref[...] * (g_a * _sigmoid(g_a)))
        parts_b = []
        for j in range(RET_PER_GROUP):
            lo = h * GW + j * RET_DV
            g_b = gate("gb", lo, RET_DV)
            o_b = o_ref[:, GLA_V + lo:GLA_V + lo + RET_DV]
            parts_b.append(_sigmoid(gate("mb", lo, RET_DV)) * (_rms(o_b) * (g_b * _sigmoid(g_b))))
        mg_ref[:, sl] = (part_a + jnp.concatenate(parts_b, axis=-1)).astype(mg_ref.dtype)


def _merge_sample(o, z_raw, gn):
    n = o.shape[0]
    return pl.pallas_call(
        _merge_sample_kernel,
        out_shape=jax.ShapeDtypeStruct((n, D_MODEL), BF16),
        compiler_params=pltpu.CompilerParams(vmem_limit_bytes=VMEM_LIMIT),
        name="merge_sample",
    )(o, z_raw, gn)


def _rope_tables(pos):
    half = RET_DK // 2
    inv = 1.0 / (ROPE_BASE ** jnp.linspace(0.0, 1.0, half, dtype=jnp.float32))
    ang = pos[:, None] * inv[None, :]
    return jnp.cos(ang), jnp.sin(ang)


def kernel(x_prompt, x_sample, state_gla, state_ret, p_prompt, p_sample, norm_mix, w_in, w_gla_up, b_gla,
           gla_norm, w_out, norm_ple, w_ple_gate, w_ple_proj, norm_final):
    depth = w_in.shape[0]
    Bp, Lp, D = x_prompt.shape
    Bs, Ls, _ = x_sample.shape
    assert Ls == 1 and Lp % T_BLK == 0
    cos_p, sin_p = _rope_tables(jnp.arange(Lp, dtype=jnp.float32))
    cos_s, sin_s = _rope_tables(PAST_LEN + jnp.arange(Ls, dtype=jnp.float32))
    log_gamma = jnp.log(1.0 - jnp.exp2(-5.0 - jnp.arange(RET_HEADS, dtype=jnp.float32)))
    nfin = norm_final.reshape(1, D)

    hp = x_prompt
    hs = x_sample.reshape(Bs, D)
    gla_p, ret_p, gla_s, ret_s = [], [], [], []
    for i in range(depth):
        last = i == depth - 1
        nmix = norm_mix[i].reshape(1, D)
        nple = norm_ple[i].reshape(1, D)
        gn = gla_norm[i].reshape(1, GLA_DV)
        w_f32 = w_in[i].T
        wup = jnp.pad(w_gla_up[i], ((0, LANE - GLA_RANK), (0, 0))).astype(BF16)
        wup = wup.reshape(LANE, GLA_HEADS, GLA_DK).transpose(1, 0, 2)
        bg = b_gla[i].reshape(GLA_HEADS, 1, GLA_DK)
        w_o = w_out[i].astype(BF16)
        w_pg = w_ple_gate[i].astype(BF16)
        w_pp = w_ple_proj[i].astype(BF16)

        w_t, z_raw, r_s = _prep_weights(hs, nmix, w_f32)
        rows = _sample_transform(z_raw, r_s, wup, bg, cos_s, sin_s)
        u, r = _norm_prompt(hp.reshape(Bp * Lp, D), nmix, w_f32)
        merged, sg, sr, nsg, nsr, o_s = _mix_prompt(
            u.reshape(Bp, Lp, D), r.reshape(Bp, Lp, LANE), log_gamma, w_t, wup, bg, gn, cos_p, sin_p,
            state_gla[i], state_ret[i], rows.reshape(Bs, 1, ROW_W))
        hp = _out_proj(hp.reshape(Bp * Lp, D), merged.reshape(Bp * Lp, D), p_prompt[i].reshape(Bp * Lp, PLE_DIM),
                       w_o, nple, w_pg, w_pp, nfin, last).reshape(Bp, Lp, D)
        gla_p.append(sg)
        ret_p.append(sr)

        merged_s = _merge_sample(o_s.reshape(Bs, O_W), z_raw, gn)
        hs = _out_proj(hs, merged_s, p_sample[i].reshape(Bs, PLE_DIM), w_o, nple, w_pg, w_pp, nfin, last)
        gla_s.append(nsg)
        ret_s.append(nsr)

    return (hp, hs.reshape(Bs, Ls, D), jnp.stack(gla_p), jnp.stack(ret_p), jnp.stack(gla_s), jnp.stack(ret_s))
```

```python
import functools
import itertools

import jax
import jax.numpy as jnp
import numpy as np
from jax import lax
from jax.experimental import pallas as pl
from jax.experimental.pallas import tpu as pltpu

F32 = jnp.float32
BF16 = jnp.bfloat16

D_MODEL = 2048
PAST_LEN = 16384
PLE_DIM = 256
GLA_HEADS = 4
GLA_DK = 256
GLA_DV = 512
GLA_RANK = 16
GLA_TAU = 16.0
RET_HEADS = 8
RET_DK = 256
RET_DV = 256
ROPE_BASE = 10000.0
EPS = 1e-6

GLA_QK = GLA_HEADS * GLA_DK
GLA_V = GLA_HEADS * GLA_DV
RET_QK = RET_HEADS * RET_DK
RET_V = RET_HEADS * RET_DV
IN_SPLITS = (GLA_QK, GLA_QK, GLA_V, GLA_V, GLA_RANK, RET_QK, RET_QK, RET_V, RET_V, D_MODEL, D_MODEL)
IN_OFFS = tuple(int(v) for v in np.concatenate([[0], np.cumsum(IN_SPLITS)[:-1]]))

N_GROUPS = GLA_HEADS
RET_PER_GROUP = RET_HEADS // N_GROUPS
GW = GLA_DV
LANE = 128
SUBLANE = 8

R_START = IN_OFFS[4]
HI_START = IN_OFFS[5]
N_PACK = HI_START - GLA_RANK + sum(IN_SPLITS[5:])
W_PIECES = {
    "qa": (IN_OFFS[0], GLA_DK),
    "ka": (IN_OFFS[1], GLA_DK),
    "va": (IN_OFFS[2], GLA_DV),
    "ga": (IN_OFFS[3], GLA_DV),
    "qb": (IN_OFFS[5] - GLA_RANK, GW),
    "kb": (IN_OFFS[6] - GLA_RANK, GW),
    "vb": (IN_OFFS[7] - GLA_RANK, GW),
    "gb": (IN_OFFS[8] - GLA_RANK, GW),
    "ma": (IN_OFFS[9] - GLA_RANK, GW),
    "mb": (IN_OFFS[10] - GLA_RANK, GW),
}
W_NAMES = tuple(W_PIECES)

T_BLK = 256
C_GLA = 64
N_CHUNKS = T_BLK // C_GLA
TM_OUT = 512
TM_NORM = 1024
PREP_ROWS = 1024
VMEM_LIMIT = 56 * 1024 * 1024


def _rms(x):
    return x * lax.rsqrt(jnp.mean(x * x, axis=-1, keepdims=True) + EPS)


def _sigmoid(x):
    return 1.0 / (1.0 + jnp.exp(-x))


def _log_sigmoid(x):
    return jnp.minimum(x, 0.0) - jnp.log(1.0 + jnp.exp(-jnp.abs(x)))


def _tokens(ref):
    return ref[...] if len(ref.shape) == 2 else ref[:, 0, :]


def _dot(a, b):
    return jnp.dot(a, b, preferred_element_type=F32)


def _dot_nt(a, b):
    return lax.dot_general(a, b, (((1,), (1,)), ((), ())), preferred_element_type=F32)


def _dot_tn(a, b):
    return lax.dot_general(a, b, (((0,), (0,)), ((), ())), preferred_element_type=F32)


def _rotary(x, cos, sin):
    half = x.shape[-1] // 2
    x1, x2 = x[:, :half], x[:, half:]
    return jnp.concatenate([x1 * cos - x2 * sin, x1 * sin + x2 * cos], axis=-1)


def _gla_log_alpha(r, wup, bg):
    pre = _dot(r.astype(BF16), wup) + bg
    return _log_sigmoid(pre) * (1.0 / GLA_TAU)


def _weight_specs(group_of, **kw):
    specs = []
    for n in W_NAMES:
        start, width = W_PIECES[n]
        assert start % width == 0

        def index_map(*idx, first=start // width):
            return (0, first + group_of(*idx))

        specs.append(pl.BlockSpec((D_MODEL, width), index_map, **kw))
    return specs


def _gate_code_spec():
    return pl.BlockSpec((pl.Element(GLA_RANK), pl.Element(D_MODEL)), lambda *idx: (R_START, 0))


def _gate_code(u, wr_ref):
    wr = jnp.concatenate([wr_ref[...].astype(BF16), jnp.zeros((LANE - GLA_RANK, D_MODEL), BF16)], axis=0)
    return _dot_nt(u, wr)


N_W = len(W_NAMES)


Z_BUFFERS = (
    ("qx", (T_BLK, GLA_DK), BF16),
    ("kx", (T_BLK, GLA_DK), BF16),
    ("qi", (T_BLK, GLA_DK), BF16),
    ("ko", (T_BLK, GLA_DK), BF16),
    ("va", (T_BLK, GLA_DV), BF16),
    ("dl", (SUBLANE, GLA_DK), F32),
    ("gta", (T_BLK, GW), F32),
    ("qb", (RET_PER_GROUP, T_BLK, RET_DK), BF16),
    ("kb", (RET_PER_GROUP, T_BLK, RET_DK), BF16),
    ("qbi", (RET_PER_GROUP, T_BLK, RET_DK), BF16),
    ("kbo", (RET_PER_GROUP, T_BLK, RET_DK), BF16),
    ("vb", (RET_PER_GROUP, T_BLK, RET_DV), BF16),
    ("gtb", (T_BLK, GW), F32),
)
Z_NAMES = tuple(n for n, _, _ in Z_BUFFERS)
assert N_CHUNKS <= SUBLANE


VT_DEC = 0
VT_KA = VT_DEC + GLA_HEADS
VT_QA = VT_KA + GLA_HEADS
VT_KB = VT_QA + GLA_HEADS
VT_QB = VT_KB + RET_HEADS
VT_N = VT_QB + RET_HEADS
VT_PAD = 32


ROW_PIECES = (("dec", GLA_HEADS, GLA_DK), ("ka", GLA_HEADS, GLA_DK), ("qa", GLA_HEADS, GLA_DK),
              ("kb", RET_HEADS, RET_DK), ("qb", RET_HEADS, RET_DK), ("va", GLA_HEADS, GLA_DV),
              ("vb", RET_HEADS, RET_DV))
ROW_OFFS = dict(zip([n for n, _, _ in ROW_PIECES],
                    np.concatenate([[0], np.cumsum([h * w for _, h, w in ROW_PIECES])[:-1]]).tolist()))
ROW_W = sum(h * w for _, h, w in ROW_PIECES)
O_W = GLA_V + RET_V


def _state_update(lg_ref, sg_ref, sr_ref, row_ref, nsg_ref, nsr_ref, o_ref):
    def vec(name, h, width):
        off = ROW_OFFS[name] + h * width
        return row_ref[0, :, off:off + width]

    rows = [vec(name, h, GLA_DK) for name, heads, _ in ROW_PIECES[:5] for h in range(heads)]
    rows.append(jnp.zeros((VT_PAD - VT_N, GLA_DK), F32))
    vt = jnp.concatenate(rows, axis=0).T

    def col(i):
        return vt[:, i:i + 1]

    for h in range(GLA_HEADS):
        s_new = col(VT_DEC + h) * sg_ref[0, h] + col(VT_KA + h) * vec("va", h, GLA_DV)
        nsg_ref[0, h] = s_new
        o_ref[0, :, h * GLA_DV:(h + 1) * GLA_DV] = jnp.sum(col(VT_QA + h) * s_new, axis=0, keepdims=True)
        yield
    for h in range(RET_HEADS):
        gamma = jnp.exp(jnp.full((1, RET_DV), lg_ref[h], F32))
        s_new = gamma * sr_ref[0, h] + col(VT_KB + h) * vec("vb", h, RET_DV)
        nsr_ref[0, h] = s_new
        o_ref[0, :, GLA_V + h * RET_DV:GLA_V + (h + 1) * RET_DV] = jnp.sum(
            col(VT_QB + h) * s_new, axis=0, keepdims=True)
        yield


N_SAMPLE_IN = 3
N_SAMPLE_OUT = 3


def _project_block(u_ref, r_ref, w, wup_ref, bg_ref, cos_ref, sin_ref, lg_ref, g, z):
    T = T_BLK
    u = u_ref[0]

    def proj(name, j=0, n=None):
        ref = w[name]
        n = ref.shape[1] if n is None else n
        return _dot(u, ref[:, j * n:(j + 1) * n])

    la = _gla_log_alpha(r_ref[0], wup_ref[0], bg_ref[0])
    yield
    q = proj("qa") * (GLA_DK ** -0.5)
    yield
    k = proj("ka")
    ri = lax.broadcasted_iota(jnp.int32, (C_GLA, C_GLA), 0)
    ci = lax.broadcasted_iota(jnp.int32, (C_GLA, C_GLA), 1)
    tri = jnp.where(ri >= ci, 1.0, 0.0).astype(BF16)
    for c in range(N_CHUNKS):
        sl = slice(c * C_GLA, (c + 1) * C_GLA)
        la_c = la[sl]
        la_hi = la_c.astype(BF16)
        la_lo = (la_c - la_hi.astype(F32)).astype(BF16)
        b = _dot(tri, la_hi) + _dot(tri, la_lo)
        b_mid = b[C_GLA // 2 - 1:C_GLA // 2]
        b_last = b[C_GLA - 1:C_GLA]
        q_c, k_c = q[sl], k[sl]
        z["qx"][sl] = (q_c * jnp.exp(b - b_mid)).astype(BF16)
        z["kx"][sl] = (k_c * jnp.exp(b_mid - b)).astype(BF16)
        z["qi"][sl] = (q_c * jnp.exp(b)).astype(BF16)
        z["ko"][sl] = (k_c * jnp.exp(b_last - b)).astype(BF16)
        z["dl"][c:c + 1] = jnp.exp(b_last)
    yield
    z["va"][...] = proj("va").astype(BF16)
    yield
    g_a = proj("ga")
    silu_a = g_a * _sigmoid(g_a)
    yield
    z["gta"][...] = _sigmoid(proj("ma")) * silu_a
    yield

    cos = cos_ref[...]
    sin = sin_ref[...]
    tl = lax.broadcasted_iota(jnp.int32, (T, LANE), 0).astype(F32)
    for j in range(RET_PER_GROUP):
        lg = lg_ref[g * RET_PER_GROUP + j]
        q_b = _rotary(proj("qb", j, RET_DK), cos, sin)
        k_b = _rotary(proj("kb", j, RET_DK), cos, sin) * (RET_DK ** -0.5)
        dec_in = jnp.exp((tl + 1.0) * lg)
        dec_out = jnp.exp((T - 1.0 - tl) * lg)
        z["qb"][j] = q_b.astype(BF16)
        z["kb"][j] = k_b.astype(BF16)
        z["qbi"][j] = (q_b * jnp.concatenate([dec_in, dec_in], axis=-1)).astype(BF16)
        z["kbo"][j] = (k_b * jnp.concatenate([dec_out, dec_out], axis=-1)).astype(BF16)
        yield
        z["vb"][j] = proj("vb", j, RET_DV).astype(BF16)
        g_b = proj("gb", j, RET_DV)
        yield
        z["gtb"][:, j * RET_DV:(j + 1) * RET_DV] = _sigmoid(proj("mb", j, RET_DV)) * (g_b * _sigmoid(g_b))
        yield


def _recur_block(z, lg_ref, g, gn_ref, sgt_ref, sret_ref, merged_ref):
    T = T_BLK
    ri = lax.broadcasted_iota(jnp.int32, (C_GLA, C_GLA), 0)
    ci = lax.broadcasted_iota(jnp.int32, (C_GLA, C_GLA), 1)
    causal = ri >= ci
    st = sgt_ref[...]
    o_chunks = []
    for c in range(N_CHUNKS):
        sl = slice(c * C_GLA, (c + 1) * C_GLA)
        v_c = z["va"][sl]
        a = jnp.where(causal, _dot_nt(z["qx"][sl], z["kx"][sl]), 0.0).astype(BF16)
        o_chunks.append(_dot_nt(z["qi"][sl], st.astype(BF16)) + _dot(a, v_c))
        st = st * z["dl"][c:c + 1] + _dot_tn(v_c, z["ko"][sl])
        yield
    sgt_ref[...] = st
    part_a = z["gta"][...] * (_rms(jnp.concatenate(o_chunks, axis=0)) * gn_ref[...])
    yield

    rt = lax.broadcasted_iota(jnp.int32, (T, T), 0)
    ct = lax.broadcasted_iota(jnp.int32, (T, T), 1)
    parts_b = []
    for j in range(RET_PER_GROUP):
        lg = lg_ref[g * RET_PER_GROUP + j]
        v_b = z["vb"][j]
        dmat = jnp.where(rt >= ct, jnp.exp((rt - ct).astype(F32) * lg), 0.0)
        a = (_dot_nt(z["qb"][j], z["kb"][j]) * dmat).astype(BF16)
        s = sret_ref[0, j]
        o_b = _rms(_dot(z["qbi"][j], s.astype(BF16)) + _dot(a, v_b))
        dec_all = jnp.exp(jnp.full((1, RET_DV), T * lg, F32))
        sret_ref[0, j] = s * dec_all + _dot_tn(z["kbo"][j], v_b)
        parts_b.append(z["gtb"][:, j * RET_DV:(j + 1) * RET_DV] * o_b)
        yield
    merged_ref[0] = (part_a + jnp.concatenate(parts_b, axis=-1)).astype(merged_ref.dtype)


def _mix_prompt_kernel(lg_ref, u_ref, r_ref, *refs, blocks_per_group, blocks_per_seq):
    nw, nz = len(W_NAMES), len(Z_NAMES)
    w = dict(zip(W_NAMES, refs[:nw]))
    refs = list(refs[nw:])
    wup_ref, bg_ref, gn_ref, cos_ref, sin_ref = refs[:5]
    sample_in = refs[5:5 + N_SAMPLE_IN]
    refs = refs[5 + N_SAMPLE_IN:]
    merged_ref, sgla_ref, sret_ref = refs[:3]
    sample_out = refs[3:3 + N_SAMPLE_OUT]
    sgt_ref = refs[3 + N_SAMPLE_OUT]
    refs = refs[4 + N_SAMPLE_OUT:]
    z_even = dict(zip(Z_NAMES, refs[:nz]))
    z_odd = dict(zip(Z_NAMES, refs[nz:]))
    s = pl.program_id(0)
    n_blocks = pl.num_programs(0) - 1
    g_p = jnp.minimum(s, n_blocks - 1) // blocks_per_group
    r = jnp.maximum(s - 1, 0)
    g_r = r // blocks_per_group
    t_r = r % blocks_per_seq

    @pl.when(s == 0)
    def _():
        for ref in z_odd.values():
            ref[...] = jnp.zeros_like(ref)

    @pl.when(t_r == 0)
    def _():
        sgt_ref[...] = jnp.zeros_like(sgt_ref)
        sret_ref[...] = jnp.zeros_like(sret_ref)

    def step(z_write, z_read):
        rec = _recur_block(z_read, lg_ref, g_r, gn_ref, sgt_ref, sret_ref, merged_ref)
        prj = _project_block(u_ref, r_ref, w, wup_ref, bg_ref, cos_ref, sin_ref, lg_ref, g_p, z_write)
        upd = _state_update(lg_ref, *sample_in, *sample_out)
        for _ in itertools.zip_longest(prj, rec, upd):
            pass

    @pl.when(s % 2 == 0)
    def _():
        step(z_even, z_odd)

    @pl.when(s % 2 == 1)
    def _():
        step(z_odd, z_even)

    @pl.when((t_r == blocks_per_seq - 1) & (s > 0))
    def _():
        sgla_ref[0, 0] = sgt_ref[...].T


def _norm_kernel(x_ref, g_ref, wr_ref, u_ref, r_ref):
    u = (_rms(x_ref[...]) * g_ref[...]).astype(u_ref.dtype)
    u_ref[...] = u
    r_ref[...] = _gate_code(u, wr_ref)


def _norm_prompt(x, gain, w_f32):
    n, D = x.shape
    tm = min(TM_NORM, n)
    return pl.pallas_call(
        _norm_kernel,
        grid=(n // tm,),
        in_specs=[pl.BlockSpec((tm, D), lambda i: (i, 0)), pl.BlockSpec((1, D), lambda i: (0, 0)),
                  _gate_code_spec()],
        out_specs=[pl.BlockSpec((tm, D), lambda i: (i, 0)), pl.BlockSpec((tm, LANE), lambda i: (i, 0))],
        out_shape=[jax.ShapeDtypeStruct((n, D), BF16), jax.ShapeDtypeStruct((n, LANE), F32)],
        compiler_params=pltpu.CompilerParams(
            dimension_semantics=("arbitrary",), vmem_limit_bytes=VMEM_LIMIT),
        name="norm_prompt",
    )(x, gain, w_f32)


def _mix_prompt(u, r, lg, w_t, wup, bg, gn, cos, sin, sg, sr, rows):
    B, L, D = u.shape
    nt = L // T_BLK
    n_blocks = N_GROUPS * B * nt
    n_req = sg.shape[0]
    assert n_req <= n_blocks + 1
    assert rows.shape == (n_req, 1, ROW_W)

    def req_row(s):
        return (jnp.minimum(s, n_req - 1), 0, 0)

    def req_blk(s):
        return (jnp.minimum(s, n_req - 1), 0, 0, 0)

    def proj_idx(s):
        p = jnp.minimum(s, n_blocks - 1)
        return p // (B * nt), (p // nt) % B, p % nt

    def recur_idx(s):
        r = jnp.maximum(s - 1, 0)
        return r // (B * nt), (r // nt) % B, r % nt

    def out_map(s):
        g, b, t = recur_idx(s)
        return (b, t, g)

    def state_map(s):
        g, b, _ = recur_idx(s)
        return (b, g, 0, 0)

    z_scratch = [pltpu.VMEM(shape, dtype) for _, shape, dtype in Z_BUFFERS]
    return pl.pallas_call(
        functools.partial(_mix_prompt_kernel, blocks_per_group=B * nt, blocks_per_seq=nt),
        grid=(n_blocks + 1,),
        in_specs=[
            pl.BlockSpec(memory_space=pltpu.SMEM),
            pl.BlockSpec((1, T_BLK, D), lambda s: (proj_idx(s)[1], proj_idx(s)[2], 0)),
            pl.BlockSpec((1, T_BLK, LANE), lambda s: (proj_idx(s)[1], proj_idx(s)[2], 0)),
            *_weight_specs(lambda s: proj_idx(s)[0], pipeline_mode=pl.Buffered(1)),
            pl.BlockSpec((1, LANE, GLA_DK), lambda s: (proj_idx(s)[0], 0, 0)),
            pl.BlockSpec((1, 1, GLA_DK), lambda s: (proj_idx(s)[0], 0, 0)),
            pl.BlockSpec((1, GLA_DV), lambda s: (0, 0)),
            pl.BlockSpec((T_BLK, LANE), lambda s: (proj_idx(s)[2], 0)),
            pl.BlockSpec((T_BLK, LANE), lambda s: (proj_idx(s)[2], 0)),
            pl.BlockSpec((1, GLA_HEADS, GLA_DK, GLA_DV), req_blk),
            pl.BlockSpec((1, RET_HEADS, RET_DK, RET_DV), req_blk),
            pl.BlockSpec((1, 1, ROW_W), req_row),
        ],
        out_specs=[
            pl.BlockSpec((1, T_BLK, GW), out_map),
            pl.BlockSpec((1, 1, GLA_DK, GLA_DV), state_map),
            pl.BlockSpec((1, RET_PER_GROUP, RET_DK, RET_DV), state_map),
            pl.BlockSpec((1, GLA_HEADS, GLA_DK, GLA_DV), req_blk),
            pl.BlockSpec((1, RET_HEADS, RET_DK, RET_DV), req_blk),
            pl.BlockSpec((1, 1, O_W), req_row),
        ],
        out_shape=[
            jax.ShapeDtypeStruct((B, L, D_MODEL), BF16),
            jax.ShapeDtypeStruct((B, GLA_HEADS, GLA_DK, GLA_DV), F32),
            jax.ShapeDtypeStruct((B, RET_HEADS, RET_DK, RET_DV), F32),
            jax.ShapeDtypeStruct(sg.shape, F32),
            jax.ShapeDtypeStruct(sr.shape, F32),
            jax.ShapeDtypeStruct((n_req, 1, O_W), F32),
        ],
        scratch_shapes=[pltpu.VMEM((GLA_DV, GLA_DK), F32)] + z_scratch + z_scratch,
        compiler_params=pltpu.CompilerParams(
            dimension_semantics=("arbitrary",), vmem_limit_bytes=VMEM_LIMIT),
        name="mix_prompt",
    )(lg, u, r, *([w_t] * N_W), wup, bg, gn, cos, sin, sg, sr, rows)


def _out_kernel(x_ref, mg_ref, p_ref, wout_ref, nple_ref, wpg_ref, wpp_ref, nfin_ref, y_ref, *,
                final_norm):
    h = _tokens(x_ref) + _dot(mg_ref[...], wout_ref[...])
    hn = (_rms(h) * nple_ref[...]).astype(BF16)
    gate = _sigmoid(_dot(hn, wpg_ref[...]))
    h = h + gate * _dot(_tokens(p_ref).astype(BF16), wpp_ref[...])
    if final_norm:
        h = _rms(h) * nfin_ref[...]
    if len(y_ref.shape) == 2:
        y_ref[...] = h
    else:
        y_ref[:, 0, :] = h


def _out_proj(x, merged, p, w_out, nple, w_pg, w_pp, nfin, final_norm):
    n, D = x.shape[0], x.shape[-1]
    tm = min(TM_OUT, n)
    const = lambda i: (0, 0)

    def token_spec(a):
        return pl.BlockSpec((tm,) + a.shape[1:], lambda i: (i,) + (0,) * (a.ndim - 1))

    return pl.pallas_call(
        functools.partial(_out_kernel, final_norm=final_norm),
        grid=(n // tm,),
        in_specs=[
            token_spec(x),
            token_spec(merged),
            token_spec(p),
            pl.BlockSpec((D, D), const, pipeline_mode=pl.Buffered(1)),
            pl.BlockSpec((1, D), const),
            pl.BlockSpec((D, D), const, pipeline_mode=pl.Buffered(1)),
            pl.BlockSpec((PLE_DIM, D), const, pipeline_mode=pl.Buffered(1)),
            pl.BlockSpec((1, D), const),
        ],
        out_specs=token_spec(x),
        out_shape=jax.ShapeDtypeStruct(x.shape, F32),
        compiler_params=pltpu.CompilerParams(
            dimension_semantics=("arbitrary",), vmem_limit_bytes=VMEM_LIMIT),
        name="out_proj",
    )(x, merged, p, w_out, nple, w_pg, w_pp, nfin)


def _prep_kernel(x_ref, nmix_ref, w_ref, wr_ref, wb_ref, z_ref, r_ref):
    u = (_rms(_tokens(x_ref)) * nmix_ref[...]).astype(BF16)
    wb = w_ref[...].astype(BF16)
    wb_ref[...] = wb.T
    z_ref[...] = _dot_nt(u, wb)

    @pl.when(pl.program_id(0) == 0)
    def _():
        r_ref[...] = _gate_code(u, wr_ref)


def _prep_weights(x, nmix, w_f32):
    n, D = x.shape[0], x.shape[-1]
    lo_chunks = R_START // PREP_ROWS
    assert R_START % PREP_ROWS == 0 and N_PACK % PREP_ROWS == 0 and HI_START % SUBLANE == 0

    def rows(k):
        skip = jnp.where(k >= lo_chunks, GLA_RANK // SUBLANE, 0)
        return (SUBLANE * ((PREP_ROWS // SUBLANE) * k + skip), 0)

    return pl.pallas_call(
        _prep_kernel,
        grid=(N_PACK // PREP_ROWS,),
        in_specs=[
            pl.BlockSpec(x.shape, lambda k: (0,) * x.ndim),
            pl.BlockSpec((1, D), lambda k: (0, 0)),
            pl.BlockSpec((pl.Element(PREP_ROWS), pl.Element(D)), rows),
            _gate_code_spec(),
        ],
        out_specs=[
            pl.BlockSpec((D, PREP_ROWS), lambda k: (0, k)),
            pl.BlockSpec((n, PREP_ROWS), lambda k: (0, k)),
            pl.BlockSpec((n, LANE), lambda k: (0, 0)),
        ],
        out_shape=[
            jax.ShapeDtypeStruct((D, N_PACK), BF16),
            jax.ShapeDtypeStruct((n, N_PACK), F32),
            jax.ShapeDtypeStruct((n, LANE), F32),
        ],
        compiler_params=pltpu.CompilerParams(
            dimension_semantics=("arbitrary",), vmem_limit_bytes=VMEM_LIMIT),
        name="prep_weights",
    )(x, nmix, w_f32, w_f32)


def _sample_transform_kernel(z_ref, r_ref, wup_ref, bg_ref, cos_ref, sin_ref, row_ref):
    def piece(name):
        start, width = W_PIECES[name]
        return z_ref[:, start:start + N_GROUPS * width]

    def put(name, value, h=0):
        off = ROW_OFFS[name] + h * value.shape[-1]
        row_ref[:, 0, off:off + value.shape[-1]] = value

    r = r_ref[...]
    for g in range(N_GROUPS):
        put("dec", jnp.exp(_gla_log_alpha(r, wup_ref[g], bg_ref[g])), g)
    put("qa", piece("qa") * (GLA_DK ** -0.5))
    for name in ("ka", "va", "vb"):
        put(name, piece(name))
    cos = cos_ref[...]
    sin = sin_ref[...]
    q_b = piece("qb")
    k_b = piece("kb")
    for h in range(RET_HEADS):
        sl = slice(h * RET_DK, (h + 1) * RET_DK)
        put("qb", _rotary(q_b[:, sl], cos, sin), h)
        put("kb", _rotary(k_b[:, sl], cos, sin) * (RET_DK ** -0.5), h)


def _sample_transform(z, r, wup, bg, cos, sin):
    n = z.shape[0]
    return pl.pallas_call(
        _sample_transform_kernel,
        out_shape=jax.ShapeDtypeStruct((n, 1, ROW_W), F32),
        compiler_params=pltpu.CompilerParams(vmem_limit_bytes=VMEM_LIMIT),
        name="sample_transform",
    )(z, r, wup, bg, cos, sin)


def _merge_sample_kernel(o_ref, z_ref, gn_ref, mg_ref):
    def gate(name, lo, n):
        start = W_PIECES[name][0] + lo
        return z_ref[:, start:start + n]

    for h in range(GLA_HEADS):
        sl = slice(h * GLA_DV, (h + 1) * GLA_DV)
        g_a = gate("ga", h * GLA_DV, GLA_DV)
        part_a = _sigmoid(gate("ma", h * GW, GW)) * (_rms(o_ref[:, 0, sl]) * gn_ref[...] * (g_a * _sigmoid(g_a)))
        parts_b = []
        for j in range(RET_PER_GROUP):
            lo = h * GW + j * RET_DV
            g_b = gate("gb", lo, RET_DV)
            o_b = o_ref[:, 0, GLA_V + lo:GLA_V + lo + RET_DV]
            parts_b.append(_sigmoid(gate("mb", lo, RET_DV)) * (_rms(o_b) * (g_b * _sigmoid(g_b))))
        mg_ref[:, sl] = (part_a + jnp.concatenate(parts_b, axis=-1)).astype(mg_ref.dtype)


def _merge_sample(o, z_raw, gn):
    n = o.shape[0]
    return pl.pallas_call(
        _merge_sample_kernel,
        out_shape=jax.ShapeDtypeStruct((n, D_MODEL), BF16),
        compiler_params=pltpu.CompilerParams(vmem_limit_bytes=VMEM_LIMIT),
        name="merge_sample",
    )(o, z_raw, gn)


def _rope_tables(pos):
    half = RET_DK // 2
    inv = 1.0 / (ROPE_BASE ** jnp.linspace(0.0, 1.0, half, dtype=jnp.float32))
    ang = pos[:, None] * inv[None, :]
    return jnp.cos(ang), jnp.sin(ang)


def kernel(x_prompt, x_sample, state_gla, state_ret, p_prompt, p_sample, norm_mix, w_in, w_gla_up, b_gla,
           gla_norm, w_out, norm_ple, w_ple_gate, w_ple_proj, norm_final):
    depth = w_in.shape[0]
    Bp, Lp, D = x_prompt.shape
    Bs, Ls, _ = x_sample.shape
    assert Ls == 1 and Lp % T_BLK == 0
    cos_p, sin_p = _rope_tables(jnp.arange(Lp, dtype=jnp.float32))
    cos_s, sin_s = _rope_tables(PAST_LEN + jnp.arange(Ls, dtype=jnp.float32))
    log_gamma = jnp.log(1.0 - jnp.exp2(-5.0 - jnp.arange(RET_HEADS, dtype=jnp.float32)))
    nfin = norm_final.reshape(1, D)

    hp = x_prompt
    hs = x_sample
    gla_p, ret_p, gla_s, ret_s = [], [], [], []
    for i in range(depth):
        last = i == depth - 1
        nmix = norm_mix[i].reshape(1, D)
        nple = norm_ple[i].reshape(1, D)
        gn = gla_norm[i].reshape(1, GLA_DV)
        w_f32 = w_in[i].T
        wup = jnp.pad(w_gla_up[i], ((0, LANE - GLA_RANK), (0, 0))).astype(BF16)
        wup = wup.reshape(LANE, GLA_HEADS, GLA_DK).transpose(1, 0, 2)
        bg = b_gla[i].reshape(GLA_HEADS, 1, GLA_DK)
        w_o = w_out[i].astype(BF16)
        w_pg = w_ple_gate[i].astype(BF16)
        w_pp = w_ple_proj[i].astype(BF16)

        w_t, z_raw, r_s = _prep_weights(hs, nmix, w_f32)
        rows = _sample_transform(z_raw, r_s, wup, bg, cos_s, sin_s)
        u, r = _norm_prompt(hp.reshape(Bp * Lp, D), nmix, w_f32)
        merged, sg, sr, nsg, nsr, o_s = _mix_prompt(
            u.reshape(Bp, Lp, D), r.reshape(Bp, Lp, LANE), log_gamma, w_t, wup, bg, gn, cos_p, sin_p,
            state_gla[i], state_ret[i], rows)
        hp = _out_proj(hp.reshape(Bp * Lp, D), merged.reshape(Bp * Lp, D), p_prompt[i].reshape(Bp * Lp, PLE_DIM),
                       w_o, nple, w_pg, w_pp, nfin, last).reshape(Bp, Lp, D)
        gla_p.append(sg)
        ret_p.append(sr)

        merged_s = _merge_sample(o_s, z_raw, gn)
        hs = _out_proj(hs, merged_s, p_sample[i], w_o, nple, w_pg, w_pp, nfin, last)
        gla_s.append(nsg)
        ret_s.append(nsr)

    return (hp, hs, jnp.stack(gla_p), jnp.stack(ret_p), jnp.stack(gla_s), jnp.stack(ret_s))
```

```python
import functools
import itertools

import jax
import jax.numpy as jnp
import numpy as np
from jax import lax
from jax.experimental import pallas as pl
from jax.experimental.pallas import tpu as pltpu

F32 = jnp.float32
BF16 = jnp.bfloat16

D_MODEL = 2048
PAST_LEN = 16384
PLE_DIM = 256
GLA_HEADS = 4
GLA_DK = 256
GLA_DV = 512
GLA_RANK = 16
GLA_TAU = 16.0
RET_HEADS = 8
RET_DK = 256
RET_DV = 256
ROPE_BASE = 10000.0
EPS = 1e-6

GLA_QK = GLA_HEADS * GLA_DK
GLA_V = GLA_HEADS * GLA_DV
RET_QK = RET_HEADS * RET_DK
RET_V = RET_HEADS * RET_DV
IN_SPLITS = (GLA_QK, GLA_QK, GLA_V, GLA_V, GLA_RANK, RET_QK, RET_QK, RET_V, RET_V, D_MODEL, D_MODEL)
IN_OFFS = tuple(int(v) for v in np.concatenate([[0], np.cumsum(IN_SPLITS)[:-1]]))

N_GROUPS = GLA_HEADS
RET_PER_GROUP = RET_HEADS // N_GROUPS
GW = GLA_DV
LANE = 128
SUBLANE = 8

R_START = IN_OFFS[4]
HI_START = IN_OFFS[5]
N_PACK = HI_START - GLA_RANK + sum(IN_SPLITS[5:])
W_PIECES = {
    "qa": (IN_OFFS[0], GLA_DK),
    "ka": (IN_OFFS[1], GLA_DK),
    "va": (IN_OFFS[2], GLA_DV),
    "ga": (IN_OFFS[3], GLA_DV),
    "qb": (IN_OFFS[5] - GLA_RANK, GW),
    "kb": (IN_OFFS[6] - GLA_RANK, GW),
    "vb": (IN_OFFS[7] - GLA_RANK, GW),
    "gb": (IN_OFFS[8] - GLA_RANK, GW),
    "ma": (IN_OFFS[9] - GLA_RANK, GW),
    "mb": (IN_OFFS[10] - GLA_RANK, GW),
}
W_NAMES = tuple(W_PIECES)

T_BLK = 256
C_GLA = 128
H_GLA = C_GLA // 2
N_CHUNKS = T_BLK // C_GLA
TM_OUT = 512
TM_NORM = 1024
PREP_ROWS = 1024
VMEM_LIMIT = 56 * 1024 * 1024


def _rms(x):
    return x * lax.rsqrt(jnp.mean(x * x, axis=-1, keepdims=True) + EPS)


def _sigmoid(x):
    return 1.0 / (1.0 + jnp.exp(-x))


def _log_sigmoid(x):
    return jnp.minimum(x, 0.0) - jnp.log(1.0 + jnp.exp(-jnp.abs(x)))


def _tokens(ref):
    return ref[...] if len(ref.shape) == 2 else ref[:, 0, :]


def _dot(a, b):
    return jnp.dot(a, b, preferred_element_type=F32)


def _dot_nt(a, b):
    return lax.dot_general(a, b, (((1,), (1,)), ((), ())), preferred_element_type=F32)


def _dot_tn(a, b):
    return lax.dot_general(a, b, (((0,), (0,)), ((), ())), preferred_element_type=F32)


def _rotary(x, cos, sin):
    half = x.shape[-1] // 2
    x1, x2 = x[:, :half], x[:, half:]
    return jnp.concatenate([x1 * cos - x2 * sin, x1 * sin + x2 * cos], axis=-1)


def _gla_log_alpha(r, wup, bg):
    pre = _dot(r.astype(BF16), wup) + bg
    return _log_sigmoid(pre) * (1.0 / GLA_TAU)


def _weight_specs(group_of, **kw):
    specs = []
    for n in W_NAMES:
        start, width = W_PIECES[n]
        assert start % width == 0

        def index_map(*idx, first=start // width):
            return (0, first + group_of(*idx))

        specs.append(pl.BlockSpec((D_MODEL, width), index_map, **kw))
    return specs


def _gate_code_spec():
    return pl.BlockSpec((pl.Element(GLA_RANK), pl.Element(D_MODEL)), lambda *idx: (R_START, 0))


def _gate_code(u, wr_ref):
    wr = jnp.concatenate([wr_ref[...].astype(BF16), jnp.zeros((LANE - GLA_RANK, D_MODEL), BF16)], axis=0)
    return _dot_nt(u, wr)


N_W = len(W_NAMES)


Z_BUFFERS = (
    ("qx", (T_BLK, GLA_DK), BF16),
    ("kx", (T_BLK, GLA_DK), BF16),
    ("qc", (N_CHUNKS, H_GLA, GLA_DK), BF16),
    ("kc", (N_CHUNKS, H_GLA, GLA_DK), BF16),
    ("qi", (T_BLK, GLA_DK), BF16),
    ("ko", (T_BLK, GLA_DK), BF16),
    ("va", (T_BLK, GLA_DV), BF16),
    ("dl", (SUBLANE, GLA_DK), F32),
    ("gta", (T_BLK, GW), F32),
    ("qb", (RET_PER_GROUP, T_BLK, RET_DK), BF16),
    ("kb", (RET_PER_GROUP, T_BLK, RET_DK), BF16),
    ("qbi", (RET_PER_GROUP, T_BLK, RET_DK), BF16),
    ("kbo", (RET_PER_GROUP, T_BLK, RET_DK), BF16),
    ("vb", (RET_PER_GROUP, T_BLK, RET_DV), BF16),
    ("gtb", (T_BLK, GW), F32),
)
Z_NAMES = tuple(n for n, _, _ in Z_BUFFERS)
assert N_CHUNKS <= SUBLANE


VT_DEC = 0
VT_KA = VT_DEC + GLA_HEADS
VT_QA = VT_KA + GLA_HEADS
VT_KB = VT_QA + GLA_HEADS
VT_QB = VT_KB + RET_HEADS
VT_N = VT_QB + RET_HEADS
VT_PAD = 32


ROW_PIECES = (("dec", GLA_HEADS, GLA_DK), ("ka", GLA_HEADS, GLA_DK), ("qa", GLA_HEADS, GLA_DK),
              ("kb", RET_HEADS, RET_DK), ("qb", RET_HEADS, RET_DK), ("va", GLA_HEADS, GLA_DV),
              ("vb", RET_HEADS, RET_DV))
ROW_OFFS = dict(zip([n for n, _, _ in ROW_PIECES],
                    np.concatenate([[0], np.cumsum([h * w for _, h, w in ROW_PIECES])[:-1]]).tolist()))
ROW_W = sum(h * w for _, h, w in ROW_PIECES)
O_W = GLA_V + RET_V


def _state_update(lg_ref, sg_ref, sr_ref, row_ref, nsg_ref, nsr_ref, o_ref):
    def vec(name, h, width):
        off = ROW_OFFS[name] + h * width
        return row_ref[0, :, off:off + width]

    rows = [vec(name, h, GLA_DK) for name, heads, _ in ROW_PIECES[:5] for h in range(heads)]
    rows.append(jnp.zeros((VT_PAD - VT_N, GLA_DK), F32))
    vt = jnp.concatenate(rows, axis=0).T

    def col(i):
        return vt[:, i:i + 1]

    for h in range(GLA_HEADS):
        s_new = col(VT_DEC + h) * sg_ref[0, h] + col(VT_KA + h) * vec("va", h, GLA_DV)
        nsg_ref[0, h] = s_new
        o_ref[0, :, h * GLA_DV:(h + 1) * GLA_DV] = jnp.sum(col(VT_QA + h) * s_new, axis=0, keepdims=True)
        yield
    for h in range(RET_HEADS):
        gamma = jnp.exp(jnp.full((1, RET_DV), lg_ref[h], F32))
        s_new = gamma * sr_ref[0, h] + col(VT_KB + h) * vec("vb", h, RET_DV)
        nsr_ref[0, h] = s_new
        o_ref[0, :, GLA_V + h * RET_DV:GLA_V + (h + 1) * RET_DV] = jnp.sum(
            col(VT_QB + h) * s_new, axis=0, keepdims=True)
        yield


N_SAMPLE_IN = 3
N_SAMPLE_OUT = 3


def _project_block(u_ref, r_ref, w, wup_ref, bg_ref, cos_ref, sin_ref, lg_ref, g, z):
    T = T_BLK
    u = u_ref[0]

    def proj(name, j=0, n=None):
        ref = w[name]
        n = ref.shape[1] if n is None else n
        return _dot(u, ref[:, j * n:(j + 1) * n])

    la = _gla_log_alpha(r_ref[0], wup_ref[0], bg_ref[0])
    yield
    q = proj("qa") * (GLA_DK ** -0.5)
    yield
    k = proj("ka")
    ri = lax.broadcasted_iota(jnp.int32, (C_GLA, C_GLA), 0)
    ci = lax.broadcasted_iota(jnp.int32, (C_GLA, C_GLA), 1)
    tri = jnp.where(ri >= ci, 1.0, 0.0).astype(BF16)
    for c in range(N_CHUNKS):
        sl = slice(c * C_GLA, (c + 1) * C_GLA)
        la_c = la[sl]
        la_hi = la_c.astype(BF16)
        la_lo = (la_c - la_hi.astype(F32)).astype(BF16)
        b = _dot(tri, la_hi) + _dot(tri, la_lo)
        b_last = b[C_GLA - 1:C_GLA]
        q_c, k_c = q[sl], k[sl]
        z["qi"][sl] = (q_c * jnp.exp(b)).astype(BF16)
        z["ko"][sl] = (k_c * jnp.exp(b_last - b)).astype(BF16)
        z["dl"][c:c + 1] = jnp.exp(b_last)
        for h in range(2):
            hs = slice(h * H_GLA, (h + 1) * H_GLA)
            rows = slice(c * C_GLA + h * H_GLA, c * C_GLA + (h + 1) * H_GLA)
            b_h = b[hs]
            b_mid = b_h[H_GLA // 2 - 1:H_GLA // 2]
            z["qx"][rows] = (q_c[hs] * jnp.exp(b_h - b_mid)).astype(BF16)
            z["kx"][rows] = (k_c[hs] * jnp.exp(b_mid - b_h)).astype(BF16)
        b_edge = b[H_GLA - 1:H_GLA]
        z["qc"][c] = (q_c[H_GLA:] * jnp.exp(b[H_GLA:] - b_edge)).astype(BF16)
        z["kc"][c] = (k_c[:H_GLA] * jnp.exp(b_edge - b[:H_GLA])).astype(BF16)
    yield
    z["va"][...] = proj("va").astype(BF16)
    yield
    g_a = proj("ga")
    silu_a = g_a * _sigmoid(g_a)
    yield
    z["gta"][...] = _sigmoid(proj("ma")) * silu_a
    yield

    cos = cos_ref[...]
    sin = sin_ref[...]
    tl = lax.broadcasted_iota(jnp.int32, (T, LANE), 0).astype(F32)
    for j in range(RET_PER_GROUP):
        lg = lg_ref[g * RET_PER_GROUP + j]
        q_b = _rotary(proj("qb", j, RET_DK), cos, sin)
        k_b = _rotary(proj("kb", j, RET_DK), cos, sin) * (RET_DK ** -0.5)
        dec_in = jnp.exp((tl + 1.0) * lg)
        dec_out = jnp.exp((T - 1.0 - tl) * lg)
        z["qb"][j] = q_b.astype(BF16)
        z["kb"][j] = k_b.astype(BF16)
        z["qbi"][j] = (q_b * jnp.concatenate([dec_in, dec_in], axis=-1)).astype(BF16)
        z["kbo"][j] = (k_b * jnp.concatenate([dec_out, dec_out], axis=-1)).astype(BF16)
        yield
        z["vb"][j] = proj("vb", j, RET_DV).astype(BF16)
        g_b = proj("gb", j, RET_DV)
        yield
        z["gtb"][:, j * RET_DV:(j + 1) * RET_DV] = _sigmoid(proj("mb", j, RET_DV)) * (g_b * _sigmoid(g_b))
        yield


def _recur_block(z, lg_ref, g, gn_ref, sgt_ref, sret_ref, merged_ref):
    T = T_BLK
    ri = lax.broadcasted_iota(jnp.int32, (H_GLA, H_GLA), 0)
    ci = lax.broadcasted_iota(jnp.int32, (H_GLA, H_GLA), 1)
    causal = ri >= ci
    st = sgt_ref[...]
    o_chunks = []
    for c in range(N_CHUNKS):
        sl = slice(c * C_GLA, (c + 1) * C_GLA)
        v_c = z["va"][sl]
        diag = []
        for h in range(2):
            rows = slice(c * C_GLA + h * H_GLA, c * C_GLA + (h + 1) * H_GLA)
            diag.append(jnp.where(causal, _dot_nt(z["qx"][rows], z["kx"][rows]), 0.0))
        cross = _dot_nt(z["qc"][c], z["kc"][c])
        a = jnp.concatenate([jnp.concatenate([diag[0], jnp.zeros_like(cross)], axis=1),
                             jnp.concatenate([cross, diag[1]], axis=1)], axis=0).astype(BF16)
        o_chunks.append(_dot_nt(z["qi"][sl], st.astype(BF16)) + _dot(a, v_c))
        st = st * z["dl"][c:c + 1] + _dot_tn(v_c, z["ko"][sl])
        yield
    sgt_ref[...] = st
    part_a = z["gta"][...] * (_rms(jnp.concatenate(o_chunks, axis=0)) * gn_ref[...])
    yield

    rt = lax.broadcasted_iota(jnp.int32, (T, T), 0)
    ct = lax.broadcasted_iota(jnp.int32, (T, T), 1)
    parts_b = []
    for j in range(RET_PER_GROUP):
        lg = lg_ref[g * RET_PER_GROUP + j]
        v_b = z["vb"][j]
        dmat = jnp.where(rt >= ct, jnp.exp((rt - ct).astype(F32) * lg), 0.0)
        a = (_dot_nt(z["qb"][j], z["kb"][j]) * dmat).astype(BF16)
        s = sret_ref[0, j]
        o_b = _rms(_dot(z["qbi"][j], s.astype(BF16)) + _dot(a, v_b))
        dec_all = jnp.exp(jnp.full((1, RET_DV), T * lg, F32))
        sret_ref[0, j] = s * dec_all + _dot_tn(z["kbo"][j], v_b)
        parts_b.append(z["gtb"][:, j * RET_DV:(j + 1) * RET_DV] * o_b)
        yield
    merged_ref[0] = (part_a + jnp.concatenate(parts_b, axis=-1)).astype(merged_ref.dtype)


def _mix_prompt_kernel(lg_ref, u_ref, r_ref, *refs, blocks_per_group, blocks_per_seq):
    nw, nz = len(W_NAMES), len(Z_NAMES)
    w = dict(zip(W_NAMES, refs[:nw]))
    refs = list(refs[nw:])
    wup_ref, bg_ref, gn_ref, cos_ref, sin_ref = refs[:5]
    sample_in = refs[5:5 + N_SAMPLE_IN]
    refs = refs[5 + N_SAMPLE_IN:]
    merged_ref, sgla_ref, sret_ref = refs[:3]
    sample_out = refs[3:3 + N_SAMPLE_OUT]
    sgt_ref = refs[3 + N_SAMPLE_OUT]
    refs = refs[4 + N_SAMPLE_OUT:]
    z_even = dict(zip(Z_NAMES, refs[:nz]))
    z_odd = dict(zip(Z_NAMES, refs[nz:]))
    s = pl.program_id(0)
    n_blocks = pl.num_programs(0) - 1
    g_p = jnp.minimum(s, n_blocks - 1) // blocks_per_group
    r = jnp.maximum(s - 1, 0)
    g_r = r // blocks_per_group
    t_r = r % blocks_per_seq

    @pl.when(s == 0)
    def _():
        for ref in z_odd.values():
            ref[...] = jnp.zeros_like(ref)

    @pl.when(t_r == 0)
    def _():
        sgt_ref[...] = jnp.zeros_like(sgt_ref)
        sret_ref[...] = jnp.zeros_like(sret_ref)

    def step(z_write, z_read):
        rec = _recur_block(z_read, lg_ref, g_r, gn_ref, sgt_ref, sret_ref, merged_ref)
        prj = _project_block(u_ref, r_ref, w, wup_ref, bg_ref, cos_ref, sin_ref, lg_ref, g_p, z_write)
        upd = _state_update(lg_ref, *sample_in, *sample_out)
        for _ in itertools.zip_longest(prj, rec, upd):
            pass

    @pl.when(s % 2 == 0)
    def _():
        step(z_even, z_odd)

    @pl.when(s % 2 == 1)
    def _():
        step(z_odd, z_even)

    @pl.when((t_r == blocks_per_seq - 1) & (s > 0))
    def _():
        sgla_ref[0, 0] = sgt_ref[...].T


def _norm_kernel(x_ref, g_ref, wr_ref, u_ref, r_ref):
    u = (_rms(x_ref[...]) * g_ref[...]).astype(u_ref.dtype)
    u_ref[...] = u
    r_ref[...] = _gate_code(u, wr_ref)


def _norm_prompt(x, gain, w_f32):
    n, D = x.shape
    tm = min(TM_NORM, n)
    return pl.pallas_call(
        _norm_kernel,
        grid=(n // tm,),
        in_specs=[pl.BlockSpec((tm, D), lambda i: (i, 0)), pl.BlockSpec((1, D), lambda i: (0, 0)),
                  _gate_code_spec()],
        out_specs=[pl.BlockSpec((tm, D), lambda i: (i, 0)), pl.BlockSpec((tm, LANE), lambda i: (i, 0))],
        out_shape=[jax.ShapeDtypeStruct((n, D), BF16), jax.ShapeDtypeStruct((n, LANE), F32)],
        compiler_params=pltpu.CompilerParams(
            dimension_semantics=("arbitrary",), vmem_limit_bytes=VMEM_LIMIT),
        name="norm_prompt",
    )(x, gain, w_f32)


def _mix_prompt(u, r, lg, w_t, wup, bg, gn, cos, sin, sg, sr, rows):
    B, L, D = u.shape
    nt = L // T_BLK
    n_blocks = N_GROUPS * B * nt
    n_req = sg.shape[0]
    assert n_req <= n_blocks + 1
    assert rows.shape == (n_req, 1, ROW_W)

    def req_row(s):
        return (jnp.minimum(s, n_req - 1), 0, 0)

    def req_blk(s):
        return (jnp.minimum(s, n_req - 1), 0, 0, 0)

    def proj_idx(s):
        p = jnp.minimum(s, n_blocks - 1)
        return p // (B * nt), (p // nt) % B, p % nt

    def recur_idx(s):
        r = jnp.maximum(s - 1, 0)
        return r // (B * nt), (r // nt) % B, r % nt

    def out_map(s):
        g, b, t = recur_idx(s)
        return (b, t, g)

    def state_map(s):
        g, b, _ = recur_idx(s)
        return (b, g, 0, 0)

    z_scratch = [pltpu.VMEM(shape, dtype) for _, shape, dtype in Z_BUFFERS]
    return pl.pallas_call(
        functools.partial(_mix_prompt_kernel, blocks_per_group=B * nt, blocks_per_seq=nt),
        grid=(n_blocks + 1,),
        in_specs=[
            pl.BlockSpec(memory_space=pltpu.SMEM),
            pl.BlockSpec((1, T_BLK, D), lambda s: (proj_idx(s)[1], proj_idx(s)[2], 0)),
            pl.BlockSpec((1, T_BLK, LANE), lambda s: (proj_idx(s)[1], proj_idx(s)[2], 0)),
            *_weight_specs(lambda s: proj_idx(s)[0], pipeline_mode=pl.Buffered(1)),
            pl.BlockSpec((1, LANE, GLA_DK), lambda s: (proj_idx(s)[0], 0, 0)),
            pl.BlockSpec((1, 1, GLA_DK), lambda s: (proj_idx(s)[0], 0, 0)),
            pl.BlockSpec((1, GLA_DV), lambda s: (0, 0)),
            pl.BlockSpec((T_BLK, LANE), lambda s: (proj_idx(s)[2], 0)),
            pl.BlockSpec((T_BLK, LANE), lambda s: (proj_idx(s)[2], 0)),
            pl.BlockSpec((1, GLA_HEADS, GLA_DK, GLA_DV), req_blk),
            pl.BlockSpec((1, RET_HEADS, RET_DK, RET_DV), req_blk),
            pl.BlockSpec((1, 1, ROW_W), req_row),
        ],
        out_specs=[
            pl.BlockSpec((1, T_BLK, GW), out_map),
            pl.BlockSpec((1, 1, GLA_DK, GLA_DV), state_map),
            pl.BlockSpec((1, RET_PER_GROUP, RET_DK, RET_DV), state_map),
            pl.BlockSpec((1, GLA_HEADS, GLA_DK, GLA_DV), req_blk),
            pl.BlockSpec((1, RET_HEADS, RET_DK, RET_DV), req_blk),
            pl.BlockSpec((1, 1, O_W), req_row),
        ],
        out_shape=[
            jax.ShapeDtypeStruct((B, L, D_MODEL), BF16),
            jax.ShapeDtypeStruct((B, GLA_HEADS, GLA_DK, GLA_DV), F32),
            jax.ShapeDtypeStruct((B, RET_HEADS, RET_DK, RET_DV), F32),
            jax.ShapeDtypeStruct(sg.shape, F32),
            jax.ShapeDtypeStruct(sr.shape, F32),
            jax.ShapeDtypeStruct((n_req, 1, O_W), F32),
        ],
        scratch_shapes=[pltpu.VMEM((GLA_DV, GLA_DK), F32)] + z_scratch + z_scratch,
        compiler_params=pltpu.CompilerParams(
            dimension_semantics=("arbitrary",), vmem_limit_bytes=VMEM_LIMIT),
        name="mix_prompt",
    )(lg, u, r, *([w_t] * N_W), wup, bg, gn, cos, sin, sg, sr, rows)


def _out_kernel(x_ref, mg_ref, p_ref, wout_ref, nple_ref, wpg_ref, wpp_ref, nfin_ref, y_ref, *,
                final_norm):
    h = _tokens(x_ref) + _dot(mg_ref[...], wout_ref[...])
    hn = (_rms(h) * nple_ref[...]).astype(BF16)
    gate = _sigmoid(_dot(hn, wpg_ref[...]))
    h = h + gate * _dot(_tokens(p_ref).astype(BF16), wpp_ref[...])
    if final_norm:
        h = _rms(h) * nfin_ref[...]
    if len(y_ref.shape) == 2:
        y_ref[...] = h
    else:
        y_ref[:, 0, :] = h


def _out_proj(x, merged, p, w_out, nple, w_pg, w_pp, nfin, final_norm):
    n, D = x.shape[0], x.shape[-1]
    tm = min(TM_OUT, n)
    const = lambda i: (0, 0)

    def token_spec(a):
        return pl.BlockSpec((tm,) + a.shape[1:], lambda i: (i,) + (0,) * (a.ndim - 1))

    return pl.pallas_call(
        functools.partial(_out_kernel, final_norm=final_norm),
        grid=(n // tm,),
        in_specs=[
            token_spec(x),
            token_spec(merged),
            token_spec(p),
            pl.BlockSpec((D, D), const, pipeline_mode=pl.Buffered(1)),
            pl.BlockSpec((1, D), const),
            pl.BlockSpec((D, D), const, pipeline_mode=pl.Buffered(1)),
            pl.BlockSpec((PLE_DIM, D), const, pipeline_mode=pl.Buffered(1)),
            pl.BlockSpec((1, D), const),
        ],
        out_specs=token_spec(x),
        out_shape=jax.ShapeDtypeStruct(x.shape, F32),
        compiler_params=pltpu.CompilerParams(
            dimension_semantics=("arbitrary",), vmem_limit_bytes=VMEM_LIMIT),
        name="out_proj",
    )(x, merged, p, w_out, nple, w_pg, w_pp, nfin)


def _prep_kernel(x_ref, nmix_ref, w_ref, wr_ref, wb_ref, z_ref, r_ref):
    u = (_rms(_tokens(x_ref)) * nmix_ref[...]).astype(BF16)
    wb = w_ref[...].astype(BF16)
    wb_ref[...] = wb.T
    z_ref[...] = _dot_nt(u, wb)

    @pl.when(pl.program_id(0) == 0)
    def _():
        r_ref[...] = _gate_code(u, wr_ref)


def _prep_weights(x, nmix, w_f32):
    n, D = x.shape[0], x.shape[-1]
    lo_chunks = R_START // PREP_ROWS
    assert R_START % PREP_ROWS == 0 and N_PACK % PREP_ROWS == 0 and HI_START % SUBLANE == 0

    def rows(k):
        skip = jnp.where(k >= lo_chunks, GLA_RANK // SUBLANE, 0)
        return (SUBLANE * ((PREP_ROWS // SUBLANE) * k + skip), 0)

    return pl.pallas_call(
        _prep_kernel,
        grid=(N_PACK // PREP_ROWS,),
        in_specs=[
            pl.BlockSpec(x.shape, lambda k: (0,) * x.ndim),
            pl.BlockSpec((1, D), lambda k: (0, 0)),
            pl.BlockSpec((pl.Element(PREP_ROWS), pl.Element(D)), rows),
            _gate_code_spec(),
        ],
        out_specs=[
            pl.BlockSpec((D, PREP_ROWS), lambda k: (0, k)),
            pl.BlockSpec((n, PREP_ROWS), lambda k: (0, k)),
            pl.BlockSpec((n, LANE), lambda k: (0, 0)),
        ],
        out_shape=[
            jax.ShapeDtypeStruct((D, N_PACK), BF16),
            jax.ShapeDtypeStruct((n, N_PACK), F32),
            jax.ShapeDtypeStruct((n, LANE), F32),
        ],
        compiler_params=pltpu.CompilerParams(
            dimension_semantics=("arbitrary",), vmem_limit_bytes=VMEM_LIMIT),
        name="prep_weights",
    )(x, nmix, w_f32, w_f32)


def _sample_transform_kernel(z_ref, r_ref, wup_ref, bg_ref, cos_ref, sin_ref, row_ref):
    def piece(name):
        start, width = W_PIECES[name]
        return z_ref[:, start:start + N_GROUPS * width]

    def put(name, value, h=0):
        off = ROW_OFFS[name] + h * value.shape[-1]
        row_ref[:, 0, off:off + value.shape[-1]] = value

    r = r_ref[...]
    for g in range(N_GROUPS):
        put("dec", jnp.exp(_gla_log_alpha(r, wup_ref[g], bg_ref[g])), g)
    put("qa", piece("qa") * (GLA_DK ** -0.5))
    for name in ("ka", "va", "vb"):
        put(name, piece(name))
    cos = cos_ref[...]
    sin = sin_ref[...]
    q_b = piece("qb")
    k_b = piece("kb")
    for h in range(RET_HEADS):
        sl = slice(h * RET_DK, (h + 1) * RET_DK)
        put("qb", _rotary(q_b[:, sl], cos, sin), h)
        put("kb", _rotary(k_b[:, sl], cos, sin) * (RET_DK ** -0.5), h)


def _sample_transform(z, r, wup, bg, cos, sin):
    n = z.shape[0]
    return pl.pallas_call(
        _sample_transform_kernel,
        out_shape=jax.ShapeDtypeStruct((n, 1, ROW_W), F32),
        compiler_params=pltpu.CompilerParams(vmem_limit_bytes=VMEM_LIMIT),
        name="sample_transform",
    )(z, r, wup, bg, cos, sin)


def _merge_sample_kernel(o_ref, z_ref, gn_ref, mg_ref):
    def gate(name, lo, n):
        start = W_PIECES[name][0] + lo
        return z_ref[:, start:start + n]

    for h in range(GLA_HEADS):
        sl = slice(h * GLA_DV, (h + 1) * GLA_DV)
        g_a = gate("ga", h * GLA_DV, GLA_DV)
        part_a = _sigmoid(gate("ma", h * GW, GW)) * (_rms(o_ref[:, 0, sl]) * gn_ref[...] * (g_a * _sigmoid(g_a)))
        parts_b = []
        for j in range(RET_PER_GROUP):
            lo = h * GW + j * RET_DV
            g_b = gate("gb", lo, RET_DV)
            o_b = o_ref[:, 0, GLA_V + lo:GLA_V + lo + RET_DV]
            parts_b.append(_sigmoid(gate("mb", lo, RET_DV)) * (_rms(o_b) * (g_b * _sigmoid(g_b))))
        mg_ref[:, sl] = (part_a + jnp.concatenate(parts_b, axis=-1)).astype(mg_ref.dtype)


def _merge_sample(o, z_raw, gn):
    n = o.shape[0]
    return pl.pallas_call(
        _merge_sample_kernel,
        out_shape=jax.ShapeDtypeStruct((n, D_MODEL), BF16),
        compiler_params=pltpu.CompilerParams(vmem_limit_bytes=VMEM_LIMIT),
        name="merge_sample",
    )(o, z_raw, gn)


def _rope_tables(pos):
    half = RET_DK // 2
    inv = 1.0 / (ROPE_BASE ** jnp.linspace(0.0, 1.0, half, dtype=jnp.float32))
    ang = pos[:, None] * inv[None, :]
    return jnp.cos(ang), jnp.sin(ang)


def kernel(x_prompt, x_sample, state_gla, state_ret, p_prompt, p_sample, norm_mix, w_in, w_gla_up, b_gla,
           gla_norm, w_out, norm_ple, w_ple_gate, w_ple_proj, norm_final):
    depth = w_in.shape[0]
    Bp, Lp, D = x_prompt.shape
    Bs, Ls, _ = x_sample.shape
    assert Ls == 1 and Lp % T_BLK == 0
    cos_p, sin_p = _rope_tables(jnp.arange(Lp, dtype=jnp.float32))
    cos_s, sin_s = _rope_tables(PAST_LEN + jnp.arange(Ls, dtype=jnp.float32))
    log_gamma = jnp.log(1.0 - jnp.exp2(-5.0 - jnp.arange(RET_HEADS, dtype=jnp.float32)))
    nfin = norm_final.reshape(1, D)

    hp = x_prompt
    hs = x_sample
    gla_p, ret_p, gla_s, ret_s = [], [], [], []
    for i in range(depth):
        last = i == depth - 1
        nmix = norm_mix[i].reshape(1, D)
        nple = norm_ple[i].reshape(1, D)
        gn = gla_norm[i].reshape(1, GLA_DV)
        w_f32 = w_in[i].T
        wup = jnp.pad(w_gla_up[i], ((0, LANE - GLA_RANK), (0, 0))).astype(BF16)
        wup = wup.reshape(LANE, GLA_HEADS, GLA_DK).transpose(1, 0, 2)
        bg = b_gla[i].reshape(GLA_HEADS, 1, GLA_DK)
        w_o = w_out[i].astype(BF16)
        w_pg = w_ple_gate[i].astype(BF16)
        w_pp = w_ple_proj[i].astype(BF16)

        w_t, z_raw, r_s = _prep_weights(hs, nmix, w_f32)
        rows = _sample_transform(z_raw, r_s, wup, bg, cos_s, sin_s)
        u, r = _norm_prompt(hp.reshape(Bp * Lp, D), nmix, w_f32)
        merged, sg, sr, nsg, nsr, o_s = _mix_prompt(
            u.reshape(Bp, Lp, D), r.reshape(Bp, Lp, LANE), log_gamma, w_t, wup, bg, gn, cos_p, sin_p,
            state_gla[i], state_ret[i], rows)
        hp = _out_proj(hp.reshape(Bp * Lp, D), merged.reshape(Bp * Lp, D), p_prompt[i].reshape(Bp * Lp, PLE_DIM),
                       w_o, nple, w_pg, w_pp, nfin, last).reshape(Bp, Lp, D)
        gla_p.append(sg)
        ret_p.append(sr)

        merged_s = _merge_sample(o_s, z_raw, gn)
        hs = _out_proj(hs, merged_s, p_sample[i], w_o, nple, w_pg, w_pp, nfin, last)
        gla_s.append(nsg)
        ret_s.append(nsr)

    return (hp, hs, jnp.stack(gla_p), jnp.stack(ret_p), jnp.stack(gla_s), jnp.stack(ret_s))
```

```python
import functools
import itertools

import jax
import jax.numpy as jnp
import numpy as np
from jax import lax
from jax.experimental import pallas as pl
from jax.experimental.pallas import tpu as pltpu

F32 = jnp.float32
BF16 = jnp.bfloat16

D_MODEL = 2048
PAST_LEN = 16384
PLE_DIM = 256
GLA_HEADS = 4
GLA_DK = 256
GLA_DV = 512
GLA_RANK = 16
GLA_TAU = 16.0
RET_HEADS = 8
RET_DK = 256
RET_DV = 256
ROPE_BASE = 10000.0
EPS = 1e-6

GLA_QK = GLA_HEADS * GLA_DK
GLA_V = GLA_HEADS * GLA_DV
RET_QK = RET_HEADS * RET_DK
RET_V = RET_HEADS * RET_DV
IN_SPLITS = (GLA_QK, GLA_QK, GLA_V, GLA_V, GLA_RANK, RET_QK, RET_QK, RET_V, RET_V, D_MODEL, D_MODEL)
IN_OFFS = tuple(int(v) for v in np.concatenate([[0], np.cumsum(IN_SPLITS)[:-1]]))

N_GROUPS = GLA_HEADS
RET_PER_GROUP = RET_HEADS // N_GROUPS
GW = GLA_DV
LANE = 128
SUBLANE = 8
ROW_TILE_BF16 = 16

R_START = IN_OFFS[4]
HI_START = IN_OFFS[5]
N_PACK = HI_START - GLA_RANK + sum(IN_SPLITS[5:])
W_PIECES = {
    "qa": (IN_OFFS[0], GLA_DK),
    "ka": (IN_OFFS[1], GLA_DK),
    "va": (IN_OFFS[2], GLA_DV),
    "ga": (IN_OFFS[3], GLA_DV),
    "qb": (IN_OFFS[5] - GLA_RANK, GW),
    "kb": (IN_OFFS[6] - GLA_RANK, GW),
    "vb": (IN_OFFS[7] - GLA_RANK, GW),
    "gb": (IN_OFFS[8] - GLA_RANK, GW),
    "ma": (IN_OFFS[9] - GLA_RANK, GW),
    "mb": (IN_OFFS[10] - GLA_RANK, GW),
}
W_NAMES = tuple(W_PIECES)

T_BLK = 256
C_GLA = 128
H_GLA = C_GLA // 2
N_CHUNKS = T_BLK // C_GLA
TM_OUT = 512
TM_NORM = 1024
PREP_ROWS = 1024
VMEM_LIMIT = 56 * 1024 * 1024


def _rms(x):
    return x * lax.rsqrt(jnp.mean(x * x, axis=-1, keepdims=True) + EPS)


def _sigmoid(x):
    return 1.0 / (1.0 + jnp.exp(-x))


def _log_sigmoid(x):
    return jnp.minimum(x, 0.0) - jnp.log(1.0 + jnp.exp(-jnp.abs(x)))


def _tokens(ref):
    return ref[...] if len(ref.shape) == 2 else ref[:, 0, :]


def _dot(a, b):
    return jnp.dot(a, b, preferred_element_type=F32)


def _dot_nt(a, b):
    return lax.dot_general(a, b, (((1,), (1,)), ((), ())), preferred_element_type=F32)


def _dot_tn(a, b):
    return lax.dot_general(a, b, (((0,), (0,)), ((), ())), preferred_element_type=F32)


def _rotary(x, cos, sin):
    half = x.shape[-1] // 2
    x1, x2 = x[:, :half], x[:, half:]
    return jnp.concatenate([x1 * cos - x2 * sin, x1 * sin + x2 * cos], axis=-1)


def _gla_log_alpha(r, wup, bg):
    pre = _dot(r.astype(BF16), wup) + bg
    return _log_sigmoid(pre) * (1.0 / GLA_TAU)


def _weight_specs(group_of, **kw):
    specs = []
    for n in W_NAMES:
        start, width = W_PIECES[n]
        assert start % width == 0

        def index_map(*idx, first=start // width):
            return (0, first + group_of(*idx))

        specs.append(pl.BlockSpec((D_MODEL, width), index_map, **kw))
    return specs


def _gate_code_spec():
    return pl.BlockSpec((pl.Element(GLA_RANK), pl.Element(D_MODEL)), lambda *idx: (R_START, 0))


def _gate_code(u, wr_ref):
    wr = jnp.concatenate([wr_ref[...].astype(BF16), jnp.zeros((LANE - GLA_RANK, D_MODEL), BF16)], axis=0)
    return _dot_nt(u, wr)


N_W = len(W_NAMES)


Z_BUFFERS = (
    ("qx", (T_BLK, GLA_DK), BF16),
    ("kx", (T_BLK, GLA_DK), BF16),
    ("qc", (N_CHUNKS, H_GLA, GLA_DK), BF16),
    ("kc", (N_CHUNKS, H_GLA, GLA_DK), BF16),
    ("qi", (T_BLK, GLA_DK), BF16),
    ("ko", (T_BLK, GLA_DK), BF16),
    ("va", (T_BLK, GLA_DV), BF16),
    ("dl", (SUBLANE, GLA_DK), F32),
    ("gta", (T_BLK, GW), F32),
    ("qb", (RET_PER_GROUP, T_BLK, RET_DK), BF16),
    ("kb", (RET_PER_GROUP, T_BLK, RET_DK), BF16),
    ("qbi", (RET_PER_GROUP, T_BLK, RET_DK), BF16),
    ("kbo", (RET_PER_GROUP, T_BLK, RET_DK), BF16),
    ("vb", (RET_PER_GROUP, T_BLK, RET_DV), BF16),
    ("gtb", (T_BLK, GW), F32),
)
Z_NAMES = tuple(n for n, _, _ in Z_BUFFERS)
assert N_CHUNKS <= SUBLANE


VT_DEC = 0
VT_KA = VT_DEC + GLA_HEADS
VT_QA = VT_KA + GLA_HEADS
VT_KB = VT_QA + GLA_HEADS
VT_QB = VT_KB + RET_HEADS
VT_N = VT_QB + RET_HEADS
VT_PAD = -(-VT_N // SUBLANE) * SUBLANE


ROW_PIECES = (("dec", GLA_HEADS, GLA_DK), ("ka", GLA_HEADS, GLA_DK), ("qa", GLA_HEADS, GLA_DK),
              ("kb", RET_HEADS, RET_DK), ("qb", RET_HEADS, RET_DK), ("va", GLA_HEADS, GLA_DV),
              ("vb", RET_HEADS, RET_DV))
ROW_OFFS = dict(zip([n for n, _, _ in ROW_PIECES],
                    np.concatenate([[0], np.cumsum([h * w for _, h, w in ROW_PIECES])[:-1]]).tolist()))
ROW_W = sum(h * w for _, h, w in ROW_PIECES)
O_W = GLA_V + RET_V


def _state_update(lg_ref, sg_ref, sr_ref, row_ref, nsg_ref, nsr_ref, o_ref):
    def vec(name, h, width):
        off = ROW_OFFS[name] + h * width
        return row_ref[0, :, off:off + width]

    rows = [vec(name, h, GLA_DK) for name, heads, _ in ROW_PIECES[:5] for h in range(heads)]
    rows.append(jnp.zeros((VT_PAD - VT_N, GLA_DK), F32))
    vt = jnp.concatenate(rows, axis=0).T

    def col(i):
        return vt[:, i:i + 1]

    for h in range(GLA_HEADS):
        s_new = col(VT_DEC + h) * sg_ref[0, h] + col(VT_KA + h) * vec("va", h, GLA_DV)
        nsg_ref[0, h] = s_new
        o_ref[0, :, h * GLA_DV:(h + 1) * GLA_DV] = jnp.sum(col(VT_QA + h) * s_new, axis=0, keepdims=True)
        yield
    for h in range(RET_HEADS):
        gamma = jnp.exp(jnp.full((1, RET_DV), lg_ref[h], F32))
        s_new = gamma * sr_ref[0, h] + col(VT_KB + h) * vec("vb", h, RET_DV)
        nsr_ref[0, h] = s_new
        o_ref[0, :, GLA_V + h * RET_DV:GLA_V + (h + 1) * RET_DV] = jnp.sum(
            col(VT_QB + h) * s_new, axis=0, keepdims=True)
        yield


N_SAMPLE_IN = 3
N_SAMPLE_OUT = 3
N_CAST = 2


def _project_block(u_ref, r_ref, w, wup_ref, bg_ref, cos_ref, sin_ref, lg_ref, g, z):
    T = T_BLK
    u = u_ref[0]

    def proj(name, j=0, n=None):
        ref = w[name]
        n = ref.shape[1] if n is None else n
        return _dot(u, ref[:, j * n:(j + 1) * n])

    la = _gla_log_alpha(r_ref[0], wup_ref[0], bg_ref[0])
    yield
    q = proj("qa") * (GLA_DK ** -0.5)
    yield
    k = proj("ka")
    ri = lax.broadcasted_iota(jnp.int32, (C_GLA, C_GLA), 0)
    ci = lax.broadcasted_iota(jnp.int32, (C_GLA, C_GLA), 1)
    tri = jnp.where(ri >= ci, 1.0, 0.0).astype(BF16)
    for c in range(N_CHUNKS):
        sl = slice(c * C_GLA, (c + 1) * C_GLA)
        la_c = la[sl]
        la_hi = la_c.astype(BF16)
        la_lo = (la_c - la_hi.astype(F32)).astype(BF16)
        b = _dot(tri, la_hi) + _dot(tri, la_lo)
        b_last = b[C_GLA - 1:C_GLA]
        q_c, k_c = q[sl], k[sl]
        z["qi"][sl] = (q_c * jnp.exp(b)).astype(BF16)
        z["ko"][sl] = (k_c * jnp.exp(b_last - b)).astype(BF16)
        z["dl"][c:c + 1] = jnp.exp(b_last)
        for h in range(2):
            hs = slice(h * H_GLA, (h + 1) * H_GLA)
            rows = slice(c * C_GLA + h * H_GLA, c * C_GLA + (h + 1) * H_GLA)
            b_h = b[hs]
            b_mid = b_h[H_GLA // 2 - 1:H_GLA // 2]
            z["qx"][rows] = (q_c[hs] * jnp.exp(b_h - b_mid)).astype(BF16)
            z["kx"][rows] = (k_c[hs] * jnp.exp(b_mid - b_h)).astype(BF16)
        b_edge = b[H_GLA - 1:H_GLA]
        z["qc"][c] = (q_c[H_GLA:] * jnp.exp(b[H_GLA:] - b_edge)).astype(BF16)
        z["kc"][c] = (k_c[:H_GLA] * jnp.exp(b_edge - b[:H_GLA])).astype(BF16)
    yield
    z["va"][...] = proj("va").astype(BF16)
    yield
    g_a = proj("ga")
    silu_a = g_a * _sigmoid(g_a)
    yield
    z["gta"][...] = _sigmoid(proj("ma")) * silu_a
    yield

    cos = cos_ref[...]
    sin = sin_ref[...]
    tl = lax.broadcasted_iota(jnp.int32, (T, LANE), 0).astype(F32)
    for j in range(RET_PER_GROUP):
        lg = lg_ref[g * RET_PER_GROUP + j]
        q_b = _rotary(proj("qb", j, RET_DK), cos, sin)
        k_b = _rotary(proj("kb", j, RET_DK), cos, sin) * (RET_DK ** -0.5)
        dec_in = jnp.exp((tl + 1.0) * lg)
        dec_out = jnp.exp((T - 1.0 - tl) * lg)
        z["qb"][j] = q_b.astype(BF16)
        z["kb"][j] = k_b.astype(BF16)
        z["qbi"][j] = (q_b * jnp.concatenate([dec_in, dec_in], axis=-1)).astype(BF16)
        z["kbo"][j] = (k_b * jnp.concatenate([dec_out, dec_out], axis=-1)).astype(BF16)
        yield
        z["vb"][j] = proj("vb", j, RET_DV).astype(BF16)
        g_b = proj("gb", j, RET_DV)
        yield
        z["gtb"][:, j * RET_DV:(j + 1) * RET_DV] = _sigmoid(proj("mb", j, RET_DV)) * (g_b * _sigmoid(g_b))
        yield


def _recur_block(z, lg_ref, g, gn_ref, sgt_ref, sret_ref, merged_ref):
    T = T_BLK
    ri = lax.broadcasted_iota(jnp.int32, (H_GLA, H_GLA), 0)
    ci = lax.broadcasted_iota(jnp.int32, (H_GLA, H_GLA), 1)
    causal = ri >= ci
    st = sgt_ref[...]
    o_chunks = []
    for c in range(N_CHUNKS):
        sl = slice(c * C_GLA, (c + 1) * C_GLA)
        v_c = z["va"][sl]
        diag = []
        for h in range(2):
            rows = slice(c * C_GLA + h * H_GLA, c * C_GLA + (h + 1) * H_GLA)
            diag.append(jnp.where(causal, _dot_nt(z["qx"][rows], z["kx"][rows]), 0.0))
        cross = _dot_nt(z["qc"][c], z["kc"][c])
        a = jnp.concatenate([jnp.concatenate([diag[0], jnp.zeros_like(cross)], axis=1),
                             jnp.concatenate([cross, diag[1]], axis=1)], axis=0).astype(BF16)
        o_chunks.append(_dot_nt(z["qi"][sl], st.astype(BF16)) + _dot(a, v_c))
        st = st * z["dl"][c:c + 1] + _dot_tn(v_c, z["ko"][sl])
        yield
    sgt_ref[...] = st
    part_a = z["gta"][...] * (_rms(jnp.concatenate(o_chunks, axis=0)) * gn_ref[...])
    yield

    rt = lax.broadcasted_iota(jnp.int32, (T, T), 0)
    ct = lax.broadcasted_iota(jnp.int32, (T, T), 1)
    parts_b = []
    for j in range(RET_PER_GROUP):
        lg = lg_ref[g * RET_PER_GROUP + j]
        v_b = z["vb"][j]
        dmat = jnp.where(rt >= ct, jnp.exp((rt - ct).astype(F32) * lg), 0.0)
        a = (_dot_nt(z["qb"][j], z["kb"][j]) * dmat).astype(BF16)
        s = sret_ref[0, j]
        o_b = _rms(_dot(z["qbi"][j], s.astype(BF16)) + _dot(a, v_b))
        dec_all = jnp.exp(jnp.full((1, RET_DV), T * lg, F32))
        sret_ref[0, j] = s * dec_all + _dot_tn(z["kbo"][j], v_b)
        parts_b.append(z["gtb"][:, j * RET_DV:(j + 1) * RET_DV] * o_b)
        yield
    merged_ref[0] = (part_a + jnp.concatenate(parts_b, axis=-1)).astype(merged_ref.dtype)


def _mix_prompt_kernel(lg_ref, u_ref, r_ref, *refs, blocks_per_group, blocks_per_seq):
    nw, nz = len(W_NAMES), len(Z_NAMES)
    w = dict(zip(W_NAMES, refs[:nw]))
    refs = list(refs[nw:])
    wup_ref, bg_ref, gn_ref, cos_ref, sin_ref = refs[:5]
    sample_in = refs[5:5 + N_SAMPLE_IN]
    cast_in = refs[5 + N_SAMPLE_IN:5 + N_SAMPLE_IN + N_CAST]
    refs = refs[5 + N_SAMPLE_IN + N_CAST:]
    merged_ref, sgla_ref, sret_ref = refs[:3]
    sample_out = refs[3:3 + N_SAMPLE_OUT]
    cast_out = refs[3 + N_SAMPLE_OUT:3 + N_SAMPLE_OUT + N_CAST]
    sgt_ref = refs[3 + N_SAMPLE_OUT + N_CAST]
    refs = refs[4 + N_SAMPLE_OUT + N_CAST:]
    z_even = dict(zip(Z_NAMES, refs[:nz]))
    z_odd = dict(zip(Z_NAMES, refs[nz:]))
    s = pl.program_id(0)
    n_blocks = pl.num_programs(0) - 1
    g_p = jnp.minimum(s, n_blocks - 1) // blocks_per_group
    r = jnp.maximum(s - 1, 0)
    g_r = r // blocks_per_group
    t_r = r % blocks_per_seq

    @pl.when(s == 0)
    def _():
        for ref in z_odd.values():
            ref[...] = jnp.zeros_like(ref)

    @pl.when(t_r == 0)
    def _():
        sgt_ref[...] = jnp.zeros_like(sgt_ref)
        sret_ref[...] = jnp.zeros_like(sret_ref)

    for src, dst in zip(cast_in, cast_out):
        dst[...] = src[...].astype(dst.dtype)

    def step(z_write, z_read):
        rec = _recur_block(z_read, lg_ref, g_r, gn_ref, sgt_ref, sret_ref, merged_ref)
        prj = _project_block(u_ref, r_ref, w, wup_ref, bg_ref, cos_ref, sin_ref, lg_ref, g_p, z_write)
        upd = _state_update(lg_ref, *sample_in, *sample_out)
        for _ in itertools.zip_longest(prj, rec, upd):
            pass

    @pl.when(s % 2 == 0)
    def _():
        step(z_even, z_odd)

    @pl.when(s % 2 == 1)
    def _():
        step(z_odd, z_even)

    @pl.when((t_r == blocks_per_seq - 1) & (s > 0))
    def _():
        sgla_ref[0, 0] = sgt_ref[...].T


def _norm_kernel(x_ref, g_ref, wr_ref, u_ref, r_ref):
    u = (_rms(x_ref[...]) * g_ref[...]).astype(u_ref.dtype)
    u_ref[...] = u
    r_ref[...] = _gate_code(u, wr_ref)


def _norm_prompt(x, gain, w_f32):
    n, D = x.shape
    tm = min(TM_NORM, n)
    return pl.pallas_call(
        _norm_kernel,
        grid=(n // tm,),
        in_specs=[pl.BlockSpec((tm, D), lambda i: (i, 0)), pl.BlockSpec((1, D), lambda i: (0, 0)),
                  _gate_code_spec()],
        out_specs=[pl.BlockSpec((tm, D), lambda i: (i, 0)), pl.BlockSpec((tm, LANE), lambda i: (i, 0))],
        out_shape=[jax.ShapeDtypeStruct((n, D), BF16), jax.ShapeDtypeStruct((n, LANE), F32)],
        compiler_params=pltpu.CompilerParams(
            dimension_semantics=("arbitrary",), vmem_limit_bytes=VMEM_LIMIT),
        name="norm_prompt",
    )(x, gain, w_f32)


def _mix_prompt(u, r, lg, w_t, wup, bg, gn, cos, sin, sg, sr, rows, to_cast):
    B, L, D = u.shape
    nt = L // T_BLK
    n_blocks = N_GROUPS * B * nt
    n_req = sg.shape[0]
    assert n_req <= n_blocks + 1
    assert rows.shape == (n_req, 1, ROW_W) and len(to_cast) == N_CAST
    slab = ROW_TILE_BF16
    assert all(a.shape[0] <= slab * n_blocks and a.shape[0] % slab == 0 for a in to_cast)

    def cast_specs():
        return [pl.BlockSpec((slab, a.shape[1]), lambda s, n=a.shape[0] // slab: (jnp.minimum(s, n - 1), 0))
                for a in to_cast]

    def req_row(s):
        return (jnp.minimum(s, n_req - 1), 0, 0)

    def req_blk(s):
        return (jnp.minimum(s, n_req - 1), 0, 0, 0)

    def proj_idx(s):
        p = jnp.minimum(s, n_blocks - 1)
        return p // (B * nt), (p // nt) % B, p % nt

    def recur_idx(s):
        r = jnp.maximum(s - 1, 0)
        return r // (B * nt), (r // nt) % B, r % nt

    def out_map(s):
        g, b, t = recur_idx(s)
        return (b, t, g)

    def state_map(s):
        g, b, _ = recur_idx(s)
        return (b, g, 0, 0)

    z_scratch = [pltpu.VMEM(shape, dtype) for _, shape, dtype in Z_BUFFERS]
    return pl.pallas_call(
        functools.partial(_mix_prompt_kernel, blocks_per_group=B * nt, blocks_per_seq=nt),
        grid=(n_blocks + 1,),
        in_specs=[
            pl.BlockSpec(memory_space=pltpu.SMEM),
            pl.BlockSpec((1, T_BLK, D), lambda s: (proj_idx(s)[1], proj_idx(s)[2], 0)),
            pl.BlockSpec((1, T_BLK, LANE), lambda s: (proj_idx(s)[1], proj_idx(s)[2], 0)),
            *_weight_specs(lambda s: proj_idx(s)[0], pipeline_mode=pl.Buffered(1)),
            pl.BlockSpec((1, LANE, GLA_DK), lambda s: (proj_idx(s)[0], 0, 0)),
            pl.BlockSpec((1, 1, GLA_DK), lambda s: (proj_idx(s)[0], 0, 0)),
            pl.BlockSpec((1, GLA_DV), lambda s: (0, 0)),
            pl.BlockSpec((T_BLK, LANE), lambda s: (proj_idx(s)[2], 0)),
            pl.BlockSpec((T_BLK, LANE), lambda s: (proj_idx(s)[2], 0)),
            pl.BlockSpec((1, GLA_HEADS, GLA_DK, GLA_DV), req_blk),
            pl.BlockSpec((1, RET_HEADS, RET_DK, RET_DV), req_blk),
            pl.BlockSpec((1, 1, ROW_W), req_row),
            *cast_specs(),
        ],
        out_specs=[
            pl.BlockSpec((1, T_BLK, GW), out_map),
            pl.BlockSpec((1, 1, GLA_DK, GLA_DV), state_map),
            pl.BlockSpec((1, RET_PER_GROUP, RET_DK, RET_DV), state_map),
            pl.BlockSpec((1, GLA_HEADS, GLA_DK, GLA_DV), req_blk),
            pl.BlockSpec((1, RET_HEADS, RET_DK, RET_DV), req_blk),
            pl.BlockSpec((1, 1, O_W), req_row),
            *cast_specs(),
        ],
        out_shape=[
            jax.ShapeDtypeStruct((B, L, D_MODEL), BF16),
            jax.ShapeDtypeStruct((B, GLA_HEADS, GLA_DK, GLA_DV), F32),
            jax.ShapeDtypeStruct((B, RET_HEADS, RET_DK, RET_DV), F32),
            jax.ShapeDtypeStruct(sg.shape, F32),
            jax.ShapeDtypeStruct(sr.shape, F32),
            jax.ShapeDtypeStruct((n_req, 1, O_W), F32),
            *[jax.ShapeDtypeStruct(a.shape, BF16) for a in to_cast],
        ],
        scratch_shapes=[pltpu.VMEM((GLA_DV, GLA_DK), F32)] + z_scratch + z_scratch,
        compiler_params=pltpu.CompilerParams(
            dimension_semantics=("arbitrary",), vmem_limit_bytes=VMEM_LIMIT),
        name="mix_prompt",
    )(lg, u, r, *([w_t] * N_W), wup, bg, gn, cos, sin, sg, sr, rows, *to_cast)


def _out_kernel(x_ref, mg_ref, p_ref, wout_ref, nple_ref, wpg_ref, wpp_ref, nfin_ref, y_ref, *,
                final_norm):
    h = _tokens(x_ref) + _dot(mg_ref[...], wout_ref[...])
    hn = (_rms(h) * nple_ref[...]).astype(BF16)
    gate = _sigmoid(_dot(hn, wpg_ref[...]))
    h = h + gate * _dot(_tokens(p_ref).astype(BF16), wpp_ref[...])
    if final_norm:
        h = _rms(h) * nfin_ref[...]
    if len(y_ref.shape) == 2:
        y_ref[...] = h
    else:
        y_ref[:, 0, :] = h


def _out_proj(x, merged, p, w_out, nple, w_pg, w_pp, nfin, final_norm):
    n, D = x.shape[0], x.shape[-1]
    tm = min(TM_OUT, n)
    const = lambda i: (0, 0)

    def token_spec(a):
        return pl.BlockSpec((tm,) + a.shape[1:], lambda i: (i,) + (0,) * (a.ndim - 1))

    return pl.pallas_call(
        functools.partial(_out_kernel, final_norm=final_norm),
        grid=(n // tm,),
        in_specs=[
            token_spec(x),
            token_spec(merged),
            token_spec(p),
            pl.BlockSpec((D, D), const, pipeline_mode=pl.Buffered(1)),
            pl.BlockSpec((1, D), const),
            pl.BlockSpec((D, D), const, pipeline_mode=pl.Buffered(1)),
            pl.BlockSpec((PLE_DIM, D), const, pipeline_mode=pl.Buffered(1)),
            pl.BlockSpec((1, D), const),
        ],
        out_specs=token_spec(x),
        out_shape=jax.ShapeDtypeStruct(x.shape, F32),
        compiler_params=pltpu.CompilerParams(
            dimension_semantics=("arbitrary",), vmem_limit_bytes=VMEM_LIMIT),
        name="out_proj",
    )(x, merged, p, w_out, nple, w_pg, w_pp, nfin)


def _prep_kernel(x_ref, nmix_ref, w_ref, wr_ref, wb_ref, z_ref, r_ref):
    u = (_rms(_tokens(x_ref)) * nmix_ref[...]).astype(BF16)
    wb = w_ref[...].astype(BF16)
    wb_ref[...] = wb.T
    z_ref[...] = _dot_nt(u, wb)

    @pl.when(pl.program_id(0) == 0)
    def _():
        r_ref[...] = _gate_code(u, wr_ref)


def _prep_weights(x, nmix, w_f32):
    n, D = x.shape[0], x.shape[-1]
    lo_chunks = R_START // PREP_ROWS
    assert R_START % PREP_ROWS == 0 and N_PACK % PREP_ROWS == 0 and HI_START % SUBLANE == 0

    def rows(k):
        skip = jnp.where(k >= lo_chunks, GLA_RANK // SUBLANE, 0)
        return (SUBLANE * ((PREP_ROWS // SUBLANE) * k + skip), 0)

    return pl.pallas_call(
        _prep_kernel,
        grid=(N_PACK // PREP_ROWS,),
        in_specs=[
            pl.BlockSpec(x.shape, lambda k: (0,) * x.ndim),
            pl.BlockSpec((1, D), lambda k: (0, 0)),
            pl.BlockSpec((pl.Element(PREP_ROWS), pl.Element(D)), rows),
            _gate_code_spec(),
        ],
        out_specs=[
            pl.BlockSpec((D, PREP_ROWS), lambda k: (0, k)),
            pl.BlockSpec((n, PREP_ROWS), lambda k: (0, k)),
            pl.BlockSpec((n, LANE), lambda k: (0, 0)),
        ],
        out_shape=[
            jax.ShapeDtypeStruct((D, N_PACK), BF16),
            jax.ShapeDtypeStruct((n, N_PACK), F32),
            jax.ShapeDtypeStruct((n, LANE), F32),
        ],
        compiler_params=pltpu.CompilerParams(
            dimension_semantics=("arbitrary",), vmem_limit_bytes=VMEM_LIMIT),
        name="prep_weights",
    )(x, nmix, w_f32, w_f32)


def _sample_transform_kernel(z_ref, r_ref, wup_ref, bg_ref, cos_ref, sin_ref, row_ref):
    def piece(name):
        start, width = W_PIECES[name]
        return z_ref[:, start:start + N_GROUPS * width]

    def put(name, value, h=0):
        off = ROW_OFFS[name] + h * value.shape[-1]
        row_ref[:, 0, off:off + value.shape[-1]] = value

    r = r_ref[...]
    for g in range(N_GROUPS):
        put("dec", jnp.exp(_gla_log_alpha(r, wup_ref[g], bg_ref[g])), g)
    put("qa", piece("qa") * (GLA_DK ** -0.5))
    for name in ("ka", "va", "vb"):
        put(name, piece(name))
    cos = cos_ref[...]
    sin = sin_ref[...]
    q_b = piece("qb")
    k_b = piece("kb")
    for h in range(RET_HEADS):
        sl = slice(h * RET_DK, (h + 1) * RET_DK)
        put("qb", _rotary(q_b[:, sl], cos, sin), h)
        put("kb", _rotary(k_b[:, sl], cos, sin) * (RET_DK ** -0.5), h)


def _sample_transform(z, r, wup, bg, cos, sin):
    n = z.shape[0]
    return pl.pallas_call(
        _sample_transform_kernel,
        out_shape=jax.ShapeDtypeStruct((n, 1, ROW_W), F32),
        compiler_params=pltpu.CompilerParams(vmem_limit_bytes=VMEM_LIMIT),
        name="sample_transform",
    )(z, r, wup, bg, cos, sin)


def _merge_sample_kernel(o_ref, z_ref, gn_ref, mg_ref):
    def gate(name, lo, n):
        start = W_PIECES[name][0] + lo
        return z_ref[:, start:start + n]

    for h in range(GLA_HEADS):
        sl = slice(h * GLA_DV, (h + 1) * GLA_DV)
        g_a = gate("ga", h * GLA_DV, GLA_DV)
        part_a = _sigmoid(gate("ma", h * GW, GW)) * (_rms(o_ref[:, 0, sl]) * gn_ref[...] * (g_a * _sigmoid(g_a)))
        parts_b = []
        for j in range(RET_PER_GROUP):
            lo = h * GW + j * RET_DV
            g_b = gate("gb", lo, RET_DV)
            o_b = o_ref[:, 0, GLA_V + lo:GLA_V + lo + RET_DV]
            parts_b.append(_sigmoid(gate("mb", lo, RET_DV)) * (_rms(o_b) * (g_b * _sigmoid(g_b))))
        mg_ref[:, sl] = (part_a + jnp.concatenate(parts_b, axis=-1)).astype(mg_ref.dtype)


def _merge_sample(o, z_raw, gn):
    n = o.shape[0]
    return pl.pallas_call(
        _merge_sample_kernel,
        out_shape=jax.ShapeDtypeStruct((n, D_MODEL), BF16),
        compiler_params=pltpu.CompilerParams(vmem_limit_bytes=VMEM_LIMIT),
        name="merge_sample",
    )(o, z_raw, gn)


def _rope_tables(pos):
    half = RET_DK // 2
    inv = 1.0 / (ROPE_BASE ** jnp.linspace(0.0, 1.0, half, dtype=jnp.float32))
    ang = pos[:, None] * inv[None, :]
    return jnp.cos(ang), jnp.sin(ang)


def kernel(x_prompt, x_sample, state_gla, state_ret, p_prompt, p_sample, norm_mix, w_in, w_gla_up, b_gla,
           gla_norm, w_out, norm_ple, w_ple_gate, w_ple_proj, norm_final):
    depth = w_in.shape[0]
    Bp, Lp, D = x_prompt.shape
    Bs, Ls, _ = x_sample.shape
    assert Ls == 1 and Lp % T_BLK == 0
    cos_p, sin_p = _rope_tables(jnp.arange(Lp, dtype=jnp.float32))
    cos_s, sin_s = _rope_tables(PAST_LEN + jnp.arange(Ls, dtype=jnp.float32))
    log_gamma = jnp.log(1.0 - jnp.exp2(-5.0 - jnp.arange(RET_HEADS, dtype=jnp.float32)))
    nfin = norm_final.reshape(1, D)

    hp = x_prompt
    hs = x_sample
    gla_p, ret_p, gla_s, ret_s = [], [], [], []
    for i in range(depth):
        last = i == depth - 1
        nmix = norm_mix[i].reshape(1, D)
        nple = norm_ple[i].reshape(1, D)
        gn = gla_norm[i].reshape(1, GLA_DV)
        w_f32 = w_in[i].T
        wup = jnp.pad(w_gla_up[i], ((0, LANE - GLA_RANK), (0, 0))).astype(BF16)
        wup = wup.reshape(LANE, GLA_HEADS, GLA_DK).transpose(1, 0, 2)
        bg = b_gla[i].reshape(GLA_HEADS, 1, GLA_DK)
        w_pp = w_ple_proj[i].astype(BF16)

        w_t, z_raw, r_s = _prep_weights(hs, nmix, w_f32)
        rows = _sample_transform(z_raw, r_s, wup, bg, cos_s, sin_s)
        u, r = _norm_prompt(hp.reshape(Bp * Lp, D), nmix, w_f32)
        merged, sg, sr, nsg, nsr, o_s, w_o, w_pg = _mix_prompt(
            u.reshape(Bp, Lp, D), r.reshape(Bp, Lp, LANE), log_gamma, w_t, wup, bg, gn, cos_p, sin_p,
            state_gla[i], state_ret[i], rows, [w_out[i], w_ple_gate[i]])
        hp = _out_proj(hp.reshape(Bp * Lp, D), merged.reshape(Bp * Lp, D), p_prompt[i].reshape(Bp * Lp, PLE_DIM),
                       w_o, nple, w_pg, w_pp, nfin, last).reshape(Bp, Lp, D)
        gla_p.append(sg)
        ret_p.append(sr)

        merged_s = _merge_sample(o_s, z_raw, gn)
        hs = _out_proj(hs, merged_s, p_sample[i], w_o, nple, w_pg, w_pp, nfin, last)
        gla_s.append(nsg)
        ret_s.append(nsr)

    return (hp, hs, jnp.stack(gla_p), jnp.stack(ret_p), jnp.stack(gla_s), jnp.stack(ret_s))
```

```python
import functools
import itertools

import jax
import jax.numpy as jnp
import numpy as np
from jax import lax
from jax.experimental import pallas as pl
from jax.experimental.pallas import tpu as pltpu

F32 = jnp.float32
BF16 = jnp.bfloat16

D_MODEL = 2048
PAST_LEN = 16384
PLE_DIM = 256
GLA_HEADS = 4
GLA_DK = 256
GLA_DV = 512
GLA_RANK = 16
GLA_TAU = 16.0
RET_HEADS = 8
RET_DK = 256
RET_DV = 256
ROPE_BASE = 10000.0
EPS = 1e-6

GLA_QK = GLA_HEADS * GLA_DK
GLA_V = GLA_HEADS * GLA_DV
RET_QK = RET_HEADS * RET_DK
RET_V = RET_HEADS * RET_DV
IN_SPLITS = (GLA_QK, GLA_QK, GLA_V, GLA_V, GLA_RANK, RET_QK, RET_QK, RET_V, RET_V, D_MODEL, D_MODEL)
IN_OFFS = tuple(int(v) for v in np.concatenate([[0], np.cumsum(IN_SPLITS)[:-1]]))

N_GROUPS = GLA_HEADS
RET_PER_GROUP = RET_HEADS // N_GROUPS
GW = GLA_DV
LANE = 128
SUBLANE = 8
ROW_TILE_BF16 = 16

R_START = IN_OFFS[4]
WIDE_PIECES = ("va", "ga", "qb", "kb", "vb", "gb", "ma", "mb")
WIDE_ROWS = tuple(IN_OFFS[i] for i in (2, 3, 5, 6, 7, 8, 9, 10))
W_PIECES = {"qa": (0, GLA_DK), "ka": (GLA_DK, GLA_DK)}
W_PIECES.update({n: (2 * GLA_DK + i * GW, GW) for i, n in enumerate(WIDE_PIECES)})
GROUP_W = 2 * GLA_DK + len(WIDE_PIECES) * GW
N_PACK = N_GROUPS * GROUP_W
assert GW == 2 * GLA_DK

T_BLK = 256
C_GLA = 128
H_GLA = C_GLA // 2
N_CHUNKS = T_BLK // C_GLA
TM_OUT = 512
TM_NORM = 1024
VMEM_LIMIT = 56 * 1024 * 1024


def _rms(x):
    return x * lax.rsqrt(jnp.mean(x * x, axis=-1, keepdims=True) + EPS)


def _sigmoid(x):
    return 1.0 / (1.0 + jnp.exp(-x))


def _log_sigmoid(x):
    return jnp.minimum(x, 0.0) - jnp.log(1.0 + jnp.exp(-jnp.abs(x)))


def _tokens(ref):
    return ref[...] if len(ref.shape) == 2 else ref[:, 0, :]


def _dot(a, b):
    return jnp.dot(a, b, preferred_element_type=F32)


def _dot_nt(a, b):
    return lax.dot_general(a, b, (((1,), (1,)), ((), ())), preferred_element_type=F32)


def _dot_tn(a, b):
    return lax.dot_general(a, b, (((0,), (0,)), ((), ())), preferred_element_type=F32)


def _rotary(x, cos, sin):
    half = x.shape[-1] // 2
    x1, x2 = x[:, :half], x[:, half:]
    return jnp.concatenate([x1 * cos - x2 * sin, x1 * sin + x2 * cos], axis=-1)


def _gla_log_alpha(r, wup, bg):
    pre = _dot(r.astype(BF16), wup) + bg
    return _log_sigmoid(pre) * (1.0 / GLA_TAU)


def _gate_code_spec():
    return pl.BlockSpec((pl.Element(GLA_RANK), pl.Element(D_MODEL)), lambda *idx: (R_START, 0))


def _gate_code(u, wr_ref):
    wr = jnp.concatenate([wr_ref[...].astype(BF16), jnp.zeros((LANE - GLA_RANK, D_MODEL), BF16)], axis=0)
    return _dot_nt(u, wr)


Z_BUFFERS = (
    ("qx", (T_BLK, GLA_DK), BF16),
    ("kx", (T_BLK, GLA_DK), BF16),
    ("qc", (N_CHUNKS, H_GLA, GLA_DK), BF16),
    ("kc", (N_CHUNKS, H_GLA, GLA_DK), BF16),
    ("qi", (T_BLK, GLA_DK), BF16),
    ("ko", (T_BLK, GLA_DK), BF16),
    ("va", (T_BLK, GLA_DV), BF16),
    ("dl", (SUBLANE, GLA_DK), F32),
    ("gta", (T_BLK, GW), F32),
    ("qb", (RET_PER_GROUP, T_BLK, RET_DK), BF16),
    ("kb", (RET_PER_GROUP, T_BLK, RET_DK), BF16),
    ("qbi", (RET_PER_GROUP, T_BLK, RET_DK), BF16),
    ("kbo", (RET_PER_GROUP, T_BLK, RET_DK), BF16),
    ("vb", (RET_PER_GROUP, T_BLK, RET_DV), BF16),
    ("gtb", (T_BLK, GW), F32),
)
Z_NAMES = tuple(n for n, _, _ in Z_BUFFERS)
assert N_CHUNKS <= SUBLANE


VT_DEC = 0
VT_KA = VT_DEC + GLA_HEADS
VT_QA = VT_KA + GLA_HEADS
VT_KB = VT_QA + GLA_HEADS
VT_QB = VT_KB + RET_HEADS
VT_N = VT_QB + RET_HEADS
VT_PAD = -(-VT_N // SUBLANE) * SUBLANE


ROW_PIECES = (("dec", GLA_HEADS, GLA_DK), ("ka", GLA_HEADS, GLA_DK), ("qa", GLA_HEADS, GLA_DK),
              ("kb", RET_HEADS, RET_DK), ("qb", RET_HEADS, RET_DK), ("va", GLA_HEADS, GLA_DV),
              ("vb", RET_HEADS, RET_DV))
ROW_OFFS = dict(zip([n for n, _, _ in ROW_PIECES],
                    np.concatenate([[0], np.cumsum([h * w for _, h, w in ROW_PIECES])[:-1]]).tolist()))
ROW_W = sum(h * w for _, h, w in ROW_PIECES)
O_W = GLA_V + RET_V


def _state_update(lg_ref, sg_ref, sr_ref, row_ref, nsg_ref, nsr_ref, o_ref):
    def vec(name, h, width):
        off = ROW_OFFS[name] + h * width
        return row_ref[0, :, off:off + width]

    rows = [vec(name, h, GLA_DK) for name, heads, _ in ROW_PIECES[:5] for h in range(heads)]
    rows.append(jnp.zeros((VT_PAD - VT_N, GLA_DK), F32))
    vt = jnp.concatenate(rows, axis=0).T

    def col(i):
        return vt[:, i:i + 1]

    for h in range(GLA_HEADS):
        s_new = col(VT_DEC + h) * sg_ref[0, h] + col(VT_KA + h) * vec("va", h, GLA_DV)
        nsg_ref[0, h] = s_new
        o_ref[0, :, h * GLA_DV:(h + 1) * GLA_DV] = jnp.sum(col(VT_QA + h) * s_new, axis=0, keepdims=True)
        yield
    for h in range(RET_HEADS):
        gamma = jnp.exp(jnp.full((1, RET_DV), lg_ref[h], F32))
        s_new = gamma * sr_ref[0, h] + col(VT_KB + h) * vec("vb", h, RET_DV)
        nsr_ref[0, h] = s_new
        o_ref[0, :, GLA_V + h * RET_DV:GLA_V + (h + 1) * RET_DV] = jnp.sum(
            col(VT_QB + h) * s_new, axis=0, keepdims=True)
        yield


N_SAMPLE_IN = 3
N_SAMPLE_OUT = 3
N_CAST = 2


def _project_block(u_ref, r_ref, w, wup_ref, bg_ref, cos_ref, sin_ref, lg_ref, g, z):
    T = T_BLK
    u = u_ref[0]

    def proj(name, j=0, n=None):
        start, width = W_PIECES[name]
        n = width if n is None else n
        return _dot(u, w[:, start + j * n:start + (j + 1) * n])

    la = _gla_log_alpha(r_ref[0], wup_ref[0], bg_ref[0])
    yield
    q = proj("qa") * (GLA_DK ** -0.5)
    yield
    k = proj("ka")
    ri = lax.broadcasted_iota(jnp.int32, (C_GLA, C_GLA), 0)
    ci = lax.broadcasted_iota(jnp.int32, (C_GLA, C_GLA), 1)
    tri = jnp.where(ri >= ci, 1.0, 0.0).astype(BF16)
    for c in range(N_CHUNKS):
        sl = slice(c * C_GLA, (c + 1) * C_GLA)
        la_c = la[sl]
        la_hi = la_c.astype(BF16)
        la_lo = (la_c - la_hi.astype(F32)).astype(BF16)
        b = _dot(tri, la_hi) + _dot(tri, la_lo)
        b_last = b[C_GLA - 1:C_GLA]
        q_c, k_c = q[sl], k[sl]
        z["qi"][sl] = (q_c * jnp.exp(b)).astype(BF16)
        z["ko"][sl] = (k_c * jnp.exp(b_last - b)).astype(BF16)
        z["dl"][c:c + 1] = jnp.exp(b_last)
        for h in range(2):
            hs = slice(h * H_GLA, (h + 1) * H_GLA)
            rows = slice(c * C_GLA + h * H_GLA, c * C_GLA + (h + 1) * H_GLA)
            b_h = b[hs]
            b_mid = b_h[H_GLA // 2 - 1:H_GLA // 2]
            z["qx"][rows] = (q_c[hs] * jnp.exp(b_h - b_mid)).astype(BF16)
            z["kx"][rows] = (k_c[hs] * jnp.exp(b_mid - b_h)).astype(BF16)
        b_edge = b[H_GLA - 1:H_GLA]
        z["qc"][c] = (q_c[H_GLA:] * jnp.exp(b[H_GLA:] - b_edge)).astype(BF16)
        z["kc"][c] = (k_c[:H_GLA] * jnp.exp(b_edge - b[:H_GLA])).astype(BF16)
    yield
    z["va"][...] = proj("va").astype(BF16)
    yield
    g_a = proj("ga")
    silu_a = g_a * _sigmoid(g_a)
    yield
    z["gta"][...] = _sigmoid(proj("ma")) * silu_a
    yield

    cos = cos_ref[...]
    sin = sin_ref[...]
    tl = lax.broadcasted_iota(jnp.int32, (T, LANE), 0).astype(F32)
    for j in range(RET_PER_GROUP):
        lg = lg_ref[g * RET_PER_GROUP + j]
        q_b = _rotary(proj("qb", j, RET_DK), cos, sin)
        k_b = _rotary(proj("kb", j, RET_DK), cos, sin) * (RET_DK ** -0.5)
        dec_in = jnp.exp((tl + 1.0) * lg)
        dec_out = jnp.exp((T - 1.0 - tl) * lg)
        z["qb"][j] = q_b.astype(BF16)
        z["kb"][j] = k_b.astype(BF16)
        z["qbi"][j] = (q_b * jnp.concatenate([dec_in, dec_in], axis=-1)).astype(BF16)
        z["kbo"][j] = (k_b * jnp.concatenate([dec_out, dec_out], axis=-1)).astype(BF16)
        yield
        z["vb"][j] = proj("vb", j, RET_DV).astype(BF16)
        g_b = proj("gb", j, RET_DV)
        yield
        z["gtb"][:, j * RET_DV:(j + 1) * RET_DV] = _sigmoid(proj("mb", j, RET_DV)) * (g_b * _sigmoid(g_b))
        yield


def _recur_block(z, lg_ref, g, gn_ref, sgt_ref, sret_ref, merged_ref):
    T = T_BLK
    ri = lax.broadcasted_iota(jnp.int32, (H_GLA, H_GLA), 0)
    ci = lax.broadcasted_iota(jnp.int32, (H_GLA, H_GLA), 1)
    causal = ri >= ci
    st = sgt_ref[...]
    o_chunks = []
    for c in range(N_CHUNKS):
        sl = slice(c * C_GLA, (c + 1) * C_GLA)
        v_c = z["va"][sl]
        diag = []
        for h in range(2):
            rows = slice(c * C_GLA + h * H_GLA, c * C_GLA + (h + 1) * H_GLA)
            diag.append(jnp.where(causal, _dot_nt(z["qx"][rows], z["kx"][rows]), 0.0))
        cross = _dot_nt(z["qc"][c], z["kc"][c])
        a = jnp.concatenate([jnp.concatenate([diag[0], jnp.zeros_like(cross)], axis=1),
                             jnp.concatenate([cross, diag[1]], axis=1)], axis=0).astype(BF16)
        o_chunks.append(_dot_nt(z["qi"][sl], st.astype(BF16)) + _dot(a, v_c))
        st = st * z["dl"][c:c + 1] + _dot_tn(v_c, z["ko"][sl])
        yield
    sgt_ref[...] = st
    part_a = z["gta"][...] * (_rms(jnp.concatenate(o_chunks, axis=0)) * gn_ref[...])
    yield

    rt = lax.broadcasted_iota(jnp.int32, (T, T), 0)
    ct = lax.broadcasted_iota(jnp.int32, (T, T), 1)
    parts_b = []
    for j in range(RET_PER_GROUP):
        lg = lg_ref[g * RET_PER_GROUP + j]
        v_b = z["vb"][j]
        dmat = jnp.where(rt >= ct, jnp.exp((rt - ct).astype(F32) * lg), 0.0)
        a = (_dot_nt(z["qb"][j], z["kb"][j]) * dmat).astype(BF16)
        s = sret_ref[0, j]
        o_b = _rms(_dot(z["qbi"][j], s.astype(BF16)) + _dot(a, v_b))
        dec_all = jnp.exp(jnp.full((1, RET_DV), T * lg, F32))
        sret_ref[0, j] = s * dec_all + _dot_tn(z["kbo"][j], v_b)
        parts_b.append(z["gtb"][:, j * RET_DV:(j + 1) * RET_DV] * o_b)
        yield
    merged_ref[0] = (part_a + jnp.concatenate(parts_b, axis=-1)).astype(merged_ref.dtype)


def _mix_prompt_kernel(lg_ref, u_ref, r_ref, *refs, blocks_per_group, blocks_per_seq):
    nz = len(Z_NAMES)
    w = refs[0]
    refs = list(refs[1:])
    wup_ref, bg_ref, gn_ref, cos_ref, sin_ref = refs[:5]
    sample_in = refs[5:5 + N_SAMPLE_IN]
    cast_in = refs[5 + N_SAMPLE_IN:5 + N_SAMPLE_IN + N_CAST]
    refs = refs[5 + N_SAMPLE_IN + N_CAST:]
    merged_ref, sgla_ref, sret_ref = refs[:3]
    sample_out = refs[3:3 + N_SAMPLE_OUT]
    cast_out = refs[3 + N_SAMPLE_OUT:3 + N_SAMPLE_OUT + N_CAST]
    sgt_ref = refs[3 + N_SAMPLE_OUT + N_CAST]
    refs = refs[4 + N_SAMPLE_OUT + N_CAST:]
    z_even = dict(zip(Z_NAMES, refs[:nz]))
    z_odd = dict(zip(Z_NAMES, refs[nz:]))
    s = pl.program_id(0)
    n_blocks = pl.num_programs(0) - 1
    g_p = jnp.minimum(s, n_blocks - 1) // blocks_per_group
    r = jnp.maximum(s - 1, 0)
    g_r = r // blocks_per_group
    t_r = r % blocks_per_seq

    @pl.when(s == 0)
    def _():
        for ref in z_odd.values():
            ref[...] = jnp.zeros_like(ref)

    @pl.when(t_r == 0)
    def _():
        sgt_ref[...] = jnp.zeros_like(sgt_ref)
        sret_ref[...] = jnp.zeros_like(sret_ref)

    for src, dst in zip(cast_in, cast_out):
        dst[...] = src[...].astype(dst.dtype)

    def step(z_write, z_read):
        rec = _recur_block(z_read, lg_ref, g_r, gn_ref, sgt_ref, sret_ref, merged_ref)
        prj = _project_block(u_ref, r_ref, w, wup_ref, bg_ref, cos_ref, sin_ref, lg_ref, g_p, z_write)
        upd = _state_update(lg_ref, *sample_in, *sample_out)
        for _ in itertools.zip_longest(prj, rec, upd):
            pass

    @pl.when(s % 2 == 0)
    def _():
        step(z_even, z_odd)

    @pl.when(s % 2 == 1)
    def _():
        step(z_odd, z_even)

    @pl.when((t_r == blocks_per_seq - 1) & (s > 0))
    def _():
        sgla_ref[0, 0] = sgt_ref[...].T


def _norm_kernel(x_ref, g_ref, wr_ref, u_ref, r_ref):
    u = (_rms(x_ref[...]) * g_ref[...]).astype(u_ref.dtype)
    u_ref[...] = u
    r_ref[...] = _gate_code(u, wr_ref)


def _norm_prompt(x, gain, w_f32):
    n, D = x.shape
    tm = min(TM_NORM, n)
    return pl.pallas_call(
        _norm_kernel,
        grid=(n // tm,),
        in_specs=[pl.BlockSpec((tm, D), lambda i: (i, 0)), pl.BlockSpec((1, D), lambda i: (0, 0)),
                  _gate_code_spec()],
        out_specs=[pl.BlockSpec((tm, D), lambda i: (i, 0)), pl.BlockSpec((tm, LANE), lambda i: (i, 0))],
        out_shape=[jax.ShapeDtypeStruct((n, D), BF16), jax.ShapeDtypeStruct((n, LANE), F32)],
        compiler_params=pltpu.CompilerParams(
            dimension_semantics=("arbitrary",), vmem_limit_bytes=VMEM_LIMIT),
        name="norm_prompt",
    )(x, gain, w_f32)


def _mix_prompt(u, r, lg, w_t, wup, bg, gn, cos, sin, sg, sr, rows, to_cast):
    B, L, D = u.shape
    nt = L // T_BLK
    n_blocks = N_GROUPS * B * nt
    n_req = sg.shape[0]
    assert n_req <= n_blocks + 1
    assert rows.shape == (n_req, 1, ROW_W) and len(to_cast) == N_CAST
    slab = ROW_TILE_BF16
    assert all(a.shape[0] <= slab * n_blocks and a.shape[0] % slab == 0 for a in to_cast)

    def cast_specs():
        return [pl.BlockSpec((slab, a.shape[1]), lambda s, n=a.shape[0] // slab: (jnp.minimum(s, n - 1), 0))
                for a in to_cast]

    def req_row(s):
        return (jnp.minimum(s, n_req - 1), 0, 0)

    def req_blk(s):
        return (jnp.minimum(s, n_req - 1), 0, 0, 0)

    def proj_idx(s):
        p = jnp.minimum(s, n_blocks - 1)
        return p // (B * nt), (p // nt) % B, p % nt

    def recur_idx(s):
        r = jnp.maximum(s - 1, 0)
        return r // (B * nt), (r // nt) % B, r % nt

    def out_map(s):
        g, b, t = recur_idx(s)
        return (b, t, g)

    def state_map(s):
        g, b, _ = recur_idx(s)
        return (b, g, 0, 0)

    z_scratch = [pltpu.VMEM(shape, dtype) for _, shape, dtype in Z_BUFFERS]
    return pl.pallas_call(
        functools.partial(_mix_prompt_kernel, blocks_per_group=B * nt, blocks_per_seq=nt),
        grid=(n_blocks + 1,),
        in_specs=[
            pl.BlockSpec(memory_space=pltpu.SMEM),
            pl.BlockSpec((1, T_BLK, D), lambda s: (proj_idx(s)[1], proj_idx(s)[2], 0)),
            pl.BlockSpec((1, T_BLK, LANE), lambda s: (proj_idx(s)[1], proj_idx(s)[2], 0)),
            pl.BlockSpec((D, GROUP_W), lambda s: (0, proj_idx(s)[0]), pipeline_mode=pl.Buffered(1)),
            pl.BlockSpec((1, LANE, GLA_DK), lambda s: (proj_idx(s)[0], 0, 0)),
            pl.BlockSpec((1, 1, GLA_DK), lambda s: (proj_idx(s)[0], 0, 0)),
            pl.BlockSpec((1, GLA_DV), lambda s: (0, 0)),
            pl.BlockSpec((T_BLK, LANE), lambda s: (proj_idx(s)[2], 0)),
            pl.BlockSpec((T_BLK, LANE), lambda s: (proj_idx(s)[2], 0)),
            pl.BlockSpec((1, GLA_HEADS, GLA_DK, GLA_DV), req_blk),
            pl.BlockSpec((1, RET_HEADS, RET_DK, RET_DV), req_blk),
            pl.BlockSpec((1, 1, ROW_W), req_row),
            *cast_specs(),
        ],
        out_specs=[
            pl.BlockSpec((1, T_BLK, GW), out_map),
            pl.BlockSpec((1, 1, GLA_DK, GLA_DV), state_map),
            pl.BlockSpec((1, RET_PER_GROUP, RET_DK, RET_DV), state_map),
            pl.BlockSpec((1, GLA_HEADS, GLA_DK, GLA_DV), req_blk),
            pl.BlockSpec((1, RET_HEADS, RET_DK, RET_DV), req_blk),
            pl.BlockSpec((1, 1, O_W), req_row),
            *cast_specs(),
        ],
        out_shape=[
            jax.ShapeDtypeStruct((B, L, D_MODEL), BF16),
            jax.ShapeDtypeStruct((B, GLA_HEADS, GLA_DK, GLA_DV), F32),
            jax.ShapeDtypeStruct((B, RET_HEADS, RET_DK, RET_DV), F32),
            jax.ShapeDtypeStruct(sg.shape, F32),
            jax.ShapeDtypeStruct(sr.shape, F32),
            jax.ShapeDtypeStruct((n_req, 1, O_W), F32),
            *[jax.ShapeDtypeStruct(a.shape, BF16) for a in to_cast],
        ],
        scratch_shapes=[pltpu.VMEM((GLA_DV, GLA_DK), F32)] + z_scratch + z_scratch,
        compiler_params=pltpu.CompilerParams(
            dimension_semantics=("arbitrary",), vmem_limit_bytes=VMEM_LIMIT),
        name="mix_prompt",
    )(lg, u, r, w_t, wup, bg, gn, cos, sin, sg, sr, rows, *to_cast)


def _out_kernel(x_ref, mg_ref, p_ref, wout_ref, nple_ref, wpg_ref, wpp_ref, nfin_ref, y_ref, *,
                final_norm):
    h = _tokens(x_ref) + _dot(mg_ref[...], wout_ref[...])
    hn = (_rms(h) * nple_ref[...]).astype(BF16)
    gate = _sigmoid(_dot(hn, wpg_ref[...]))
    h = h + gate * _dot(_tokens(p_ref).astype(BF16), wpp_ref[...])
    if final_norm:
        h = _rms(h) * nfin_ref[...]
    if len(y_ref.shape) == 2:
        y_ref[...] = h
    else:
        y_ref[:, 0, :] = h


def _out_proj(x, merged, p, w_out, nple, w_pg, w_pp, nfin, final_norm):
    n, D = x.shape[0], x.shape[-1]
    tm = min(TM_OUT, n)
    const = lambda i: (0, 0)

    def token_spec(a):
        return pl.BlockSpec((tm,) + a.shape[1:], lambda i: (i,) + (0,) * (a.ndim - 1))

    return pl.pallas_call(
        functools.partial(_out_kernel, final_norm=final_norm),
        grid=(n // tm,),
        in_specs=[
            token_spec(x),
            token_spec(merged),
            token_spec(p),
            pl.BlockSpec((D, D), const, pipeline_mode=pl.Buffered(1)),
            pl.BlockSpec((1, D), const),
            pl.BlockSpec((D, D), const, pipeline_mode=pl.Buffered(1)),
            pl.BlockSpec((PLE_DIM, D), const, pipeline_mode=pl.Buffered(1)),
            pl.BlockSpec((1, D), const),
        ],
        out_specs=token_spec(x),
        out_shape=jax.ShapeDtypeStruct(x.shape, F32),
        compiler_params=pltpu.CompilerParams(
            dimension_semantics=("arbitrary",), vmem_limit_bytes=VMEM_LIMIT),
        name="out_proj",
    )(x, merged, p, w_out, nple, w_pg, w_pp, nfin)


def _prep_kernel(x_ref, nmix_ref, wq_ref, wk_ref, w_ref, wr_ref, wb_ref, z_ref, r_ref):
    u = (_rms(_tokens(x_ref)) * nmix_ref[...]).astype(BF16)
    j = pl.program_id(1)

    def emit(rows_f32):
        wb = rows_f32.astype(BF16)
        wb_ref[...] = wb.T
        z_ref[...] = _dot_nt(u, wb)

    @pl.when(j == 0)
    def _():
        emit(jnp.concatenate([wq_ref[...], wk_ref[...]], axis=0))

    @pl.when(j != 0)
    def _():
        emit(w_ref[...])

    @pl.when((pl.program_id(0) == 0) & (j == 0))
    def _():
        r_ref[...] = _gate_code(u, wr_ref)


def _prep_weights(x, nmix, w_f32):
    n, D = x.shape[0], x.shape[-1]
    chunks = GROUP_W // GW
    step = WIDE_ROWS[1] - WIDE_ROWS[0]
    after = next(i for i in range(1, len(WIDE_ROWS)) if WIDE_ROWS[i] - WIDE_ROWS[i - 1] != step)
    assert all(WIDE_ROWS[i] == WIDE_ROWS[0] + i * step + (GLA_RANK if i >= after else 0)
               for i in range(len(WIDE_ROWS)))
    assert WIDE_ROWS[0] % SUBLANE == 0 and step % SUBLANE == 0 and GLA_RANK % SUBLANE == 0

    def wide_rows(g, j):
        i = jnp.maximum(j - 1, 0)
        skip = jnp.where(i >= after, GLA_RANK // SUBLANE, 0)
        return (SUBLANE * (WIDE_ROWS[0] // SUBLANE + (step // SUBLANE) * i + skip + (GW // SUBLANE) * g), 0)

    def narrow_rows(first):
        return lambda g, j: (SUBLANE * (first // SUBLANE + (GLA_DK // SUBLANE) * g), 0)

    return pl.pallas_call(
        _prep_kernel,
        grid=(N_GROUPS, chunks),
        in_specs=[
            pl.BlockSpec(x.shape, lambda g, j: (0,) * x.ndim),
            pl.BlockSpec((1, D), lambda g, j: (0, 0)),
            pl.BlockSpec((pl.Element(GLA_DK), pl.Element(D)), narrow_rows(IN_OFFS[0])),
            pl.BlockSpec((pl.Element(GLA_DK), pl.Element(D)), narrow_rows(IN_OFFS[1])),
            pl.BlockSpec((pl.Element(GW), pl.Element(D)), wide_rows),
            _gate_code_spec(),
        ],
        out_specs=[
            pl.BlockSpec((D, GW), lambda g, j: (0, g * chunks + j)),
            pl.BlockSpec((n, GW), lambda g, j: (0, g * chunks + j)),
            pl.BlockSpec((n, LANE), lambda g, j: (0, 0)),
        ],
        out_shape=[
            jax.ShapeDtypeStruct((D, N_PACK), BF16),
            jax.ShapeDtypeStruct((n, N_PACK), F32),
            jax.ShapeDtypeStruct((n, LANE), F32),
        ],
        compiler_params=pltpu.CompilerParams(
            dimension_semantics=("arbitrary", "arbitrary"), vmem_limit_bytes=VMEM_LIMIT),
        name="prep_weights",
    )(x, nmix, w_f32, w_f32, w_f32, w_f32)


def _sample_transform_kernel(z_ref, r_ref, wup_ref, bg_ref, cos_ref, sin_ref, row_ref):
    def piece(name):
        start, width = W_PIECES[name]
        return jnp.concatenate([z_ref[:, g * GROUP_W + start:g * GROUP_W + start + width]
                                for g in range(N_GROUPS)], axis=-1)

    def put(name, value, h=0):
        off = ROW_OFFS[name] + h * value.shape[-1]
        row_ref[:, 0, off:off + value.shape[-1]] = value

    r = r_ref[...]
    for g in range(N_GROUPS):
        put("dec", jnp.exp(_gla_log_alpha(r, wup_ref[g], bg_ref[g])), g)
    put("qa", piece("qa") * (GLA_DK ** -0.5))
    for name in ("ka", "va", "vb"):
        put(name, piece(name))
    cos = cos_ref[...]
    sin = sin_ref[...]
    q_b = piece("qb")
    k_b = piece("kb")
    for h in range(RET_HEADS):
        sl = slice(h * RET_DK, (h + 1) * RET_DK)
        put("qb", _rotary(q_b[:, sl], cos, sin), h)
        put("kb", _rotary(k_b[:, sl], cos, sin) * (RET_DK ** -0.5), h)


def _sample_transform(z, r, wup, bg, cos, sin):
    n = z.shape[0]
    return pl.pallas_call(
        _sample_transform_kernel,
        out_shape=jax.ShapeDtypeStruct((n, 1, ROW_W), F32),
        compiler_params=pltpu.CompilerParams(vmem_limit_bytes=VMEM_LIMIT),
        name="sample_transform",
    )(z, r, wup, bg, cos, sin)


def _merge_sample_kernel(o_ref, z_ref, gn_ref, mg_ref):
    def gate(name, g, lo, n):
        start = g * GROUP_W + W_PIECES[name][0] + lo
        return z_ref[:, start:start + n]

    for h in range(GLA_HEADS):
        sl = slice(h * GLA_DV, (h + 1) * GLA_DV)
        g_a = gate("ga", h, 0, GLA_DV)
        part_a = _sigmoid(gate("ma", h, 0, GW)) * (_rms(o_ref[:, 0, sl]) * gn_ref[...] * (g_a * _sigmoid(g_a)))
        parts_b = []
        for j in range(RET_PER_GROUP):
            lo = h * GW + j * RET_DV
            g_b = gate("gb", h, j * RET_DV, RET_DV)
            o_b = o_ref[:, 0, GLA_V + lo:GLA_V + lo + RET_DV]
            parts_b.append(_sigmoid(gate("mb", h, j * RET_DV, RET_DV)) * (_rms(o_b) * (g_b * _sigmoid(g_b))))
        mg_ref[:, sl] = (part_a + jnp.concatenate(parts_b, axis=-1)).astype(mg_ref.dtype)


def _merge_sample(o, z_raw, gn):
    n = o.shape[0]
    return pl.pallas_call(
        _merge_sample_kernel,
        out_shape=jax.ShapeDtypeStruct((n, D_MODEL), BF16),
        compiler_params=pltpu.CompilerParams(vmem_limit_bytes=VMEM_LIMIT),
        name="merge_sample",
    )(o, z_raw, gn)


def _rope_tables(pos):
    half = RET_DK // 2
    inv = 1.0 / (ROPE_BASE ** jnp.linspace(0.0, 1.0, half, dtype=jnp.float32))
    ang = pos[:, None] * inv[None, :]
    return jnp.cos(ang), jnp.sin(ang)


def kernel(x_prompt, x_sample, state_gla, state_ret, p_prompt, p_sample, norm_mix, w_in, w_gla_up, b_gla,
           gla_norm, w_out, norm_ple, w_ple_gate, w_ple_proj, norm_final):
    depth = w_in.shape[0]
    Bp, Lp, D = x_prompt.shape
    Bs, Ls, _ = x_sample.shape
    assert Ls == 1 and Lp % T_BLK == 0
    cos_p, sin_p = _rope_tables(jnp.arange(Lp, dtype=jnp.float32))
    cos_s, sin_s = _rope_tables(PAST_LEN + jnp.arange(Ls, dtype=jnp.float32))
    log_gamma = jnp.log(1.0 - jnp.exp2(-5.0 - jnp.arange(RET_HEADS, dtype=jnp.float32)))
    nfin = norm_final.reshape(1, D)

    hp = x_prompt
    hs = x_sample
    gla_p, ret_p, gla_s, ret_s = [], [], [], []
    for i in range(depth):
        last = i == depth - 1
        nmix = norm_mix[i].reshape(1, D)
        nple = norm_ple[i].reshape(1, D)
        gn = gla_norm[i].reshape(1, GLA_DV)
        w_f32 = w_in[i].T
        wup = jnp.pad(w_gla_up[i], ((0, LANE - GLA_RANK), (0, 0))).astype(BF16)
        wup = wup.reshape(LANE, GLA_HEADS, GLA_DK).transpose(1, 0, 2)
        bg = b_gla[i].reshape(GLA_HEADS, 1, GLA_DK)
        w_pp = w_ple_proj[i].astype(BF16)

        w_t, z_raw, r_s = _prep_weights(hs, nmix, w_f32)
        rows = _sample_transform(z_raw, r_s, wup, bg, cos_s, sin_s)
        u, r = _norm_prompt(hp.reshape(Bp * Lp, D), nmix, w_f32)
        merged, sg, sr, nsg, nsr, o_s, w_o, w_pg = _mix_prompt(
            u.reshape(Bp, Lp, D), r.reshape(Bp, Lp, LANE), log_gamma, w_t, wup, bg, gn, cos_p, sin_p,
            state_gla[i], state_ret[i], rows, [w_out[i], w_ple_gate[i]])
        hp = _out_proj(hp.reshape(Bp * Lp, D), merged.reshape(Bp * Lp, D), p_prompt[i].reshape(Bp * Lp, PLE_DIM),
                       w_o, nple, w_pg, w_pp, nfin, last).reshape(Bp, Lp, D)
        gla_p.append(sg)
        ret_p.append(sr)

        merged_s = _merge_sample(o_s, z_raw, gn)
        hs = _out_proj(hs, merged_s, p_sample[i], w_o, nple, w_pg, w_pp, nfin, last)
        gla_s.append(nsg)
        ret_s.append(nsr)

    return (hp, hs, jnp.stack(gla_p), jnp.stack(ret_p), jnp.stack(gla_s), jnp.stack(ret_s))
```

```python
import functools
import itertools

import jax
import jax.numpy as jnp
import numpy as np
from jax import lax
from jax.experimental import pallas as pl
from jax.experimental.pallas import tpu as pltpu

F32 = jnp.float32
BF16 = jnp.bfloat16

D_MODEL = 2048
PAST_LEN = 16384
PLE_DIM = 256
GLA_HEADS = 4
GLA_DK = 256
GLA_DV = 512
GLA_RANK = 16
GLA_TAU = 16.0
RET_HEADS = 8
RET_DK = 256
RET_DV = 256
ROPE_BASE = 10000.0
EPS = 1e-6

GLA_QK = GLA_HEADS * GLA_DK
GLA_V = GLA_HEADS * GLA_DV
RET_QK = RET_HEADS * RET_DK
RET_V = RET_HEADS * RET_DV
IN_SPLITS = (GLA_QK, GLA_QK, GLA_V, GLA_V, GLA_RANK, RET_QK, RET_QK, RET_V, RET_V, D_MODEL, D_MODEL)
IN_OFFS = tuple(int(v) for v in np.concatenate([[0], np.cumsum(IN_SPLITS)[:-1]]))

N_GROUPS = GLA_HEADS
RET_PER_GROUP = RET_HEADS // N_GROUPS
GW = GLA_DV
LANE = 128
SUBLANE = 8
ROW_TILE_BF16 = 16

R_START = IN_OFFS[4]
HI_START = IN_OFFS[5]
N_PACK = HI_START - GLA_RANK + sum(IN_SPLITS[5:])
W_PIECES = {
    "qa": (IN_OFFS[0], GLA_DK),
    "ka": (IN_OFFS[1], GLA_DK),
    "va": (IN_OFFS[2], GLA_DV),
    "ga": (IN_OFFS[3], GLA_DV),
    "qb": (IN_OFFS[5] - GLA_RANK, GW),
    "kb": (IN_OFFS[6] - GLA_RANK, GW),
    "vb": (IN_OFFS[7] - GLA_RANK, GW),
    "gb": (IN_OFFS[8] - GLA_RANK, GW),
    "ma": (IN_OFFS[9] - GLA_RANK, GW),
    "mb": (IN_OFFS[10] - GLA_RANK, GW),
}
W_NAMES = tuple(W_PIECES)

T_BLK = 256
C_GLA = 128
H_GLA = C_GLA // 2
N_CHUNKS = T_BLK // C_GLA
TM_OUT = 512
TM_NORM = 1024
PREP_ROWS = 1024
VMEM_LIMIT = 56 * 1024 * 1024


def _rms(x):
    return x * lax.rsqrt(jnp.mean(x * x, axis=-1, keepdims=True) + EPS)


def _sigmoid(x):
    return 1.0 / (1.0 + jnp.exp(-x))


def _log_sigmoid(x):
    return jnp.minimum(x, 0.0) - jnp.log(1.0 + jnp.exp(-jnp.abs(x)))


def _tokens(ref):
    return ref[...] if len(ref.shape) == 2 else ref[:, 0, :]


def _dot(a, b):
    return jnp.dot(a, b, preferred_element_type=F32)


def _dot_nt(a, b):
    return lax.dot_general(a, b, (((1,), (1,)), ((), ())), preferred_element_type=F32)


def _dot_tn(a, b):
    return lax.dot_general(a, b, (((0,), (0,)), ((), ())), preferred_element_type=F32)


def _rotary(x, cos, sin):
    half = x.shape[-1] // 2
    x1, x2 = x[:, :half], x[:, half:]
    return jnp.concatenate([x1 * cos - x2 * sin, x1 * sin + x2 * cos], axis=-1)


def _gla_log_alpha(r, wup, bg):
    pre = _dot(r.astype(BF16), wup) + bg
    return _log_sigmoid(pre) * (1.0 / GLA_TAU)


def _weight_specs(group_of, **kw):
    specs = []
    for n in W_NAMES:
        start, width = W_PIECES[n]
        assert start % width == 0

        def index_map(*idx, first=start // width):
            return (0, first + group_of(*idx))

        specs.append(pl.BlockSpec((D_MODEL, width), index_map, **kw))
    return specs


def _gate_code_spec():
    return pl.BlockSpec((pl.Element(GLA_RANK), pl.Element(D_MODEL)), lambda *idx: (R_START, 0))


def _gate_code(u, wr_ref):
    wr = jnp.concatenate([wr_ref[...].astype(BF16), jnp.zeros((LANE - GLA_RANK, D_MODEL), BF16)], axis=0)
    return _dot_nt(u, wr)


N_W = len(W_NAMES)


Z_BUFFERS = (
    ("qx", (T_BLK, GLA_DK), BF16),
    ("kx", (T_BLK, GLA_DK), BF16),
    ("qc", (N_CHUNKS, H_GLA, GLA_DK), BF16),
    ("kc", (N_CHUNKS, H_GLA, GLA_DK), BF16),
    ("qi", (T_BLK, GLA_DK), BF16),
    ("ko", (T_BLK, GLA_DK), BF16),
    ("va", (T_BLK, GLA_DV), BF16),
    ("dl", (SUBLANE, GLA_DK), F32),
    ("gta", (T_BLK, GW), F32),
    ("qb", (RET_PER_GROUP, T_BLK, RET_DK), BF16),
    ("kb", (RET_PER_GROUP, T_BLK, RET_DK), BF16),
    ("qbi", (RET_PER_GROUP, T_BLK, RET_DK), BF16),
    ("kbo", (RET_PER_GROUP, T_BLK, RET_DK), BF16),
    ("vb", (RET_PER_GROUP, T_BLK, RET_DV), BF16),
    ("gtb", (T_BLK, GW), F32),
)
Z_NAMES = tuple(n for n, _, _ in Z_BUFFERS)
assert N_CHUNKS <= SUBLANE


VT_DEC = 0
VT_KA = VT_DEC + GLA_HEADS
VT_QA = VT_KA + GLA_HEADS
VT_KB = VT_QA + GLA_HEADS
VT_QB = VT_KB + RET_HEADS
VT_N = VT_QB + RET_HEADS
VT_PAD = -(-VT_N // SUBLANE) * SUBLANE


ROW_PIECES = (("dec", GLA_HEADS, GLA_DK), ("ka", GLA_HEADS, GLA_DK), ("qa", GLA_HEADS, GLA_DK),
              ("kb", RET_HEADS, RET_DK), ("qb", RET_HEADS, RET_DK), ("va", GLA_HEADS, GLA_DV),
              ("vb", RET_HEADS, RET_DV))
ROW_OFFS = dict(zip([n for n, _, _ in ROW_PIECES],
                    np.concatenate([[0], np.cumsum([h * w for _, h, w in ROW_PIECES])[:-1]]).tolist()))
ROW_W = sum(h * w for _, h, w in ROW_PIECES)
O_W = GLA_V + RET_V


def _state_update(lg_ref, sg_ref, sr_ref, row_ref, nsg_ref, nsr_ref, o_ref):
    def vec(name, h, width):
        off = ROW_OFFS[name] + h * width
        return row_ref[0, :, off:off + width]

    rows = [vec(name, h, GLA_DK) for name, heads, _ in ROW_PIECES[:5] for h in range(heads)]
    rows.append(jnp.zeros((VT_PAD - VT_N, GLA_DK), F32))
    vt = jnp.concatenate(rows, axis=0).T

    def col(i):
        return vt[:, i:i + 1]

    for h in range(GLA_HEADS):
        s_new = col(VT_DEC + h) * sg_ref[0, h] + col(VT_KA + h) * vec("va", h, GLA_DV)
        nsg_ref[0, h] = s_new
        o_ref[0, :, h * GLA_DV:(h + 1) * GLA_DV] = jnp.sum(col(VT_QA + h) * s_new, axis=0, keepdims=True)
        yield
    for h in range(RET_HEADS):
        gamma = jnp.exp(jnp.full((1, RET_DV), lg_ref[h], F32))
        s_new = gamma * sr_ref[0, h] + col(VT_KB + h) * vec("vb", h, RET_DV)
        nsr_ref[0, h] = s_new
        o_ref[0, :, GLA_V + h * RET_DV:GLA_V + (h + 1) * RET_DV] = jnp.sum(
            col(VT_QB + h) * s_new, axis=0, keepdims=True)
        yield


N_SAMPLE_IN = 3
N_SAMPLE_OUT = 3
N_CAST = 2


def _project_block(u_ref, r_ref, w, wup_ref, bg_ref, cos_ref, sin_ref, lg_ref, g, z):
    T = T_BLK
    u = u_ref[0]

    def proj(name, j=0, n=None):
        ref = w[name]
        n = ref.shape[1] if n is None else n
        return _dot(u, ref[:, j * n:(j + 1) * n])

    la = _gla_log_alpha(r_ref[0], wup_ref[0], bg_ref[0])
    yield
    q = proj("qa") * (GLA_DK ** -0.5)
    yield
    k = proj("ka")
    ri = lax.broadcasted_iota(jnp.int32, (C_GLA, C_GLA), 0)
    ci = lax.broadcasted_iota(jnp.int32, (C_GLA, C_GLA), 1)
    tri = jnp.where(ri >= ci, 1.0, 0.0).astype(BF16)
    for c in range(N_CHUNKS):
        sl = slice(c * C_GLA, (c + 1) * C_GLA)
        la_c = la[sl]
        la_hi = la_c.astype(BF16)
        la_lo = (la_c - la_hi.astype(F32)).astype(BF16)
        b = _dot(tri, la_hi) + _dot(tri, la_lo)
        b_last = b[C_GLA - 1:C_GLA]
        q_c, k_c = q[sl], k[sl]
        z["qi"][sl] = (q_c * jnp.exp(b)).astype(BF16)
        z["ko"][sl] = (k_c * jnp.exp(b_last - b)).astype(BF16)
        z["dl"][c:c + 1] = jnp.exp(b_last)
        for h in range(2):
            hs = slice(h * H_GLA, (h + 1) * H_GLA)
            rows = slice(c * C_GLA + h * H_GLA, c * C_GLA + (h + 1) * H_GLA)
            b_h = b[hs]
            b_mid = b_h[H_GLA // 2 - 1:H_GLA // 2]
            z["qx"][rows] = (q_c[hs] * jnp.exp(b_h - b_mid)).astype(BF16)
            z["kx"][rows] = (k_c[hs] * jnp.exp(b_mid - b_h)).astype(BF16)
        b_edge = b[H_GLA - 1:H_GLA]
        z["qc"][c] = (q_c[H_GLA:] * jnp.exp(b[H_GLA:] - b_edge)).astype(BF16)
        z["kc"][c] = (k_c[:H_GLA] * jnp.exp(b_edge - b[:H_GLA])).astype(BF16)
    yield
    z["va"][...] = proj("va").astype(BF16)
    yield
    g_a = proj("ga")
    silu_a = g_a * _sigmoid(g_a)
    yield
    z["gta"][...] = _sigmoid(proj("ma")) * silu_a
    yield

    cos = cos_ref[...]
    sin = sin_ref[...]
    tl = lax.broadcasted_iota(jnp.int32, (T, LANE), 0).astype(F32)
    for j in range(RET_PER_GROUP):
        lg = lg_ref[g * RET_PER_GROUP + j]
        q_b = _rotary(proj("qb", j, RET_DK), cos, sin)
        k_b = _rotary(proj("kb", j, RET_DK), cos, sin) * (RET_DK ** -0.5)
        dec_in = jnp.exp((tl + 1.0) * lg)
        dec_out = jnp.exp((T - 1.0 - tl) * lg)
        z["qb"][j] = q_b.astype(BF16)
        z["kb"][j] = k_b.astype(BF16)
        z["qbi"][j] = (q_b * jnp.concatenate([dec_in, dec_in], axis=-1)).astype(BF16)
        z["kbo"][j] = (k_b * jnp.concatenate([dec_out, dec_out], axis=-1)).astype(BF16)
        yield
        z["vb"][j] = proj("vb", j, RET_DV).astype(BF16)
        g_b = proj("gb", j, RET_DV)
        yield
        z["gtb"][:, j * RET_DV:(j + 1) * RET_DV] = _sigmoid(proj("mb", j, RET_DV)) * (g_b * _sigmoid(g_b))
        yield


def _recur_block(z, lg_ref, g, gn_ref, sgt_ref, sret_ref, merged_ref):
    T = T_BLK
    ri = lax.broadcasted_iota(jnp.int32, (H_GLA, H_GLA), 0)
    ci = lax.broadcasted_iota(jnp.int32, (H_GLA, H_GLA), 1)
    causal = ri >= ci
    st = sgt_ref[...]
    o_chunks = []
    for c in range(N_CHUNKS):
        sl = slice(c * C_GLA, (c + 1) * C_GLA)
        v_c = z["va"][sl]
        diag = []
        for h in range(2):
            rows = slice(c * C_GLA + h * H_GLA, c * C_GLA + (h + 1) * H_GLA)
            diag.append(jnp.where(causal, _dot_nt(z["qx"][rows], z["kx"][rows]), 0.0))
        cross = _dot_nt(z["qc"][c], z["kc"][c])
        a = jnp.concatenate([jnp.concatenate([diag[0], jnp.zeros_like(cross)], axis=1),
                             jnp.concatenate([cross, diag[1]], axis=1)], axis=0).astype(BF16)
        o_chunks.append(_dot_nt(z["qi"][sl], st.astype(BF16)) + _dot(a, v_c))
        st = st * z["dl"][c:c + 1] + _dot_tn(v_c, z["ko"][sl])
        yield
    sgt_ref[...] = st
    part_a = z["gta"][...] * (_rms(jnp.concatenate(o_chunks, axis=0)) * gn_ref[...])
    yield

    rt = lax.broadcasted_iota(jnp.int32, (T, T), 0)
    ct = lax.broadcasted_iota(jnp.int32, (T, T), 1)
    parts_b = []
    for j in range(RET_PER_GROUP):
        lg = lg_ref[g * RET_PER_GROUP + j]
        v_b = z["vb"][j]
        dmat = jnp.where(rt >= ct, jnp.exp((rt - ct).astype(F32) * lg), 0.0)
        a = (_dot_nt(z["qb"][j], z["kb"][j]) * dmat).astype(BF16)
        s = sret_ref[0, j]
        o_b = _rms(_dot(z["qbi"][j], s.astype(BF16)) + _dot(a, v_b))
        dec_all = jnp.exp(jnp.full((1, RET_DV), T * lg, F32))
        sret_ref[0, j] = s * dec_all + _dot_tn(z["kbo"][j], v_b)
        parts_b.append(z["gtb"][:, j * RET_DV:(j + 1) * RET_DV] * o_b)
        yield
    merged_ref[0] = (part_a + jnp.concatenate(parts_b, axis=-1)).astype(merged_ref.dtype)


def _mix_prompt_kernel(lg_ref, u_ref, r_ref, *refs, blocks_per_group, blocks_per_seq):
    nw, nz = len(W_NAMES), len(Z_NAMES)
    w = dict(zip(W_NAMES, refs[:nw]))
    refs = list(refs[nw:])
    wup_ref, bg_ref, gn_ref, cos_ref, sin_ref = refs[:5]
    sample_in = refs[5:5 + N_SAMPLE_IN]
    cast_in = refs[5 + N_SAMPLE_IN:5 + N_SAMPLE_IN + N_CAST]
    refs = refs[5 + N_SAMPLE_IN + N_CAST:]
    merged_ref, sgla_ref, sret_ref = refs[:3]
    sample_out = refs[3:3 + N_SAMPLE_OUT]
    cast_out = refs[3 + N_SAMPLE_OUT:3 + N_SAMPLE_OUT + N_CAST]
    sgt_ref = refs[3 + N_SAMPLE_OUT + N_CAST]
    refs = refs[4 + N_SAMPLE_OUT + N_CAST:]
    z_even = dict(zip(Z_NAMES, refs[:nz]))
    z_odd = dict(zip(Z_NAMES, refs[nz:]))
    s = pl.program_id(0)
    n_blocks = pl.num_programs(0) - 1
    g_p = jnp.minimum(s, n_blocks - 1) // blocks_per_group
    r = jnp.maximum(s - 1, 0)
    g_r = r // blocks_per_group
    t_r = r % blocks_per_seq

    @pl.when(s == 0)
    def _():
        for ref in z_odd.values():
            ref[...] = jnp.zeros_like(ref)

    @pl.when(t_r == 0)
    def _():
        sgt_ref[...] = jnp.zeros_like(sgt_ref)
        sret_ref[...] = jnp.zeros_like(sret_ref)

    for src, dst in zip(cast_in, cast_out):
        dst[...] = src[...].astype(dst.dtype)

    def step(z_write, z_read):
        rec = _recur_block(z_read, lg_ref, g_r, gn_ref, sgt_ref, sret_ref, merged_ref)
        prj = _project_block(u_ref, r_ref, w, wup_ref, bg_ref, cos_ref, sin_ref, lg_ref, g_p, z_write)
        upd = _state_update(lg_ref, *sample_in, *sample_out)
        for _ in itertools.zip_longest(prj, rec, upd):
            pass

    @pl.when(s % 2 == 0)
    def _():
        step(z_even, z_odd)

    @pl.when(s % 2 == 1)
    def _():
        step(z_odd, z_even)

    @pl.when((t_r == blocks_per_seq - 1) & (s > 0))
    def _():
        sgla_ref[0, 0] = sgt_ref[...].T


def _norm_kernel(x_ref, g_ref, wr_ref, u_ref, r_ref):
    u = (_rms(x_ref[...]) * g_ref[...]).astype(u_ref.dtype)
    u_ref[...] = u
    r_ref[...] = _gate_code(u, wr_ref)


def _norm_prompt(x, gain, w_f32):
    n, D = x.shape
    tm = min(TM_NORM, n)
    return pl.pallas_call(
        _norm_kernel,
        grid=(n // tm,),
        in_specs=[pl.BlockSpec((tm, D), lambda i: (i, 0)), pl.BlockSpec((1, D), lambda i: (0, 0)),
                  _gate_code_spec()],
        out_specs=[pl.BlockSpec((tm, D), lambda i: (i, 0)), pl.BlockSpec((tm, LANE), lambda i: (i, 0))],
        out_shape=[jax.ShapeDtypeStruct((n, D), BF16), jax.ShapeDtypeStruct((n, LANE), F32)],
        compiler_params=pltpu.CompilerParams(
            dimension_semantics=("arbitrary",), vmem_limit_bytes=VMEM_LIMIT),
        name="norm_prompt",
    )(x, gain, w_f32)


def _mix_prompt(u, r, lg, w_t, wup, bg, gn, cos, sin, sg, sr, rows, to_cast):
    B, L, D = u.shape
    nt = L // T_BLK
    n_blocks = N_GROUPS * B * nt
    n_req = sg.shape[0]
    assert n_req <= n_blocks + 1
    assert rows.shape == (n_req, 1, ROW_W) and len(to_cast) == N_CAST
    slab = ROW_TILE_BF16
    assert all(a.shape[0] <= slab * n_blocks and a.shape[0] % slab == 0 for a in to_cast)

    def cast_specs():
        return [pl.BlockSpec((slab, a.shape[1]), lambda s, n=a.shape[0] // slab: (jnp.minimum(s, n - 1), 0))
                for a in to_cast]

    def req_row(s):
        return (jnp.minimum(s, n_req - 1), 0, 0)

    def req_blk(s):
        return (jnp.minimum(s, n_req - 1), 0, 0, 0)

    def proj_idx(s):
        p = jnp.minimum(s, n_blocks - 1)
        return p // (B * nt), (p // nt) % B, p % nt

    def recur_idx(s):
        r = jnp.maximum(s - 1, 0)
        return r // (B * nt), (r // nt) % B, r % nt

    def out_map(s):
        g, b, t = recur_idx(s)
        return (b, t, g)

    def state_map(s):
        g, b, _ = recur_idx(s)
        return (b, g, 0, 0)

    z_scratch = [pltpu.VMEM(shape, dtype) for _, shape, dtype in Z_BUFFERS]
    return pl.pallas_call(
        functools.partial(_mix_prompt_kernel, blocks_per_group=B * nt, blocks_per_seq=nt),
        grid=(n_blocks + 1,),
        in_specs=[
            pl.BlockSpec(memory_space=pltpu.SMEM),
            pl.BlockSpec((1, T_BLK, D), lambda s: (proj_idx(s)[1], proj_idx(s)[2], 0)),
            pl.BlockSpec((1, T_BLK, LANE), lambda s: (proj_idx(s)[1], proj_idx(s)[2], 0)),
            *_weight_specs(lambda s: proj_idx(s)[0], pipeline_mode=pl.Buffered(1)),
            pl.BlockSpec((1, LANE, GLA_DK), lambda s: (proj_idx(s)[0], 0, 0)),
            pl.BlockSpec((1, 1, GLA_DK), lambda s: (proj_idx(s)[0], 0, 0)),
            pl.BlockSpec((1, GLA_DV), lambda s: (0, 0)),
            pl.BlockSpec((T_BLK, LANE), lambda s: (proj_idx(s)[2], 0)),
            pl.BlockSpec((T_BLK, LANE), lambda s: (proj_idx(s)[2], 0)),
            pl.BlockSpec((1, GLA_HEADS, GLA_DK, GLA_DV), req_blk),
            pl.BlockSpec((1, RET_HEADS, RET_DK, RET_DV), req_blk),
            pl.BlockSpec((1, 1, ROW_W), req_row),
            *cast_specs(),
        ],
        out_specs=[
            pl.BlockSpec((1, T_BLK, GW), out_map),
            pl.BlockSpec((1, 1, GLA_DK, GLA_DV), state_map),
            pl.BlockSpec((1, RET_PER_GROUP, RET_DK, RET_DV), state_map),
            pl.BlockSpec((1, GLA_HEADS, GLA_DK, GLA_DV), req_blk),
            pl.BlockSpec((1, RET_HEADS, RET_DK, RET_DV), req_blk),
            pl.BlockSpec((1, 1, O_W), req_row),
            *cast_specs(),
        ],
        out_shape=[
            jax.ShapeDtypeStruct((B, L, D_MODEL), BF16),
            jax.ShapeDtypeStruct((B, GLA_HEADS, GLA_DK, GLA_DV), F32),
            jax.ShapeDtypeStruct((B, RET_HEADS, RET_DK, RET_DV), F32),
            jax.ShapeDtypeStruct(sg.shape, F32),
            jax.ShapeDtypeStruct(sr.shape, F32),
            jax.ShapeDtypeStruct((n_req, 1, O_W), F32),
            *[jax.ShapeDtypeStruct(a.shape, BF16) for a in to_cast],
        ],
        scratch_shapes=[pltpu.VMEM((GLA_DV, GLA_DK), F32)] + z_scratch + z_scratch,
        compiler_params=pltpu.CompilerParams(
            dimension_semantics=("arbitrary",), vmem_limit_bytes=VMEM_LIMIT),
        name="mix_prompt",
    )(lg, u, r, *([w_t] * N_W), wup, bg, gn, cos, sin, sg, sr, rows, *to_cast)


def _out_kernel(x_ref, mg_ref, p_ref, wout_ref, nple_ref, wpg_ref, wpp_ref, nfin_ref, y_ref, *,
                final_norm):
    h = _tokens(x_ref) + _dot(mg_ref[...], wout_ref[...])
    hn = (_rms(h) * nple_ref[...]).astype(BF16)
    gate = _sigmoid(_dot(hn, wpg_ref[...]))
    h = h + gate * _dot(_tokens(p_ref).astype(BF16), wpp_ref[...])
    if final_norm:
        h = _rms(h) * nfin_ref[...]
    if len(y_ref.shape) == 2:
        y_ref[...] = h
    else:
        y_ref[:, 0, :] = h


def _out_proj(x, merged, p, w_out, nple, w_pg, w_pp, nfin, final_norm):
    n, D = x.shape[0], x.shape[-1]
    tm = min(TM_OUT, n)
    const = lambda i: (0, 0)

    def token_spec(a):
        return pl.BlockSpec((tm,) + a.shape[1:], lambda i: (i,) + (0,) * (a.ndim - 1))

    return pl.pallas_call(
        functools.partial(_out_kernel, final_norm=final_norm),
        grid=(n // tm,),
        in_specs=[
            token_spec(x),
            token_spec(merged),
            token_spec(p),
            pl.BlockSpec((D, D), const, pipeline_mode=pl.Buffered(1)),
            pl.BlockSpec((1, D), const),
            pl.BlockSpec((D, D), const, pipeline_mode=pl.Buffered(1)),
            pl.BlockSpec((PLE_DIM, D), const, pipeline_mode=pl.Buffered(1)),
            pl.BlockSpec((1, D), const),
        ],
        out_specs=token_spec(x),
        out_shape=jax.ShapeDtypeStruct(x.shape, F32),
        compiler_params=pltpu.CompilerParams(
            dimension_semantics=("arbitrary",), vmem_limit_bytes=VMEM_LIMIT),
        name="out_proj",
    )(x, merged, p, w_out, nple, w_pg, w_pp, nfin)


def _prep_kernel(x_ref, nmix_ref, w_ref, wr_ref, wb_ref, z_ref, r_ref, u_ref):
    @pl.when(pl.program_id(0) == 0)
    def _():
        u = (_rms(_tokens(x_ref)) * nmix_ref[...]).astype(BF16)
        u_ref[...] = u
        r_ref[...] = _gate_code(u, wr_ref)

    wb = w_ref[...].astype(BF16)
    wb_ref[...] = wb.T
    z_ref[...] = _dot_nt(u_ref[...], wb)


def _prep_weights(x, nmix, w_f32):
    n, D = x.shape[0], x.shape[-1]
    lo_chunks = R_START // PREP_ROWS
    assert R_START % PREP_ROWS == 0 and N_PACK % PREP_ROWS == 0 and HI_START % SUBLANE == 0

    def rows(k):
        skip = jnp.where(k >= lo_chunks, GLA_RANK // SUBLANE, 0)
        return (SUBLANE * ((PREP_ROWS // SUBLANE) * k + skip), 0)

    return pl.pallas_call(
        _prep_kernel,
        grid=(N_PACK // PREP_ROWS,),
        in_specs=[
            pl.BlockSpec(x.shape, lambda k: (0,) * x.ndim),
            pl.BlockSpec((1, D), lambda k: (0, 0)),
            pl.BlockSpec((pl.Element(PREP_ROWS), pl.Element(D)), rows),
            _gate_code_spec(),
        ],
        out_specs=[
            pl.BlockSpec((D, PREP_ROWS), lambda k: (0, k)),
            pl.BlockSpec((n, PREP_ROWS), lambda k: (0, k)),
            pl.BlockSpec((n, LANE), lambda k: (0, 0)),
        ],
        out_shape=[
            jax.ShapeDtypeStruct((D, N_PACK), BF16),
            jax.ShapeDtypeStruct((n, N_PACK), F32),
            jax.ShapeDtypeStruct((n, LANE), F32),
        ],
        scratch_shapes=[pltpu.VMEM((n, D), BF16)],
        compiler_params=pltpu.CompilerParams(
            dimension_semantics=("arbitrary",), vmem_limit_bytes=VMEM_LIMIT),
        name="prep_weights",
    )(x, nmix, w_f32, w_f32)


def _sample_transform_kernel(z_ref, r_ref, wup_ref, bg_ref, cos_ref, sin_ref, row_ref):
    def piece(name):
        start, width = W_PIECES[name]
        return z_ref[:, start:start + N_GROUPS * width]

    def put(name, value, h=0):
        off = ROW_OFFS[name] + h * value.shape[-1]
        row_ref[:, 0, off:off + value.shape[-1]] = value

    r = r_ref[...]
    for g in range(N_GROUPS):
        put("dec", jnp.exp(_gla_log_alpha(r, wup_ref[g], bg_ref[g])), g)
    put("qa", piece("qa") * (GLA_DK ** -0.5))
    for name in ("ka", "va", "vb"):
        put(name, piece(name))
    cos = cos_ref[...]
    sin = sin_ref[...]
    q_b = piece("qb")
    k_b = piece("kb")
    for h in range(RET_HEADS):
        sl = slice(h * RET_DK, (h + 1) * RET_DK)
        put("qb", _rotary(q_b[:, sl], cos, sin), h)
        put("kb", _rotary(k_b[:, sl], cos, sin) * (RET_DK ** -0.5), h)


def _sample_transform(z, r, wup, bg, cos, sin):
    n = z.shape[0]
    return pl.pallas_call(
        _sample_transform_kernel,
        out_shape=jax.ShapeDtypeStruct((n, 1, ROW_W), F32),
        compiler_params=pltpu.CompilerParams(vmem_limit_bytes=VMEM_LIMIT),
        name="sample_transform",
    )(z, r, wup, bg, cos, sin)


def _merge_sample_kernel(o_ref, z_ref, gn_ref, mg_ref):
    def gate(name, lo, n):
        start = W_PIECES[name][0] + lo
        return z_ref[:, start:start + n]

    for h in range(GLA_HEADS):
        sl = slice(h * GLA_DV, (h + 1) * GLA_DV)
        g_a = gate("ga", h * GLA_DV, GLA_DV)
        part_a = _sigmoid(gate("ma", h * GW, GW)) * (_rms(o_ref[:, 0, sl]) * gn_ref[...] * (g_a * _sigmoid(g_a)))
        parts_b = []
        for j in range(RET_PER_GROUP):
            lo = h * GW + j * RET_DV
            g_b = gate("gb", lo, RET_DV)
            o_b = o_ref[:, 0, GLA_V + lo:GLA_V + lo + RET_DV]
            parts_b.append(_sigmoid(gate("mb", lo, RET_DV)) * (_rms(o_b) * (g_b * _sigmoid(g_b))))
        mg_ref[:, sl] = (part_a + jnp.concatenate(parts_b, axis=-1)).astype(mg_ref.dtype)


def _merge_sample(o, z_raw, gn):
    n = o.shape[0]
    return pl.pallas_call(
        _merge_sample_kernel,
        out_shape=jax.ShapeDtypeStruct((n, D_MODEL), BF16),
        compiler_params=pltpu.CompilerParams(vmem_limit_bytes=VMEM_LIMIT),
        name="merge_sample",
    )(o, z_raw, gn)


def _rope_tables(pos):
    half = RET_DK // 2
    inv = 1.0 / (ROPE_BASE ** jnp.linspace(0.0, 1.0, half, dtype=jnp.float32))
    ang = pos[:, None] * inv[None, :]
    return jnp.cos(ang), jnp.sin(ang)


def kernel(x_prompt, x_sample, state_gla, state_ret, p_prompt, p_sample, norm_mix, w_in, w_gla_up, b_gla,
           gla_norm, w_out, norm_ple, w_ple_gate, w_ple_proj, norm_final):
    depth = w_in.shape[0]
    Bp, Lp, D = x_prompt.shape
    Bs, Ls, _ = x_sample.shape
    assert Ls == 1 and Lp % T_BLK == 0
    cos_p, sin_p = _rope_tables(jnp.arange(Lp, dtype=jnp.float32))
    cos_s, sin_s = _rope_tables(PAST_LEN + jnp.arange(Ls, dtype=jnp.float32))
    log_gamma = jnp.log(1.0 - jnp.exp2(-5.0 - jnp.arange(RET_HEADS, dtype=jnp.float32)))
    nfin = norm_final.reshape(1, D)

    hp = x_prompt
    hs = x_sample
    gla_p, ret_p, gla_s, ret_s = [], [], [], []
    for i in range(depth):
        last = i == depth - 1
        nmix = norm_mix[i].reshape(1, D)
        nple = norm_ple[i].reshape(1, D)
        gn = gla_norm[i].reshape(1, GLA_DV)
        w_f32 = w_in[i].T
        wup = jnp.pad(w_gla_up[i], ((0, LANE - GLA_RANK), (0, 0))).astype(BF16)
        wup = wup.reshape(LANE, GLA_HEADS, GLA_DK).transpose(1, 0, 2)
        bg = b_gla[i].reshape(GLA_HEADS, 1, GLA_DK)
        w_pp = w_ple_proj[i].astype(BF16)

        w_t, z_raw, r_s = _prep_weights(hs, nmix, w_f32)
        rows = _sample_transform(z_raw, r_s, wup, bg, cos_s, sin_s)
        u, r = _norm_prompt(hp.reshape(Bp * Lp, D), nmix, w_f32)
        merged, sg, sr, nsg, nsr, o_s, w_o, w_pg = _mix_prompt(
            u.reshape(Bp, Lp, D), r.reshape(Bp, Lp, LANE), log_gamma, w_t, wup, bg, gn, cos_p, sin_p,
            state_gla[i], state_ret[i], rows, [w_out[i], w_ple_gate[i]])
        hp = _out_proj(hp.reshape(Bp * Lp, D), merged.reshape(Bp * Lp, D), p_prompt[i].reshape(Bp * Lp, PLE_DIM),
                       w_o, nple, w_pg, w_pp, nfin, last).reshape(Bp, Lp, D)
        gla_p.append(sg)
        ret_p.append(sr)

        merged_s = _merge_sample(o_s, z_raw, gn)
        hs = _out_proj(hs, merged_s, p_sample[i], w_o, nple, w_pg, w_pp, nfin, last)
        gla_s.append(nsg)
        ret_s.append(nsr)

    return (hp, hs, jnp.stack(gla_p), jnp.stack(ret_p), jnp.stack(gla_s), jnp.stack(ret_s))
```

```python
import functools
import itertools

import jax
import jax.numpy as jnp
import numpy as np
from jax import lax
from jax.experimental import pallas as pl
from jax.experimental.pallas import tpu as pltpu

F32 = jnp.float32
BF16 = jnp.bfloat16

D_MODEL = 2048
PAST_LEN = 16384
PLE_DIM = 256
GLA_HEADS = 4
GLA_DK = 256
GLA_DV = 512
GLA_RANK = 16
GLA_TAU = 16.0
RET_HEADS = 8
RET_DK = 256
RET_DV = 256
ROPE_BASE = 10000.0
EPS = 1e-6

GLA_QK = GLA_HEADS * GLA_DK
GLA_V = GLA_HEADS * GLA_DV
RET_QK = RET_HEADS * RET_DK
RET_V = RET_HEADS * RET_DV
IN_SPLITS = (GLA_QK, GLA_QK, GLA_V, GLA_V, GLA_RANK, RET_QK, RET_QK, RET_V, RET_V, D_MODEL, D_MODEL)
IN_OFFS = tuple(int(v) for v in np.concatenate([[0], np.cumsum(IN_SPLITS)[:-1]]))

N_GROUPS = GLA_HEADS
RET_PER_GROUP = RET_HEADS // N_GROUPS
GW = GLA_DV
LANE = 128
SUBLANE = 8
ROW_TILE_BF16 = 16

R_START = IN_OFFS[4]
HI_START = IN_OFFS[5]
N_PACK = HI_START - GLA_RANK + sum(IN_SPLITS[5:])
W_PIECES = {
    "qa": (IN_OFFS[0], GLA_DK),
    "ka": (IN_OFFS[1], GLA_DK),
    "va": (IN_OFFS[2], GLA_DV),
    "ga": (IN_OFFS[3], GLA_DV),
    "qb": (IN_OFFS[5] - GLA_RANK, GW),
    "kb": (IN_OFFS[6] - GLA_RANK, GW),
    "vb": (IN_OFFS[7] - GLA_RANK, GW),
    "gb": (IN_OFFS[8] - GLA_RANK, GW),
    "ma": (IN_OFFS[9] - GLA_RANK, GW),
    "mb": (IN_OFFS[10] - GLA_RANK, GW),
}
W_NAMES = tuple(W_PIECES)

T_BLK = 256
C_GLA = 128
H_GLA = C_GLA // 2
N_CHUNKS = T_BLK // C_GLA
TM_OUT = 512
TM_NORM = 1024
PREP_ROWS = 1024
VMEM_LIMIT = 56 * 1024 * 1024


def _rms(x):
    return x * lax.rsqrt(jnp.mean(x * x, axis=-1, keepdims=True) + EPS)


def _sigmoid(x):
    return 1.0 / (1.0 + jnp.exp(-x))


def _log_sigmoid(x):
    return jnp.minimum(x, 0.0) - jnp.log(1.0 + jnp.exp(-jnp.abs(x)))


def _tokens(ref):
    return ref[...] if len(ref.shape) == 2 else ref[:, 0, :]


def _dot(a, b):
    return jnp.dot(a, b, preferred_element_type=F32)


def _dot_nt(a, b):
    return lax.dot_general(a, b, (((1,), (1,)), ((), ())), preferred_element_type=F32)


def _dot_tn(a, b):
    return lax.dot_general(a, b, (((0,), (0,)), ((), ())), preferred_element_type=F32)


def _rotary(x, cos, sin):
    half = x.shape[-1] // 2
    x1, x2 = x[:, :half], x[:, half:]
    return jnp.concatenate([x1 * cos - x2 * sin, x1 * sin + x2 * cos], axis=-1)


def _gla_log_alpha(r, wup, bg):
    pre = _dot(r.astype(BF16), wup) + bg
    return _log_sigmoid(pre) * (1.0 / GLA_TAU)


def _weight_specs(group_of, **kw):
    specs = []
    for n in W_NAMES:
        start, width = W_PIECES[n]
        assert start % width == 0

        def index_map(*idx, first=start // width):
            return (0, first + group_of(*idx))

        specs.append(pl.BlockSpec((D_MODEL, width), index_map, **kw))
    return specs


def _gate_code_spec():
    return pl.BlockSpec((pl.Element(GLA_RANK), pl.Element(D_MODEL)), lambda *idx: (R_START, 0))


def _gate_code(u, wr_ref):
    wr = jnp.concatenate([wr_ref[...].astype(BF16), jnp.zeros((LANE - GLA_RANK, D_MODEL), BF16)], axis=0)
    return _dot_nt(u, wr)


N_W = len(W_NAMES)


Z_BUFFERS = (
    ("qx", (T_BLK, GLA_DK), BF16),
    ("kx", (T_BLK, GLA_DK), BF16),
    ("qc", (N_CHUNKS, H_GLA, GLA_DK), BF16),
    ("kc", (N_CHUNKS, H_GLA, GLA_DK), BF16),
    ("qi", (T_BLK, GLA_DK), BF16),
    ("ko", (T_BLK, GLA_DK), BF16),
    ("va", (T_BLK, GLA_DV), BF16),
    ("dl", (SUBLANE, GLA_DK), F32),
    ("gta", (T_BLK, GW), F32),
    ("qb", (RET_PER_GROUP, T_BLK, RET_DK), BF16),
    ("kb", (RET_PER_GROUP, T_BLK, RET_DK), BF16),
    ("qbi", (RET_PER_GROUP, T_BLK, RET_DK), BF16),
    ("kbo", (RET_PER_GROUP, T_BLK, RET_DK), BF16),
    ("vb", (RET_PER_GROUP, T_BLK, RET_DV), BF16),
    ("gtb", (T_BLK, GW), F32),
)
Z_NAMES = tuple(n for n, _, _ in Z_BUFFERS)
assert N_CHUNKS <= SUBLANE


VT_DEC = 0
VT_KA = VT_DEC + GLA_HEADS
VT_QA = VT_KA + GLA_HEADS
VT_KB = VT_QA + GLA_HEADS
VT_QB = VT_KB + RET_HEADS
VT_N = VT_QB + RET_HEADS
VT_PAD = -(-VT_N // SUBLANE) * SUBLANE


ROW_PIECES = (("dec", GLA_HEADS, GLA_DK), ("ka", GLA_HEADS, GLA_DK), ("qa", GLA_HEADS, GLA_DK),
              ("kb", RET_HEADS, RET_DK), ("qb", RET_HEADS, RET_DK), ("va", GLA_HEADS, GLA_DV),
              ("vb", RET_HEADS, RET_DV))
ROW_OFFS = dict(zip([n for n, _, _ in ROW_PIECES],
                    np.concatenate([[0], np.cumsum([h * w for _, h, w in ROW_PIECES])[:-1]]).tolist()))
ROW_W = sum(h * w for _, h, w in ROW_PIECES)
O_W = GLA_V + RET_V


def _state_update(lg_ref, sg_ref, sr_ref, row_ref, nsg_ref, nsr_ref, o_ref):
    def vec(name, h, width):
        off = ROW_OFFS[name] + h * width
        return row_ref[0, :, off:off + width]

    rows = [vec(name, h, GLA_DK) for name, heads, _ in ROW_PIECES[:5] for h in range(heads)]
    rows.append(jnp.zeros((VT_PAD - VT_N, GLA_DK), F32))
    vt = jnp.concatenate(rows, axis=0).T

    def col(i):
        return vt[:, i:i + 1]

    for h in range(GLA_HEADS):
        s_new = col(VT_DEC + h) * sg_ref[0, h] + col(VT_KA + h) * vec("va", h, GLA_DV)
        nsg_ref[0, h] = s_new
        o_ref[0, :, h * GLA_DV:(h + 1) * GLA_DV] = jnp.sum(col(VT_QA + h) * s_new, axis=0, keepdims=True)
        yield
    for h in range(RET_HEADS):
        gamma = jnp.exp(jnp.full((1, RET_DV), lg_ref[h], F32))
        s_new = gamma * sr_ref[0, h] + col(VT_KB + h) * vec("vb", h, RET_DV)
        nsr_ref[0, h] = s_new
        o_ref[0, :, GLA_V + h * RET_DV:GLA_V + (h + 1) * RET_DV] = jnp.sum(
            col(VT_QB + h) * s_new, axis=0, keepdims=True)
        yield


N_SAMPLE_IN = 3
N_SAMPLE_OUT = 3
N_CAST = 2


def _project_block(u_ref, r_ref, w, wup_ref, bg_ref, cos_ref, sin_ref, lg_ref, g, z):
    T = T_BLK
    u = u_ref[0]

    def proj(name, j=0, n=None):
        ref = w[name]
        n = ref.shape[1] if n is None else n
        return _dot(u, ref[:, j * n:(j + 1) * n])

    la = _gla_log_alpha(r_ref[0], wup_ref[0], bg_ref[0])
    yield
    q = proj("qa") * (GLA_DK ** -0.5)
    yield
    k = proj("ka")
    ri = lax.broadcasted_iota(jnp.int32, (C_GLA, C_GLA), 0)
    ci = lax.broadcasted_iota(jnp.int32, (C_GLA, C_GLA), 1)
    tri = jnp.where(ri >= ci, 1.0, 0.0).astype(BF16)
    for c in range(N_CHUNKS):
        sl = slice(c * C_GLA, (c + 1) * C_GLA)
        la_c = la[sl]
        la_hi = la_c.astype(BF16)
        la_lo = (la_c - la_hi.astype(F32)).astype(BF16)
        b = _dot(tri, la_hi) + _dot(tri, la_lo)
        b_last = b[C_GLA - 1:C_GLA]
        q_c, k_c = q[sl], k[sl]
        z["qi"][sl] = (q_c * jnp.exp(b)).astype(BF16)
        z["ko"][sl] = (k_c * jnp.exp(b_last - b)).astype(BF16)
        z["dl"][c:c + 1] = jnp.exp(b_last)
        for h in range(2):
            hs = slice(h * H_GLA, (h + 1) * H_GLA)
            rows = slice(c * C_GLA + h * H_GLA, c * C_GLA + (h + 1) * H_GLA)
            b_h = b[hs]
            b_mid = b_h[H_GLA // 2 - 1:H_GLA // 2]
            z["qx"][rows] = (q_c[hs] * jnp.exp(b_h - b_mid)).astype(BF16)
            z["kx"][rows] = (k_c[hs] * jnp.exp(b_mid - b_h)).astype(BF16)
        b_edge = b[H_GLA - 1:H_GLA]
        z["qc"][c] = (q_c[H_GLA:] * jnp.exp(b[H_GLA:] - b_edge)).astype(BF16)
        z["kc"][c] = (k_c[:H_GLA] * jnp.exp(b_edge - b[:H_GLA])).astype(BF16)
    yield
    z["va"][...] = proj("va").astype(BF16)
    yield
    g_a = proj("ga")
    silu_a = g_a * _sigmoid(g_a)
    yield
    z["gta"][...] = _sigmoid(proj("ma")) * silu_a
    yield

    cos = cos_ref[...]
    sin = sin_ref[...]
    tl = lax.broadcasted_iota(jnp.int32, (T, LANE), 0).astype(F32)
    for j in range(RET_PER_GROUP):
        lg = lg_ref[g * RET_PER_GROUP + j]
        q_b = _rotary(proj("qb", j, RET_DK), cos, sin)
        k_b = _rotary(proj("kb", j, RET_DK), cos, sin) * (RET_DK ** -0.5)
        dec_in = jnp.exp((tl + 1.0) * lg)
        dec_out = jnp.exp((T - 1.0 - tl) * lg)
        z["qb"][j] = q_b.astype(BF16)
        z["kb"][j] = k_b.astype(BF16)
        z["qbi"][j] = (q_b * jnp.concatenate([dec_in, dec_in], axis=-1)).astype(BF16)
        z["kbo"][j] = (k_b * jnp.concatenate([dec_out, dec_out], axis=-1)).astype(BF16)
        yield
        z["vb"][j] = proj("vb", j, RET_DV).astype(BF16)
        g_b = proj("gb", j, RET_DV)
        yield
        z["gtb"][:, j * RET_DV:(j + 1) * RET_DV] = _sigmoid(proj("mb", j, RET_DV)) * (g_b * _sigmoid(g_b))
        yield


def _recur_block(z, lg_ref, g, gn_ref, sgt_ref, sret_ref, merged_ref):
    T = T_BLK
    ri = lax.broadcasted_iota(jnp.int32, (H_GLA, H_GLA), 0)
    ci = lax.broadcasted_iota(jnp.int32, (H_GLA, H_GLA), 1)
    causal = ri >= ci
    st = sgt_ref[...]
    o_chunks = []
    for c in range(N_CHUNKS):
        sl = slice(c * C_GLA, (c + 1) * C_GLA)
        v_c = z["va"][sl]
        diag = []
        for h in range(2):
            rows = slice(c * C_GLA + h * H_GLA, c * C_GLA + (h + 1) * H_GLA)
            diag.append(jnp.where(causal, _dot_nt(z["qx"][rows], z["kx"][rows]), 0.0))
        cross = _dot_nt(z["qc"][c], z["kc"][c])
        a = jnp.concatenate([jnp.concatenate([diag[0], jnp.zeros_like(cross)], axis=1),
                             jnp.concatenate([cross, diag[1]], axis=1)], axis=0).astype(BF16)
        o_chunks.append(_dot_nt(z["qi"][sl], st.astype(BF16)) + _dot(a, v_c))
        st = st * z["dl"][c:c + 1] + _dot_tn(v_c, z["ko"][sl])
        yield
    sgt_ref[...] = st
    part_a = z["gta"][...] * (_rms(jnp.concatenate(o_chunks, axis=0)) * gn_ref[...])
    yield

    rt = lax.broadcasted_iota(jnp.int32, (T, T), 0)
    ct = lax.broadcasted_iota(jnp.int32, (T, T), 1)
    parts_b = []
    for j in range(RET_PER_GROUP):
        lg = lg_ref[g * RET_PER_GROUP + j]
        v_b = z["vb"][j]
        dmat = jnp.where(rt >= ct, jnp.exp((rt - ct).astype(F32) * lg), 0.0)
        a = (_dot_nt(z["qb"][j], z["kb"][j]) * dmat).astype(BF16)
        s = sret_ref[0, j]
        o_b = _rms(_dot(z["qbi"][j], s.astype(BF16)) + _dot(a, v_b))
        dec_all = jnp.exp(jnp.full((1, RET_DV), T * lg, F32))
        sret_ref[0, j] = s * dec_all + _dot_tn(z["kbo"][j], v_b)
        parts_b.append(z["gtb"][:, j * RET_DV:(j + 1) * RET_DV] * o_b)
        yield
    merged_ref[0] = (part_a + jnp.concatenate(parts_b, axis=-1)).astype(merged_ref.dtype)


def _mix_prompt_kernel(lg_ref, u_ref, r_ref, *refs, blocks_per_group, blocks_per_seq):
    nw, nz = len(W_NAMES), len(Z_NAMES)
    w = dict(zip(W_NAMES, refs[:nw]))
    refs = list(refs[nw:])
    wup_ref, bg_ref, gn_ref, cos_ref, sin_ref = refs[:5]
    sample_in = refs[5:5 + N_SAMPLE_IN]
    cast_in = refs[5 + N_SAMPLE_IN:5 + N_SAMPLE_IN + N_CAST]
    refs = refs[5 + N_SAMPLE_IN + N_CAST:]
    merged_ref, sgla_ref, sret_ref = refs[:3]
    sample_out = refs[3:3 + N_SAMPLE_OUT]
    cast_out = refs[3 + N_SAMPLE_OUT:3 + N_SAMPLE_OUT + N_CAST]
    sgt_ref = refs[3 + N_SAMPLE_OUT + N_CAST]
    refs = refs[4 + N_SAMPLE_OUT + N_CAST:]
    z_even = dict(zip(Z_NAMES, refs[:nz]))
    z_odd = dict(zip(Z_NAMES, refs[nz:]))
    s = pl.program_id(0)
    n_blocks = pl.num_programs(0) - 1
    g_p = jnp.minimum(s, n_blocks - 1) // blocks_per_group
    r = jnp.maximum(s - 1, 0)
    g_r = r // blocks_per_group
    t_r = r % blocks_per_seq

    @pl.when(s == 0)
    def _():
        for ref in z_odd.values():
            ref[...] = jnp.zeros_like(ref)

    @pl.when(t_r == 0)
    def _():
        sgt_ref[...] = jnp.zeros_like(sgt_ref)
        sret_ref[...] = jnp.zeros_like(sret_ref)

    for src, dst in zip(cast_in, cast_out):
        dst[...] = src[...].astype(dst.dtype)

    def step(z_write, z_read):
        rec = _recur_block(z_read, lg_ref, g_r, gn_ref, sgt_ref, sret_ref, merged_ref)
        prj = _project_block(u_ref, r_ref, w, wup_ref, bg_ref, cos_ref, sin_ref, lg_ref, g_p, z_write)
        upd = _state_update(lg_ref, *sample_in, *sample_out)
        for _ in itertools.zip_longest(prj, rec, upd):
            pass

    @pl.when(s % 2 == 0)
    def _():
        step(z_even, z_odd)

    @pl.when(s % 2 == 1)
    def _():
        step(z_odd, z_even)

    @pl.when((t_r == blocks_per_seq - 1) & (s > 0))
    def _():
        sgla_ref[0, 0] = sgt_ref[...].T


def _norm_kernel(x_ref, g_ref, wr_ref, u_ref, r_ref):
    u = (_rms(x_ref[...]) * g_ref[...]).astype(u_ref.dtype)
    u_ref[...] = u
    r_ref[...] = _gate_code(u, wr_ref)


def _norm_prompt(x, gain, w_f32):
    n, D = x.shape
    tm = min(TM_NORM, n)
    return pl.pallas_call(
        _norm_kernel,
        grid=(n // tm,),
        in_specs=[pl.BlockSpec((tm, D), lambda i: (i, 0)), pl.BlockSpec((1, D), lambda i: (0, 0)),
                  _gate_code_spec()],
        out_specs=[pl.BlockSpec((tm, D), lambda i: (i, 0)), pl.BlockSpec((tm, LANE), lambda i: (i, 0))],
        out_shape=[jax.ShapeDtypeStruct((n, D), BF16), jax.ShapeDtypeStruct((n, LANE), F32)],
        compiler_params=pltpu.CompilerParams(
            dimension_semantics=("arbitrary",), vmem_limit_bytes=VMEM_LIMIT),
        name="norm_prompt",
    )(x, gain, w_f32)


def _mix_prompt(u, r, lg, w_t, wup, bg, gn, cos, sin, sg, sr, rows, to_cast):
    B, L, D = u.shape
    nt = L // T_BLK
    n_blocks = N_GROUPS * B * nt
    n_req = sg.shape[0]
    assert n_req <= n_blocks + 1
    assert rows.shape == (n_req, 1, ROW_W) and len(to_cast) == N_CAST
    slab = ROW_TILE_BF16
    assert all(a.shape[0] <= slab * n_blocks and a.shape[0] % slab == 0 for a in to_cast)

    def cast_specs():
        return [pl.BlockSpec((slab, a.shape[1]), lambda s, n=a.shape[0] // slab: (jnp.minimum(s, n - 1), 0))
                for a in to_cast]

    def req_row(s):
        return (jnp.minimum(s, n_req - 1), 0, 0)

    def req_blk(s):
        return (jnp.minimum(s, n_req - 1), 0, 0, 0)

    def proj_idx(s):
        p = jnp.minimum(s, n_blocks - 1)
        return p // (B * nt), (p // nt) % B, p % nt

    def recur_idx(s):
        r = jnp.maximum(s - 1, 0)
        return r // (B * nt), (r // nt) % B, r % nt

    def out_map(s):
        g, b, t = recur_idx(s)
        return (b, t, g)

    def state_map(s):
        g, b, _ = recur_idx(s)
        return (b, g, 0, 0)

    z_scratch = [pltpu.VMEM(shape, dtype) for _, shape, dtype in Z_BUFFERS]
    return pl.pallas_call(
        functools.partial(_mix_prompt_kernel, blocks_per_group=B * nt, blocks_per_seq=nt),
        grid=(n_blocks + 1,),
        in_specs=[
            pl.BlockSpec(memory_space=pltpu.SMEM),
            pl.BlockSpec((1, T_BLK, D), lambda s: (proj_idx(s)[1], proj_idx(s)[2], 0)),
            pl.BlockSpec((1, T_BLK, LANE), lambda s: (proj_idx(s)[1], proj_idx(s)[2], 0)),
            *_weight_specs(lambda s: proj_idx(s)[0], pipeline_mode=pl.Buffered(1)),
            pl.BlockSpec((1, LANE, GLA_DK), lambda s: (proj_idx(s)[0], 0, 0)),
            pl.BlockSpec((1, 1, GLA_DK), lambda s: (proj_idx(s)[0], 0, 0)),
            pl.BlockSpec((1, GLA_DV), lambda s: (0, 0)),
            pl.BlockSpec((T_BLK, LANE), lambda s: (proj_idx(s)[2], 0)),
            pl.BlockSpec((T_BLK, LANE), lambda s: (proj_idx(s)[2], 0)),
            pl.BlockSpec((1, GLA_HEADS, GLA_DK, GLA_DV), req_blk),
            pl.BlockSpec((1, RET_HEADS, RET_DK, RET_DV), req_blk),
            pl.BlockSpec((1, 1, ROW_W), req_row),
            *cast_specs(),
        ],
        out_specs=[
            pl.BlockSpec((1, T_BLK, GW), out_map),
            pl.BlockSpec((1, 1, GLA_DK, GLA_DV), state_map),
            pl.BlockSpec((1, RET_PER_GROUP, RET_DK, RET_DV), state_map),
            pl.BlockSpec((1, GLA_HEADS, GLA_DK, GLA_DV), req_blk),
            pl.BlockSpec((1, RET_HEADS, RET_DK, RET_DV), req_blk),
            pl.BlockSpec((1, 1, O_W), req_row),
            *cast_specs(),
        ],
        out_shape=[
            jax.ShapeDtypeStruct((B, L, D_MODEL), BF16),
            jax.ShapeDtypeStruct((B, GLA_HEADS, GLA_DK, GLA_DV), F32),
            jax.ShapeDtypeStruct((B, RET_HEADS, RET_DK, RET_DV), F32),
            jax.ShapeDtypeStruct(sg.shape, F32),
            jax.ShapeDtypeStruct(sr.shape, F32),
            jax.ShapeDtypeStruct((n_req, 1, O_W), F32),
            *[jax.ShapeDtypeStruct(a.shape, BF16) for a in to_cast],
        ],
        scratch_shapes=[pltpu.VMEM((GLA_DV, GLA_DK), F32)] + z_scratch + z_scratch,
        compiler_params=pltpu.CompilerParams(
            dimension_semantics=("arbitrary",), vmem_limit_bytes=VMEM_LIMIT),
        name="mix_prompt",
    )(lg, u, r, *([w_t] * N_W), wup, bg, gn, cos, sin, sg, sr, rows, *to_cast)


GATE_PIECES = ("ga", "ma", "gb", "mb")


def _merge_sample(o_ref, gates, gn):
    ga_ref, ma_ref, gb_ref, mb_ref = gates
    merged = []
    for h in range(GLA_HEADS):
        sl = slice(h * GLA_DV, (h + 1) * GLA_DV)
        g_a = ga_ref[:, sl]
        part_a = _sigmoid(ma_ref[:, sl]) * (_rms(o_ref[:, 0, sl]) * gn * (g_a * _sigmoid(g_a)))
        parts_b = []
        for j in range(RET_PER_GROUP):
            sb = slice(h * GW + j * RET_DV, h * GW + (j + 1) * RET_DV)
            g_b = gb_ref[:, sb]
            o_b = o_ref[:, 0, GLA_V + sb.start:GLA_V + sb.stop]
            parts_b.append(_sigmoid(mb_ref[:, sb]) * (_rms(o_b) * (g_b * _sigmoid(g_b))))
        merged.append(part_a + jnp.concatenate(parts_b, axis=-1))
    return jnp.concatenate(merged, axis=-1)


def _out_kernel(x_ref, mg_ref, p_ref, xs_ref, os_ref, ps_ref, *refs, final_norm):
    gates = refs[:len(GATE_PIECES)]
    gn_ref, wout_ref, nple_ref, wpg_ref, wpp_ref, nfin_ref, y_ref, ys_ref = refs[len(GATE_PIECES):]
    i = pl.program_id(0)
    prompt_tiles = pl.num_programs(0) - 1

    def tile(x, merged, p):
        h = x + _dot(merged, wout_ref[...])
        hn = (_rms(h) * nple_ref[...]).astype(BF16)
        gate = _sigmoid(_dot(hn, wpg_ref[...]))
        h = h + gate * _dot(p.astype(BF16), wpp_ref[...])
        if final_norm:
            h = _rms(h) * nfin_ref[...]
        return h

    @pl.when(i < prompt_tiles)
    def _():
        y_ref[...] = tile(x_ref[...], mg_ref[...], p_ref[...])

    @pl.when(i == prompt_tiles)
    def _():
        merged = _merge_sample(os_ref, gates, gn_ref[...]).astype(BF16)
        ys_ref[:, 0, :] = tile(xs_ref[:, 0, :], merged, ps_ref[:, 0, :])


def _out_proj(x, merged, p, xs, o_s, p_s, z_raw, gn, w_out, nple, w_pg, w_pp, nfin, final_norm):
    n, D = x.shape
    tm = min(TM_OUT, n)
    nt = n // tm
    const = lambda i: (0, 0)
    once = dict(pipeline_mode=pl.Buffered(1))

    def token_spec(a):
        return pl.BlockSpec((tm, a.shape[1]), lambda i: (jnp.minimum(i, nt - 1), 0))

    def sample_spec(a):
        return pl.BlockSpec(a.shape, lambda i: (0, 0, 0), **once)

    def gate_spec(name):
        start, width = W_PIECES[name]
        assert start % D == 0 and N_GROUPS * width == D
        return pl.BlockSpec((z_raw.shape[0], D), lambda i: (0, start // D), **once)

    return pl.pallas_call(
        functools.partial(_out_kernel, final_norm=final_norm),
        grid=(nt + 1,),
        in_specs=[
            token_spec(x),
            token_spec(merged),
            token_spec(p),
            sample_spec(xs),
            sample_spec(o_s),
            sample_spec(p_s),
            *[gate_spec(name) for name in GATE_PIECES],
            pl.BlockSpec((1, GLA_DV), const),
            pl.BlockSpec((D, D), const, **once),
            pl.BlockSpec((1, D), const),
            pl.BlockSpec((D, D), const, **once),
            pl.BlockSpec((PLE_DIM, D), const, **once),
            pl.BlockSpec((1, D), const),
        ],
        out_specs=[token_spec(x), pl.BlockSpec(xs.shape, lambda i: (0, 0, 0))],
        out_shape=[jax.ShapeDtypeStruct(x.shape, F32), jax.ShapeDtypeStruct(xs.shape, F32)],
        compiler_params=pltpu.CompilerParams(
            dimension_semantics=("arbitrary",), vmem_limit_bytes=VMEM_LIMIT),
        name="out_proj",
    )(x, merged, p, xs, o_s, p_s, *([z_raw] * len(GATE_PIECES)), gn, w_out, nple, w_pg, w_pp, nfin)


def _prep_kernel(x_ref, nmix_ref, w_ref, wr_ref, wb_ref, z_ref, r_ref, u_ref):
    @pl.when(pl.program_id(0) == 0)
    def _():
        u = (_rms(_tokens(x_ref)) * nmix_ref[...]).astype(BF16)
        u_ref[...] = u
        r_ref[...] = _gate_code(u, wr_ref)

    wb = w_ref[...].astype(BF16)
    wb_ref[...] = wb.T
    z_ref[...] = _dot_nt(u_ref[...], wb)


def _prep_weights(x, nmix, w_f32):
    n, D = x.shape[0], x.shape[-1]
    lo_chunks = R_START // PREP_ROWS
    assert R_START % PREP_ROWS == 0 and N_PACK % PREP_ROWS == 0 and HI_START % SUBLANE == 0

    def rows(k):
        skip = jnp.where(k >= lo_chunks, GLA_RANK // SUBLANE, 0)
        return (SUBLANE * ((PREP_ROWS // SUBLANE) * k + skip), 0)

    return pl.pallas_call(
        _prep_kernel,
        grid=(N_PACK // PREP_ROWS,),
        in_specs=[
            pl.BlockSpec(x.shape, lambda k: (0,) * x.ndim),
            pl.BlockSpec((1, D), lambda k: (0, 0)),
            pl.BlockSpec((pl.Element(PREP_ROWS), pl.Element(D)), rows),
            _gate_code_spec(),
        ],
        out_specs=[
            pl.BlockSpec((D, PREP_ROWS), lambda k: (0, k)),
            pl.BlockSpec((n, PREP_ROWS), lambda k: (0, k)),
            pl.BlockSpec((n, LANE), lambda k: (0, 0)),
        ],
        out_shape=[
            jax.ShapeDtypeStruct((D, N_PACK), BF16),
            jax.ShapeDtypeStruct((n, N_PACK), F32),
            jax.ShapeDtypeStruct((n, LANE), F32),
        ],
        scratch_shapes=[pltpu.VMEM((n, D), BF16)],
        compiler_params=pltpu.CompilerParams(
            dimension_semantics=("arbitrary",), vmem_limit_bytes=VMEM_LIMIT),
        name="prep_weights",
    )(x, nmix, w_f32, w_f32)


def _sample_transform_kernel(z_ref, r_ref, wup_ref, bg_ref, cos_ref, sin_ref, row_ref):
    def piece(name):
        start, width = W_PIECES[name]
        return z_ref[:, start:start + N_GROUPS * width]

    def put(name, value, h=0):
        off = ROW_OFFS[name] + h * value.shape[-1]
        row_ref[:, 0, off:off + value.shape[-1]] = value

    r = r_ref[...]
    for g in range(N_GROUPS):
        put("dec", jnp.exp(_gla_log_alpha(r, wup_ref[g], bg_ref[g])), g)
    put("qa", piece("qa") * (GLA_DK ** -0.5))
    for name in ("ka", "va", "vb"):
        put(name, piece(name))
    cos = cos_ref[...]
    sin = sin_ref[...]
    q_b = piece("qb")
    k_b = piece("kb")
    for h in range(RET_HEADS):
        sl = slice(h * RET_DK, (h + 1) * RET_DK)
        put("qb", _rotary(q_b[:, sl], cos, sin), h)
        put("kb", _rotary(k_b[:, sl], cos, sin) * (RET_DK ** -0.5), h)


def _sample_transform(z, r, wup, bg, cos, sin):
    n = z.shape[0]
    return pl.pallas_call(
        _sample_transform_kernel,
        out_shape=jax.ShapeDtypeStruct((n, 1, ROW_W), F32),
        compiler_params=pltpu.CompilerParams(vmem_limit_bytes=VMEM_LIMIT),
        name="sample_transform",
    )(z, r, wup, bg, cos, sin)


def _rope_tables(pos):
    half = RET_DK // 2
    inv = 1.0 / (ROPE_BASE ** jnp.linspace(0.0, 1.0, half, dtype=jnp.float32))
    ang = pos[:, None] * inv[None, :]
    return jnp.cos(ang), jnp.sin(ang)


def kernel(x_prompt, x_sample, state_gla, state_ret, p_prompt, p_sample, norm_mix, w_in, w_gla_up, b_gla,
           gla_norm, w_out, norm_ple, w_ple_gate, w_ple_proj, norm_final):
    depth = w_in.shape[0]
    Bp, Lp, D = x_prompt.shape
    Bs, Ls, _ = x_sample.shape
    assert Ls == 1 and Lp % T_BLK == 0
    cos_p, sin_p = _rope_tables(jnp.arange(Lp, dtype=jnp.float32))
    cos_s, sin_s = _rope_tables(PAST_LEN + jnp.arange(Ls, dtype=jnp.float32))
    log_gamma = jnp.log(1.0 - jnp.exp2(-5.0 - jnp.arange(RET_HEADS, dtype=jnp.float32)))
    nfin = norm_final.reshape(1, D)

    hp = x_prompt
    hs = x_sample
    gla_p, ret_p, gla_s, ret_s = [], [], [], []
    for i in range(depth):
        last = i == depth - 1
        nmix = norm_mix[i].reshape(1, D)
        nple = norm_ple[i].reshape(1, D)
        gn = gla_norm[i].reshape(1, GLA_DV)
        w_f32 = w_in[i].T
        wup = jnp.pad(w_gla_up[i], ((0, LANE - GLA_RANK), (0, 0))).astype(BF16)
        wup = wup.reshape(LANE, GLA_HEADS, GLA_DK).transpose(1, 0, 2)
        bg = b_gla[i].reshape(GLA_HEADS, 1, GLA_DK)
        w_pp = w_ple_proj[i].astype(BF16)

        w_t, z_raw, r_s = _prep_weights(hs, nmix, w_f32)
        rows = _sample_transform(z_raw, r_s, wup, bg, cos_s, sin_s)
        u, r = _norm_prompt(hp.reshape(Bp * Lp, D), nmix, w_f32)
        merged, sg, sr, nsg, nsr, o_s, w_o, w_pg = _mix_prompt(
            u.reshape(Bp, Lp, D), r.reshape(Bp, Lp, LANE), log_gamma, w_t, wup, bg, gn, cos_p, sin_p,
            state_gla[i], state_ret[i], rows, [w_out[i], w_ple_gate[i]])
        hp, hs = _out_proj(hp.reshape(Bp * Lp, D), merged.reshape(Bp * Lp, D),
                           p_prompt[i].reshape(Bp * Lp, PLE_DIM), hs, o_s, p_sample[i], z_raw, gn,
                           w_o, nple, w_pg, w_pp, nfin, last)
        hp = hp.reshape(Bp, Lp, D)
        gla_p.append(sg)
        ret_p.append(sr)
        gla_s.append(nsg)
        ret_s.append(nsr)

    return (hp, hs, jnp.stack(gla_p), jnp.stack(ret_p), jnp.stack(gla_s), jnp.stack(ret_s))
```

```python
import functools
import itertools

import jax
import jax.numpy as jnp
import numpy as np
from jax import lax
from jax.experimental import pallas as pl
from jax.experimental.pallas import tpu as pltpu

F32 = jnp.float32
BF16 = jnp.bfloat16

D_MODEL = 2048
PAST_LEN = 16384
PLE_DIM = 256
GLA_HEADS = 4
GLA_DK = 256
GLA_DV = 512
GLA_RANK = 16
GLA_TAU = 16.0
RET_HEADS = 8
RET_DK = 256
RET_DV = 256
ROPE_BASE = 10000.0
EPS = 1e-6

GLA_QK = GLA_HEADS * GLA_DK
GLA_V = GLA_HEADS * GLA_DV
RET_QK = RET_HEADS * RET_DK
RET_V = RET_HEADS * RET_DV
IN_SPLITS = (GLA_QK, GLA_QK, GLA_V, GLA_V, GLA_RANK, RET_QK, RET_QK, RET_V, RET_V, D_MODEL, D_MODEL)
IN_OFFS = tuple(int(v) for v in np.concatenate([[0], np.cumsum(IN_SPLITS)[:-1]]))

N_GROUPS = GLA_HEADS
RET_PER_GROUP = RET_HEADS // N_GROUPS
GW = GLA_DV
LANE = 128
SUBLANE = 8
ROW_TILE_BF16 = 16

R_START = IN_OFFS[4]
HI_START = IN_OFFS[5]
N_PACK = HI_START - GLA_RANK + sum(IN_SPLITS[5:])
W_PIECES = {
    "qa": (IN_OFFS[0], GLA_DK),
    "ka": (IN_OFFS[1], GLA_DK),
    "va": (IN_OFFS[2], GLA_DV),
    "ga": (IN_OFFS[3], GLA_DV),
    "qb": (IN_OFFS[5] - GLA_RANK, GW),
    "kb": (IN_OFFS[6] - GLA_RANK, GW),
    "vb": (IN_OFFS[7] - GLA_RANK, GW),
    "gb": (IN_OFFS[8] - GLA_RANK, GW),
    "ma": (IN_OFFS[9] - GLA_RANK, GW),
    "mb": (IN_OFFS[10] - GLA_RANK, GW),
}
W_NAMES = tuple(W_PIECES)

T_BLK = 256
C_GLA = 128
H_GLA = C_GLA // 2
N_CHUNKS = T_BLK // C_GLA
TM_OUT = 512
TM_NORM = 1024
PREP_ROWS = 1024
VMEM_LIMIT = 56 * 1024 * 1024


def _rms(x):
    return x * lax.rsqrt(jnp.mean(x * x, axis=-1, keepdims=True) + EPS)


def _sigmoid(x):
    return 1.0 / (1.0 + jnp.exp(-x))


def _log_sigmoid(x):
    return jnp.minimum(x, 0.0) - jnp.log(1.0 + jnp.exp(-jnp.abs(x)))


def _tokens(ref):
    return ref[...] if len(ref.shape) == 2 else ref[:, 0, :]


def _dot(a, b):
    return jnp.dot(a, b, preferred_element_type=F32)


def _dot_nt(a, b):
    return lax.dot_general(a, b, (((1,), (1,)), ((), ())), preferred_element_type=F32)


def _dot_tn(a, b):
    return lax.dot_general(a, b, (((0,), (0,)), ((), ())), preferred_element_type=F32)


def _rotary(x, cos, sin):
    half = x.shape[-1] // 2
    x1, x2 = x[:, :half], x[:, half:]
    return jnp.concatenate([x1 * cos - x2 * sin, x1 * sin + x2 * cos], axis=-1)


def _gla_log_alpha(r, wup, bg):
    pre = _dot(r.astype(BF16), wup) + bg
    return _log_sigmoid(pre) * (1.0 / GLA_TAU)


def _weight_specs(group_of, **kw):
    specs = []
    for n in W_NAMES:
        start, width = W_PIECES[n]
        assert start % width == 0

        def index_map(*idx, first=start // width):
            return (0, first + group_of(*idx))

        specs.append(pl.BlockSpec((D_MODEL, width), index_map, **kw))
    return specs


def _gate_code_spec():
    return pl.BlockSpec((pl.Element(GLA_RANK), pl.Element(D_MODEL)), lambda *idx: (R_START, 0))


def _gate_code(u, wr_ref):
    wr = jnp.concatenate([wr_ref[...].astype(BF16), jnp.zeros((LANE - GLA_RANK, D_MODEL), BF16)], axis=0)
    return _dot_nt(u, wr)


N_W = len(W_NAMES)


Z_BUFFERS = (
    ("qx", (T_BLK, GLA_DK), BF16),
    ("kx", (T_BLK, GLA_DK), BF16),
    ("qc", (N_CHUNKS, H_GLA, GLA_DK), BF16),
    ("kc", (N_CHUNKS, H_GLA, GLA_DK), BF16),
    ("qi", (T_BLK, GLA_DK), BF16),
    ("ko", (T_BLK, GLA_DK), BF16),
    ("va", (T_BLK, GLA_DV), BF16),
    ("dl", (SUBLANE, GLA_DK), F32),
    ("gta", (T_BLK, GW), F32),
    ("qb", (RET_PER_GROUP, T_BLK, RET_DK), BF16),
    ("kb", (RET_PER_GROUP, T_BLK, RET_DK), BF16),
    ("qbi", (RET_PER_GROUP, T_BLK, RET_DK), BF16),
    ("kbo", (RET_PER_GROUP, T_BLK, RET_DK), BF16),
    ("vb", (RET_PER_GROUP, T_BLK, RET_DV), BF16),
    ("gtb", (T_BLK, GW), F32),
)
Z_NAMES = tuple(n for n, _, _ in Z_BUFFERS)
assert N_CHUNKS <= SUBLANE


VT_DEC = 0
VT_KA = VT_DEC + GLA_HEADS
VT_QA = VT_KA + GLA_HEADS
VT_KB = VT_QA + GLA_HEADS
VT_QB = VT_KB + RET_HEADS
VT_N = VT_QB + RET_HEADS
VT_PAD = -(-VT_N // SUBLANE) * SUBLANE


ROW_PIECES = (("dec", GLA_HEADS, GLA_DK), ("ka", GLA_HEADS, GLA_DK), ("qa", GLA_HEADS, GLA_DK),
              ("kb", RET_HEADS, RET_DK), ("qb", RET_HEADS, RET_DK), ("va", GLA_HEADS, GLA_DV),
              ("vb", RET_HEADS, RET_DV))
ROW_OFFS = dict(zip([n for n, _, _ in ROW_PIECES],
                    np.concatenate([[0], np.cumsum([h * w for _, h, w in ROW_PIECES])[:-1]]).tolist()))
ROW_W = sum(h * w for _, h, w in ROW_PIECES)
O_W = GLA_V + RET_V


def _state_update(lg_ref, sg_ref, sr_ref, row_ref, nsg_ref, nsr_ref, o_ref):
    def vec(name, h, width):
        off = ROW_OFFS[name] + h * width
        return row_ref[0, :, off:off + width]

    rows = [vec(name, h, GLA_DK) for name, heads, _ in ROW_PIECES[:5] for h in range(heads)]
    rows.append(jnp.zeros((VT_PAD - VT_N, GLA_DK), F32))
    vt = jnp.concatenate(rows, axis=0).T

    def col(i):
        return vt[:, i:i + 1]

    for h in range(GLA_HEADS):
        s_new = col(VT_DEC + h) * sg_ref[0, h] + col(VT_KA + h) * vec("va", h, GLA_DV)
        nsg_ref[0, h] = s_new
        o_ref[0, :, h * GLA_DV:(h + 1) * GLA_DV] = jnp.sum(col(VT_QA + h) * s_new, axis=0, keepdims=True)
        yield
    for h in range(RET_HEADS):
        gamma = jnp.exp(jnp.full((1, RET_DV), lg_ref[h], F32))
        s_new = gamma * sr_ref[0, h] + col(VT_KB + h) * vec("vb", h, RET_DV)
        nsr_ref[0, h] = s_new
        o_ref[0, :, GLA_V + h * RET_DV:GLA_V + (h + 1) * RET_DV] = jnp.sum(
            col(VT_QB + h) * s_new, axis=0, keepdims=True)
        yield


N_SAMPLE_IN = 3
N_SAMPLE_OUT = 3
N_CAST = 2


def _project_block(u_ref, r_ref, w, wup_ref, bg_ref, cos_ref, sin_ref, lg_ref, g, z):
    T = T_BLK
    u = u_ref[0]

    def proj(name, j=0, n=None):
        ref = w[name]
        n = ref.shape[1] if n is None else n
        return _dot(u, ref[:, j * n:(j + 1) * n])

    la = _gla_log_alpha(r_ref[0], wup_ref[0], bg_ref[0])
    yield
    q = proj("qa") * (GLA_DK ** -0.5)
    yield
    k = proj("ka")
    ri = lax.broadcasted_iota(jnp.int32, (C_GLA, C_GLA), 0)
    ci = lax.broadcasted_iota(jnp.int32, (C_GLA, C_GLA), 1)
    tri = jnp.where(ri >= ci, 1.0, 0.0).astype(BF16)
    for c in range(N_CHUNKS):
        sl = slice(c * C_GLA, (c + 1) * C_GLA)
        la_c = la[sl]
        la_hi = la_c.astype(BF16)
        la_lo = (la_c - la_hi.astype(F32)).astype(BF16)
        b = _dot(tri, la_hi) + _dot(tri, la_lo)
        b_last = b[C_GLA - 1:C_GLA]
        q_c, k_c = q[sl], k[sl]
        z["qi"][sl] = (q_c * jnp.exp(b)).astype(BF16)
        z["ko"][sl] = (k_c * jnp.exp(b_last - b)).astype(BF16)
        z["dl"][c:c + 1] = jnp.exp(b_last)
        for h in range(2):
            hs = slice(h * H_GLA, (h + 1) * H_GLA)
            rows = slice(c * C_GLA + h * H_GLA, c * C_GLA + (h + 1) * H_GLA)
            b_h = b[hs]
            b_mid = b_h[H_GLA // 2 - 1:H_GLA // 2]
            z["qx"][rows] = (q_c[hs] * jnp.exp(b_h - b_mid)).astype(BF16)
            z["kx"][rows] = (k_c[hs] * jnp.exp(b_mid - b_h)).astype(BF16)
        b_edge = b[H_GLA - 1:H_GLA]
        z["qc"][c] = (q_c[H_GLA:] * jnp.exp(b[H_GLA:] - b_edge)).astype(BF16)
        z["kc"][c] = (k_c[:H_GLA] * jnp.exp(b_edge - b[:H_GLA])).astype(BF16)
    yield
    z["va"][...] = proj("va").astype(BF16)
    yield
    g_a = proj("ga")
    silu_a = g_a * _sigmoid(g_a)
    yield
    z["gta"][...] = _sigmoid(proj("ma")) * silu_a
    yield

    cos = cos_ref[...]
    sin = sin_ref[...]
    tl = lax.broadcasted_iota(jnp.int32, (T, LANE), 0).astype(F32)
    for j in range(RET_PER_GROUP):
        lg = lg_ref[g * RET_PER_GROUP + j]
        q_b = _rotary(proj("qb", j, RET_DK), cos, sin)
        k_b = _rotary(proj("kb", j, RET_DK), cos, sin) * (RET_DK ** -0.5)
        dec_in = jnp.exp((tl + 1.0) * lg)
        dec_out = jnp.exp((T - 1.0 - tl) * lg)
        z["qb"][j] = q_b.astype(BF16)
        z["kb"][j] = k_b.astype(BF16)
        z["qbi"][j] = (q_b * jnp.concatenate([dec_in, dec_in], axis=-1)).astype(BF16)
        z["kbo"][j] = (k_b * jnp.concatenate([dec_out, dec_out], axis=-1)).astype(BF16)
        yield
        z["vb"][j] = proj("vb", j, RET_DV).astype(BF16)
        g_b = proj("gb", j, RET_DV)
        yield
        z["gtb"][:, j * RET_DV:(j + 1) * RET_DV] = _sigmoid(proj("mb", j, RET_DV)) * (g_b * _sigmoid(g_b))
        yield


def _recur_block(z, lg_ref, g, gn_ref, sgt_ref, sret_ref, merged_ref):
    T = T_BLK
    ri = lax.broadcasted_iota(jnp.int32, (H_GLA, H_GLA), 0)
    ci = lax.broadcasted_iota(jnp.int32, (H_GLA, H_GLA), 1)
    causal = ri >= ci
    st = sgt_ref[...]
    o_chunks = []
    for c in range(N_CHUNKS):
        sl = slice(c * C_GLA, (c + 1) * C_GLA)
        v_c = z["va"][sl]
        diag = []
        for h in range(2):
            rows = slice(c * C_GLA + h * H_GLA, c * C_GLA + (h + 1) * H_GLA)
            diag.append(jnp.where(causal, _dot_nt(z["qx"][rows], z["kx"][rows]), 0.0))
        cross = _dot_nt(z["qc"][c], z["kc"][c])
        a = jnp.concatenate([jnp.concatenate([diag[0], jnp.zeros_like(cross)], axis=1),
                             jnp.concatenate([cross, diag[1]], axis=1)], axis=0).astype(BF16)
        o_chunks.append(_dot_nt(z["qi"][sl], st.astype(BF16)) + _dot(a, v_c))
        st = st * z["dl"][c:c + 1] + _dot_tn(v_c, z["ko"][sl])
        yield
    sgt_ref[...] = st
    part_a = z["gta"][...] * (_rms(jnp.concatenate(o_chunks, axis=0)) * gn_ref[...])
    yield

    rt = lax.broadcasted_iota(jnp.int32, (T, T), 0)
    ct = lax.broadcasted_iota(jnp.int32, (T, T), 1)
    parts_b = []
    for j in range(RET_PER_GROUP):
        lg = lg_ref[g * RET_PER_GROUP + j]
        v_b = z["vb"][j]
        dmat = jnp.where(rt >= ct, jnp.exp((rt - ct).astype(F32) * lg), 0.0)
        a = (_dot_nt(z["qb"][j], z["kb"][j]) * dmat).astype(BF16)
        s = sret_ref[0, j]
        o_b = _rms(_dot(z["qbi"][j], s.astype(BF16)) + _dot(a, v_b))
        dec_all = jnp.exp(jnp.full((1, RET_DV), T * lg, F32))
        sret_ref[0, j] = s * dec_all + _dot_tn(z["kbo"][j], v_b)
        parts_b.append(z["gtb"][:, j * RET_DV:(j + 1) * RET_DV] * o_b)
        yield
    merged_ref[0] = (part_a + jnp.concatenate(parts_b, axis=-1)).astype(merged_ref.dtype)


def _mix_prompt_kernel(lg_ref, u_ref, r_ref, *refs, blocks_per_group, blocks_per_seq):
    nw, nz = len(W_NAMES), len(Z_NAMES)
    w = dict(zip(W_NAMES, refs[:nw]))
    refs = list(refs[nw:])
    wup_ref, bg_ref, gn_ref, cos_ref, sin_ref = refs[:5]
    sample_in = refs[5:5 + N_SAMPLE_IN]
    cast_in = refs[5 + N_SAMPLE_IN:5 + N_SAMPLE_IN + N_CAST]
    refs = refs[5 + N_SAMPLE_IN + N_CAST:]
    merged_ref, sgla_ref, sret_ref = refs[:3]
    sample_out = refs[3:3 + N_SAMPLE_OUT]
    cast_out = refs[3 + N_SAMPLE_OUT:3 + N_SAMPLE_OUT + N_CAST]
    sgt_ref = refs[3 + N_SAMPLE_OUT + N_CAST]
    refs = refs[4 + N_SAMPLE_OUT + N_CAST:]
    z_even = dict(zip(Z_NAMES, refs[:nz]))
    z_odd = dict(zip(Z_NAMES, refs[nz:]))
    s = pl.program_id(0)
    n_blocks = pl.num_programs(0) - 1
    g_p = jnp.minimum(s, n_blocks - 1) // blocks_per_group
    r = jnp.maximum(s - 1, 0)
    g_r = r // blocks_per_group
    t_r = r % blocks_per_seq

    @pl.when(s == 0)
    def _():
        for ref in z_odd.values():
            ref[...] = jnp.zeros_like(ref)

    @pl.when(t_r == 0)
    def _():
        sgt_ref[...] = jnp.zeros_like(sgt_ref)
        sret_ref[...] = jnp.zeros_like(sret_ref)

    for src, dst in zip(cast_in, cast_out):
        dst[...] = src[...].astype(dst.dtype)

    def step(z_write, z_read):
        rec = _recur_block(z_read, lg_ref, g_r, gn_ref, sgt_ref, sret_ref, merged_ref)
        prj = _project_block(u_ref, r_ref, w, wup_ref, bg_ref, cos_ref, sin_ref, lg_ref, g_p, z_write)
        upd = _state_update(lg_ref, *sample_in, *sample_out)
        for _ in itertools.zip_longest(prj, rec, upd):
            pass

    @pl.when(s % 2 == 0)
    def _():
        step(z_even, z_odd)

    @pl.when(s % 2 == 1)
    def _():
        step(z_odd, z_even)

    @pl.when((t_r == blocks_per_seq - 1) & (s > 0))
    def _():
        sgla_ref[0, 0] = sgt_ref[...].T


def _norm_kernel(x_ref, g_ref, wr_ref, u_ref, r_ref):
    u = (_rms(x_ref[...]) * g_ref[...]).astype(u_ref.dtype)
    u_ref[...] = u
    r_ref[...] = _gate_code(u, wr_ref)


def _norm_prompt(x, gain, w_f32):
    n, D = x.shape
    tm = min(TM_NORM, n)
    return pl.pallas_call(
        _norm_kernel,
        grid=(n // tm,),
        in_specs=[pl.BlockSpec((tm, D), lambda i: (i, 0)), pl.BlockSpec((1, D), lambda i: (0, 0)),
                  _gate_code_spec()],
        out_specs=[pl.BlockSpec((tm, D), lambda i: (i, 0)), pl.BlockSpec((tm, LANE), lambda i: (i, 0))],
        out_shape=[jax.ShapeDtypeStruct((n, D), BF16), jax.ShapeDtypeStruct((n, LANE), F32)],
        compiler_params=pltpu.CompilerParams(
            dimension_semantics=("arbitrary",), vmem_limit_bytes=VMEM_LIMIT),
        name="norm_prompt",
    )(x, gain, w_f32)


def _mix_prompt(u, r, lg, w_t, wup, bg, gn, cos, sin, sg, sr, rows, to_cast):
    B, L, D = u.shape
    nt = L // T_BLK
    n_blocks = N_GROUPS * B * nt
    n_req = sg.shape[0]
    assert n_req <= n_blocks + 1
    assert rows.shape == (n_req, 1, ROW_W) and len(to_cast) == N_CAST
    slab = ROW_TILE_BF16
    assert all(a.shape[0] <= slab * n_blocks and a.shape[0] % slab == 0 for a in to_cast)

    def cast_specs():
        return [pl.BlockSpec((slab, a.shape[1]), lambda s, n=a.shape[0] // slab: (jnp.minimum(s, n - 1), 0))
                for a in to_cast]

    def req_row(s):
        return (jnp.minimum(s, n_req - 1), 0, 0)

    def req_blk(s):
        return (jnp.minimum(s, n_req - 1), 0, 0, 0)

    def proj_idx(s):
        p = jnp.minimum(s, n_blocks - 1)
        return p // (B * nt), (p // nt) % B, p % nt

    def recur_idx(s):
        r = jnp.maximum(s - 1, 0)
        return r // (B * nt), (r // nt) % B, r % nt

    def out_map(s):
        g, b, t = recur_idx(s)
        return (b, t, g)

    def state_map(s):
        g, b, _ = recur_idx(s)
        return (b, g, 0, 0)

    z_scratch = [pltpu.VMEM(shape, dtype) for _, shape, dtype in Z_BUFFERS]
    return pl.pallas_call(
        functools.partial(_mix_prompt_kernel, blocks_per_group=B * nt, blocks_per_seq=nt),
        grid=(n_blocks + 1,),
        in_specs=[
            pl.BlockSpec(memory_space=pltpu.SMEM),
            pl.BlockSpec((1, T_BLK, D), lambda s: (proj_idx(s)[1], proj_idx(s)[2], 0)),
            pl.BlockSpec((1, T_BLK, LANE), lambda s: (proj_idx(s)[1], proj_idx(s)[2], 0)),
            *_weight_specs(lambda s: proj_idx(s)[0], pipeline_mode=pl.Buffered(1)),
            pl.BlockSpec((1, LANE, GLA_DK), lambda s: (proj_idx(s)[0], 0, 0)),
            pl.BlockSpec((1, 1, GLA_DK), lambda s: (proj_idx(s)[0], 0, 0)),
            pl.BlockSpec((1, GLA_DV), lambda s: (0, 0)),
            pl.BlockSpec((T_BLK, LANE), lambda s: (proj_idx(s)[2], 0)),
            pl.BlockSpec((T_BLK, LANE), lambda s: (proj_idx(s)[2], 0)),
            pl.BlockSpec((1, GLA_HEADS, GLA_DK, GLA_DV), req_blk),
            pl.BlockSpec((1, RET_HEADS, RET_DK, RET_DV), req_blk),
            pl.BlockSpec((1, 1, ROW_W), req_row),
            *cast_specs(),
        ],
        out_specs=[
            pl.BlockSpec((1, T_BLK, GW), out_map),
            pl.BlockSpec((1, 1, GLA_DK, GLA_DV), state_map),
            pl.BlockSpec((1, RET_PER_GROUP, RET_DK, RET_DV), state_map),
            pl.BlockSpec((1, GLA_HEADS, GLA_DK, GLA_DV), req_blk),
            pl.BlockSpec((1, RET_HEADS, RET_DK, RET_DV), req_blk),
            pl.BlockSpec((1, 1, O_W), req_row),
            *cast_specs(),
        ],
        out_shape=[
            jax.ShapeDtypeStruct((B, L, D_MODEL), BF16),
            jax.ShapeDtypeStruct((B, GLA_HEADS, GLA_DK, GLA_DV), F32),
            jax.ShapeDtypeStruct((B, RET_HEADS, RET_DK, RET_DV), F32),
            jax.ShapeDtypeStruct(sg.shape, F32),
            jax.ShapeDtypeStruct(sr.shape, F32),
            jax.ShapeDtypeStruct((n_req, 1, O_W), F32),
            *[jax.ShapeDtypeStruct(a.shape, BF16) for a in to_cast],
        ],
        scratch_shapes=[pltpu.VMEM((GLA_DV, GLA_DK), F32)] + z_scratch + z_scratch,
        compiler_params=pltpu.CompilerParams(
            dimension_semantics=("arbitrary",), vmem_limit_bytes=VMEM_LIMIT),
        name="mix_prompt",
    )(lg, u, r, *([w_t] * N_W), wup, bg, gn, cos, sin, sg, sr, rows, *to_cast)


GATE_PIECES = ("ga", "ma", "gb", "mb")


def _merge_sample(o_ref, gates, gn):
    ga_ref, ma_ref, gb_ref, mb_ref = gates
    merged = []
    for h in range(GLA_HEADS):
        sl = slice(h * GLA_DV, (h + 1) * GLA_DV)
        g_a = ga_ref[:, sl]
        part_a = _sigmoid(ma_ref[:, sl]) * (_rms(o_ref[:, 0, sl]) * gn * (g_a * _sigmoid(g_a)))
        parts_b = []
        for j in range(RET_PER_GROUP):
            sb = slice(h * GW + j * RET_DV, h * GW + (j + 1) * RET_DV)
            g_b = gb_ref[:, sb]
            o_b = o_ref[:, 0, GLA_V + sb.start:GLA_V + sb.stop]
            parts_b.append(_sigmoid(mb_ref[:, sb]) * (_rms(o_b) * (g_b * _sigmoid(g_b))))
        merged.append(part_a + jnp.concatenate(parts_b, axis=-1))
    return jnp.concatenate(merged, axis=-1)


def _out_kernel(x_ref, mg_ref, p_ref, xs_ref, os_ref, ps_ref, *refs, final_norm):
    gates = refs[:len(GATE_PIECES)]
    gn_ref, wout_ref, nple_ref, wpg_ref, wpp_ref, nfin_ref, y_ref, ys_ref = refs[len(GATE_PIECES):]
    i = pl.program_id(0)
    prompt_tiles = pl.num_programs(0) - 1

    def tile(x, merged, p):
        h = x + _dot(merged, wout_ref[...])
        hn = (_rms(h) * nple_ref[...]).astype(BF16)
        gate = _sigmoid(_dot(hn, wpg_ref[...]))
        h = h + gate * _dot(p.astype(BF16), wpp_ref[...])
        if final_norm:
            h = _rms(h) * nfin_ref[...]
        return h

    @pl.when(i < prompt_tiles)
    def _():
        y_ref[...] = tile(x_ref[...], mg_ref[...], p_ref[...])

    @pl.when(i == prompt_tiles)
    def _():
        merged = _merge_sample(os_ref, gates, gn_ref[...]).astype(BF16)
        ys_ref[:, 0, :] = tile(xs_ref[:, 0, :], merged, ps_ref[:, 0, :])


def _out_proj(x, merged, p, xs, o_s, p_s, z_raw, gn, w_out, nple, w_pg, w_pp, nfin, final_norm):
    n, D = x.shape
    tm = min(TM_OUT, n)
    nt = n // tm
    const = lambda i: (0, 0)
    once = dict(pipeline_mode=pl.Buffered(1))

    def token_spec(a):
        return pl.BlockSpec((tm, a.shape[1]), lambda i: (jnp.minimum(i, nt - 1), 0))

    def sample_spec(a):
        return pl.BlockSpec(a.shape, lambda i: (0, 0, 0), **once)

    def gate_spec(name):
        start, width = W_PIECES[name]
        assert start % D == 0 and N_GROUPS * width == D
        return pl.BlockSpec((z_raw.shape[0], D), lambda i: (0, start // D), **once)

    return pl.pallas_call(
        functools.partial(_out_kernel, final_norm=final_norm),
        grid=(nt + 1,),
        in_specs=[
            token_spec(x),
            token_spec(merged),
            token_spec(p),
            sample_spec(xs),
            sample_spec(o_s),
            sample_spec(p_s),
            *[gate_spec(name) for name in GATE_PIECES],
            pl.BlockSpec((1, GLA_DV), const),
            pl.BlockSpec((D, D), const, **once),
            pl.BlockSpec((1, D), const),
            pl.BlockSpec((D, D), const, **once),
            pl.BlockSpec((PLE_DIM, D), const, **once),
            pl.BlockSpec((1, D), const),
        ],
        out_specs=[token_spec(x), pl.BlockSpec(xs.shape, lambda i: (0, 0, 0))],
        out_shape=[jax.ShapeDtypeStruct(x.shape, F32), jax.ShapeDtypeStruct(xs.shape, F32)],
        compiler_params=pltpu.CompilerParams(
            dimension_semantics=("arbitrary",), vmem_limit_bytes=VMEM_LIMIT),
        name="out_proj",
    )(x, merged, p, xs, o_s, p_s, *([z_raw] * len(GATE_PIECES)), gn, w_out, nple, w_pg, w_pp, nfin)


def _sample_rows(zs_ref, r, wup_ref, bg_ref, cos_ref, sin_ref, row_ref):
    def piece(name):
        start, width = W_PIECES[name]
        assert start % PREP_ROWS == 0 and (N_GROUPS * width) % PREP_ROWS == 0
        first = start // PREP_ROWS
        return jnp.concatenate([zs_ref[c] for c in range(first, first + N_GROUPS * width // PREP_ROWS)], axis=-1)

    def put(name, value, h=0):
        off = ROW_OFFS[name] + h * value.shape[-1]
        row_ref[:, 0, off:off + value.shape[-1]] = value

    for g in range(N_GROUPS):
        put("dec", jnp.exp(_gla_log_alpha(r, wup_ref[g], bg_ref[g])), g)
    put("qa", piece("qa") * (GLA_DK ** -0.5))
    for name in ("ka", "va", "vb"):
        put(name, piece(name))
    cos = cos_ref[...]
    sin = sin_ref[...]
    q_b = piece("qb")
    k_b = piece("kb")
    for h in range(RET_HEADS):
        sl = slice(h * RET_DK, (h + 1) * RET_DK)
        put("qb", _rotary(q_b[:, sl], cos, sin), h)
        put("kb", _rotary(k_b[:, sl], cos, sin) * (RET_DK ** -0.5), h)


def _prep_kernel(x_ref, nmix_ref, w_ref, wr_ref, wup_ref, bg_ref, cos_ref, sin_ref,
                 wb_ref, z_ref, row_ref, u_ref, r_ref, zs_ref):
    k = pl.program_id(0)

    @pl.when(k == 0)
    def _():
        u = (_rms(_tokens(x_ref)) * nmix_ref[...]).astype(BF16)
        u_ref[...] = u
        r_ref[...] = _gate_code(u, wr_ref)

    wb = w_ref[...].astype(BF16)
    wb_ref[...] = wb.T
    z = _dot_nt(u_ref[...], wb)
    z_ref[...] = z
    zs_ref[k] = z

    @pl.when(k == pl.num_programs(0) - 1)
    def _():
        _sample_rows(zs_ref, r_ref[...], wup_ref, bg_ref, cos_ref, sin_ref, row_ref)


def _prep_weights(x, nmix, w_f32, wup, bg, cos, sin):
    n, D = x.shape[0], x.shape[-1]
    chunks = N_PACK // PREP_ROWS
    lo_chunks = R_START // PREP_ROWS
    assert R_START % PREP_ROWS == 0 and N_PACK % PREP_ROWS == 0 and HI_START % SUBLANE == 0

    def rows(k):
        skip = jnp.where(k >= lo_chunks, GLA_RANK // SUBLANE, 0)
        return (SUBLANE * ((PREP_ROWS // SUBLANE) * k + skip), 0)

    def whole(a):
        return pl.BlockSpec(a.shape, lambda k: (0,) * a.ndim)

    return pl.pallas_call(
        _prep_kernel,
        grid=(chunks,),
        in_specs=[
            whole(x),
            whole(nmix),
            pl.BlockSpec((pl.Element(PREP_ROWS), pl.Element(D)), rows),
            _gate_code_spec(),
            whole(wup),
            whole(bg),
            whole(cos),
            whole(sin),
        ],
        out_specs=[
            pl.BlockSpec((D, PREP_ROWS), lambda k: (0, k)),
            pl.BlockSpec((n, PREP_ROWS), lambda k: (0, k)),
            pl.BlockSpec((n, 1, ROW_W), lambda k: (0, 0, 0)),
        ],
        out_shape=[
            jax.ShapeDtypeStruct((D, N_PACK), BF16),
            jax.ShapeDtypeStruct((n, N_PACK), F32),
            jax.ShapeDtypeStruct((n, 1, ROW_W), F32),
        ],
        scratch_shapes=[pltpu.VMEM((n, D), BF16), pltpu.VMEM((n, LANE), F32),
                        pltpu.VMEM((chunks, n, PREP_ROWS), F32)],
        compiler_params=pltpu.CompilerParams(
            dimension_semantics=("arbitrary",), vmem_limit_bytes=VMEM_LIMIT),
        name="prep_weights",
    )(x, nmix, w_f32, w_f32, wup, bg, cos, sin)


def _rope_tables(pos):
    half = RET_DK // 2
    inv = 1.0 / (ROPE_BASE ** jnp.linspace(0.0, 1.0, half, dtype=jnp.float32))
    ang = pos[:, None] * inv[None, :]
    return jnp.cos(ang), jnp.sin(ang)


def kernel(x_prompt, x_sample, state_gla, state_ret, p_prompt, p_sample, norm_mix, w_in, w_gla_up, b_gla,
           gla_norm, w_out, norm_ple, w_ple_gate, w_ple_proj, norm_final):
    depth = w_in.shape[0]
    Bp, Lp, D = x_prompt.shape
    Bs, Ls, _ = x_sample.shape
    assert Ls == 1 and Lp % T_BLK == 0
    cos_p, sin_p = _rope_tables(jnp.arange(Lp, dtype=jnp.float32))
    cos_s, sin_s = _rope_tables(PAST_LEN + jnp.arange(Ls, dtype=jnp.float32))
    log_gamma = jnp.log(1.0 - jnp.exp2(-5.0 - jnp.arange(RET_HEADS, dtype=jnp.float32)))
    nfin = norm_final.reshape(1, D)

    hp = x_prompt
    hs = x_sample
    gla_p, ret_p, gla_s, ret_s = [], [], [], []
    for i in range(depth):
        last = i == depth - 1
        nmix = norm_mix[i].reshape(1, D)
        nple = norm_ple[i].reshape(1, D)
        gn = gla_norm[i].reshape(1, GLA_DV)
        w_f32 = w_in[i].T
        wup = jnp.pad(w_gla_up[i], ((0, LANE - GLA_RANK), (0, 0))).astype(BF16)
        wup = wup.reshape(LANE, GLA_HEADS, GLA_DK).transpose(1, 0, 2)
        bg = b_gla[i].reshape(GLA_HEADS, 1, GLA_DK)
        w_pp = w_ple_proj[i].astype(BF16)

        w_t, z_raw, rows = _prep_weights(hs, nmix, w_f32, wup, bg, cos_s, sin_s)
        u, r = _norm_prompt(hp.reshape(Bp * Lp, D), nmix, w_f32)
        merged, sg, sr, nsg, nsr, o_s, w_o, w_pg = _mix_prompt(
            u.reshape(Bp, Lp, D), r.reshape(Bp, Lp, LANE), log_gamma, w_t, wup, bg, gn, cos_p, sin_p,
            state_gla[i], state_ret[i], rows, [w_out[i], w_ple_gate[i]])
        hp, hs = _out_proj(hp.reshape(Bp * Lp, D), merged.reshape(Bp * Lp, D),
                           p_prompt[i].reshape(Bp * Lp, PLE_DIM), hs, o_s, p_sample[i], z_raw, gn,
                           w_o, nple, w_pg, w_pp, nfin, last)
        hp = hp.reshape(Bp, Lp, D)
        gla_p.append(sg)
        ret_p.append(sr)
        gla_s.append(nsg)
        ret_s.append(nsr)

    return (hp, hs, jnp.stack(gla_p), jnp.stack(ret_p), jnp.stack(gla_s), jnp.stack(ret_s))
```

```python
import functools
import itertools

import jax
import jax.numpy as jnp
import numpy as np
from jax import lax
from jax.experimental import pallas as pl
from jax.experimental.pallas import tpu as pltpu

F32 = jnp.float32
BF16 = jnp.bfloat16

D_MODEL = 2048
PAST_LEN = 16384
PLE_DIM = 256
GLA_HEADS = 4
GLA_DK = 256
GLA_DV = 512
GLA_RANK = 16
GLA_TAU = 16.0
RET_HEADS = 8
RET_DK = 256
RET_DV = 256
ROPE_BASE = 10000.0
EPS = 1e-6

GLA_QK = GLA_HEADS * GLA_DK
GLA_V = GLA_HEADS * GLA_DV
RET_QK = RET_HEADS * RET_DK
RET_V = RET_HEADS * RET_DV
IN_SPLITS = (GLA_QK, GLA_QK, GLA_V, GLA_V, GLA_RANK, RET_QK, RET_QK, RET_V, RET_V, D_MODEL, D_MODEL)
IN_OFFS = tuple(int(v) for v in np.concatenate([[0], np.cumsum(IN_SPLITS)[:-1]]))

N_GROUPS = GLA_HEADS
RET_PER_GROUP = RET_HEADS // N_GROUPS
GW = GLA_DV
LANE = 128
SUBLANE = 8
ROW_TILE_BF16 = 16

R_START = IN_OFFS[4]
HI_START = IN_OFFS[5]
N_PACK = HI_START - GLA_RANK + sum(IN_SPLITS[5:])
W_PIECES = {
    "qa": (IN_OFFS[0], GLA_DK),
    "ka": (IN_OFFS[1], GLA_DK),
    "va": (IN_OFFS[2], GLA_DV),
    "ga": (IN_OFFS[3], GLA_DV),
    "qb": (IN_OFFS[5] - GLA_RANK, GW),
    "kb": (IN_OFFS[6] - GLA_RANK, GW),
    "vb": (IN_OFFS[7] - GLA_RANK, GW),
    "gb": (IN_OFFS[8] - GLA_RANK, GW),
    "ma": (IN_OFFS[9] - GLA_RANK, GW),
    "mb": (IN_OFFS[10] - GLA_RANK, GW),
}
W_NAMES = tuple(W_PIECES)

T_BLK = 256
C_GLA = 128
H_GLA = C_GLA // 2
N_CHUNKS = T_BLK // C_GLA
TM_OUT = 512
TM_NORM = 1024
PREP_ROWS = 1024
VMEM_LIMIT = 56 * 1024 * 1024


def _rms(x):
    return x * lax.rsqrt(jnp.mean(x * x, axis=-1, keepdims=True) + EPS)


def _sigmoid(x):
    return 1.0 / (1.0 + jnp.exp(-x))


def _log_sigmoid(x):
    return jnp.minimum(x, 0.0) - jnp.log(1.0 + jnp.exp(-jnp.abs(x)))


def _tokens(ref):
    return ref[...] if len(ref.shape) == 2 else ref[:, 0, :]


def _dot(a, b):
    return jnp.dot(a, b, preferred_element_type=F32)


def _dot_nt(a, b):
    return lax.dot_general(a, b, (((1,), (1,)), ((), ())), preferred_element_type=F32)


def _dot_tn(a, b):
    return lax.dot_general(a, b, (((0,), (0,)), ((), ())), preferred_element_type=F32)


def _rotary(x, cos, sin):
    half = x.shape[-1] // 2
    x1, x2 = x[:, :half], x[:, half:]
    return jnp.concatenate([x1 * cos - x2 * sin, x1 * sin + x2 * cos], axis=-1)


def _gla_log_alpha(r, wup, bg):
    pre = _dot(r.astype(BF16), wup) + bg
    return _log_sigmoid(pre) * (1.0 / GLA_TAU)


def _weight_specs(group_of, **kw):
    specs = []
    for n in W_NAMES:
        start, width = W_PIECES[n]
        assert start % width == 0

        def index_map(*idx, first=start // width):
            return (0, first + group_of(*idx))

        specs.append(pl.BlockSpec((D_MODEL, width), index_map, **kw))
    return specs


def _gate_code_spec():
    return pl.BlockSpec((pl.Element(GLA_RANK), pl.Element(D_MODEL)), lambda *idx: (R_START, 0))


def _gate_code(u, wr_ref):
    wr = jnp.concatenate([wr_ref[...].astype(BF16), jnp.zeros((LANE - GLA_RANK, D_MODEL), BF16)], axis=0)
    return _dot_nt(u, wr)


N_W = len(W_NAMES)


Z_BUFFERS = (
    ("qx", (T_BLK, GLA_DK), BF16),
    ("kx", (T_BLK, GLA_DK), BF16),
    ("qc", (N_CHUNKS, H_GLA, GLA_DK), BF16),
    ("kc", (N_CHUNKS, H_GLA, GLA_DK), BF16),
    ("qi", (T_BLK, GLA_DK), BF16),
    ("ko", (T_BLK, GLA_DK), BF16),
    ("va", (T_BLK, GLA_DV), BF16),
    ("dl", (SUBLANE, GLA_DK), F32),
    ("gta", (T_BLK, GW), F32),
    ("qb", (RET_PER_GROUP, T_BLK, RET_DK), BF16),
    ("kb", (RET_PER_GROUP, T_BLK, RET_DK), BF16),
    ("qbi", (RET_PER_GROUP, T_BLK, RET_DK), BF16),
    ("kbo", (RET_PER_GROUP, T_BLK, RET_DK), BF16),
    ("vb", (RET_PER_GROUP, T_BLK, RET_DV), BF16),
    ("gtb", (T_BLK, GW), F32),
)
Z_NAMES = tuple(n for n, _, _ in Z_BUFFERS)
assert N_CHUNKS <= SUBLANE


VT_DEC = 0
VT_KA = VT_DEC + GLA_HEADS
VT_QA = VT_KA + GLA_HEADS
VT_KB = VT_QA + GLA_HEADS
VT_QB = VT_KB + RET_HEADS
VT_N = VT_QB + RET_HEADS
VT_PAD = -(-VT_N // SUBLANE) * SUBLANE


ROW_PIECES = (("dec", GLA_HEADS, GLA_DK), ("ka", GLA_HEADS, GLA_DK), ("qa", GLA_HEADS, GLA_DK),
              ("kb", RET_HEADS, RET_DK), ("qb", RET_HEADS, RET_DK), ("va", GLA_HEADS, GLA_DV),
              ("vb", RET_HEADS, RET_DV))
ROW_OFFS = dict(zip([n for n, _, _ in ROW_PIECES],
                    np.concatenate([[0], np.cumsum([h * w for _, h, w in ROW_PIECES])[:-1]]).tolist()))
ROW_W = sum(h * w for _, h, w in ROW_PIECES)
O_W = GLA_V + RET_V


def _state_update(lg_ref, sg_ref, sr_ref, row_ref, nsg_ref, nsr_ref, o_ref):
    def vec(name, h, width):
        off = ROW_OFFS[name] + h * width
        return row_ref[0, :, off:off + width]

    rows = [vec(name, h, GLA_DK) for name, heads, _ in ROW_PIECES[:5] for h in range(heads)]
    rows.append(jnp.zeros((VT_PAD - VT_N, GLA_DK), F32))
    vt = jnp.concatenate(rows, axis=0).T

    def col(i):
        return vt[:, i:i + 1]

    for h in range(GLA_HEADS):
        s_new = col(VT_DEC + h) * sg_ref[0, h] + col(VT_KA + h) * vec("va", h, GLA_DV)
        nsg_ref[0, h] = s_new
        o_ref[0, :, h * GLA_DV:(h + 1) * GLA_DV] = jnp.sum(col(VT_QA + h) * s_new, axis=0, keepdims=True)
        yield
    for h in range(RET_HEADS):
        gamma = jnp.exp(jnp.full((1, RET_DV), lg_ref[h], F32))
        s_new = gamma * sr_ref[0, h] + col(VT_KB + h) * vec("vb", h, RET_DV)
        nsr_ref[0, h] = s_new
        o_ref[0, :, GLA_V + h * RET_DV:GLA_V + (h + 1) * RET_DV] = jnp.sum(
            col(VT_QB + h) * s_new, axis=0, keepdims=True)
        yield


N_SAMPLE_IN = 3
N_SAMPLE_OUT = 3
N_CAST = 2


def _project_block(u_ref, la_ref, w, cos_ref, sin_ref, lg_ref, g, z):
    T = T_BLK
    u = u_ref[0]

    def proj(name, j=0, n=None):
        ref = w[name]
        n = ref.shape[1] if n is None else n
        return _dot(u, ref[:, j * n:(j + 1) * n])

    la = la_ref[0, 0]
    q = proj("qa") * (GLA_DK ** -0.5)
    yield
    k = proj("ka")
    ri = lax.broadcasted_iota(jnp.int32, (C_GLA, C_GLA), 0)
    ci = lax.broadcasted_iota(jnp.int32, (C_GLA, C_GLA), 1)
    tri = jnp.where(ri >= ci, 1.0, 0.0).astype(BF16)
    for c in range(N_CHUNKS):
        sl = slice(c * C_GLA, (c + 1) * C_GLA)
        la_c = la[sl]
        la_hi = la_c.astype(BF16)
        la_lo = (la_c - la_hi.astype(F32)).astype(BF16)
        b = _dot(tri, la_hi) + _dot(tri, la_lo)
        b_last = b[C_GLA - 1:C_GLA]
        q_c, k_c = q[sl], k[sl]
        z["qi"][sl] = (q_c * jnp.exp(b)).astype(BF16)
        z["ko"][sl] = (k_c * jnp.exp(b_last - b)).astype(BF16)
        z["dl"][c:c + 1] = jnp.exp(b_last)
        for h in range(2):
            hs = slice(h * H_GLA, (h + 1) * H_GLA)
            rows = slice(c * C_GLA + h * H_GLA, c * C_GLA + (h + 1) * H_GLA)
            b_h = b[hs]
            b_mid = b_h[H_GLA // 2 - 1:H_GLA // 2]
            z["qx"][rows] = (q_c[hs] * jnp.exp(b_h - b_mid)).astype(BF16)
            z["kx"][rows] = (k_c[hs] * jnp.exp(b_mid - b_h)).astype(BF16)
        b_edge = b[H_GLA - 1:H_GLA]
        z["qc"][c] = (q_c[H_GLA:] * jnp.exp(b[H_GLA:] - b_edge)).astype(BF16)
        z["kc"][c] = (k_c[:H_GLA] * jnp.exp(b_edge - b[:H_GLA])).astype(BF16)
    yield
    z["va"][...] = proj("va").astype(BF16)
    yield
    g_a = proj("ga")
    silu_a = g_a * _sigmoid(g_a)
    yield
    z["gta"][...] = _sigmoid(proj("ma")) * silu_a
    yield

    cos = cos_ref[...]
    sin = sin_ref[...]
    tl = lax.broadcasted_iota(jnp.int32, (T, LANE), 0).astype(F32)
    for j in range(RET_PER_GROUP):
        lg = lg_ref[g * RET_PER_GROUP + j]
        q_b = _rotary(proj("qb", j, RET_DK), cos, sin)
        k_b = _rotary(proj("kb", j, RET_DK), cos, sin) * (RET_DK ** -0.5)
        dec_in = jnp.exp((tl + 1.0) * lg)
        dec_out = jnp.exp((T - 1.0 - tl) * lg)
        z["qb"][j] = q_b.astype(BF16)
        z["kb"][j] = k_b.astype(BF16)
        z["qbi"][j] = (q_b * jnp.concatenate([dec_in, dec_in], axis=-1)).astype(BF16)
        z["kbo"][j] = (k_b * jnp.concatenate([dec_out, dec_out], axis=-1)).astype(BF16)
        yield
        z["vb"][j] = proj("vb", j, RET_DV).astype(BF16)
        g_b = proj("gb", j, RET_DV)
        yield
        z["gtb"][:, j * RET_DV:(j + 1) * RET_DV] = _sigmoid(proj("mb", j, RET_DV)) * (g_b * _sigmoid(g_b))
        yield


def _recur_block(z, lg_ref, g, gn_ref, sgt_ref, sret_ref, merged_ref):
    T = T_BLK
    ri = lax.broadcasted_iota(jnp.int32, (H_GLA, H_GLA), 0)
    ci = lax.broadcasted_iota(jnp.int32, (H_GLA, H_GLA), 1)
    causal = ri >= ci
    st = sgt_ref[...]
    o_chunks = []
    for c in range(N_CHUNKS):
        sl = slice(c * C_GLA, (c + 1) * C_GLA)
        v_c = z["va"][sl]
        diag = []
        for h in range(2):
            rows = slice(c * C_GLA + h * H_GLA, c * C_GLA + (h + 1) * H_GLA)
            diag.append(jnp.where(causal, _dot_nt(z["qx"][rows], z["kx"][rows]), 0.0))
        cross = _dot_nt(z["qc"][c], z["kc"][c])
        a = jnp.concatenate([jnp.concatenate([diag[0], jnp.zeros_like(cross)], axis=1),
                             jnp.concatenate([cross, diag[1]], axis=1)], axis=0).astype(BF16)
        o_chunks.append(_dot_nt(z["qi"][sl], st.astype(BF16)) + _dot(a, v_c))
        st = st * z["dl"][c:c + 1] + _dot_tn(v_c, z["ko"][sl])
        yield
    sgt_ref[...] = st
    part_a = z["gta"][...] * (_rms(jnp.concatenate(o_chunks, axis=0)) * gn_ref[...])
    yield

    rt = lax.broadcasted_iota(jnp.int32, (T, T), 0)
    ct = lax.broadcasted_iota(jnp.int32, (T, T), 1)
    parts_b = []
    for j in range(RET_PER_GROUP):
        lg = lg_ref[g * RET_PER_GROUP + j]
        v_b = z["vb"][j]
        dmat = jnp.where(rt >= ct, jnp.exp((rt - ct).astype(F32) * lg), 0.0)
        a = (_dot_nt(z["qb"][j], z["kb"][j]) * dmat).astype(BF16)
        s = sret_ref[0, j]
        o_b = _rms(_dot(z["qbi"][j], s.astype(BF16)) + _dot(a, v_b))
        dec_all = jnp.exp(jnp.full((1, RET_DV), T * lg, F32))
        sret_ref[0, j] = s * dec_all + _dot_tn(z["kbo"][j], v_b)
        parts_b.append(z["gtb"][:, j * RET_DV:(j + 1) * RET_DV] * o_b)
        yield
    merged_ref[0] = (part_a + jnp.concatenate(parts_b, axis=-1)).astype(merged_ref.dtype)


def _mix_prompt_kernel(lg_ref, u_ref, la_ref, *refs, blocks_per_group, blocks_per_seq):
    nw, nz = len(W_NAMES), len(Z_NAMES)
    w = dict(zip(W_NAMES, refs[:nw]))
    refs = list(refs[nw:])
    gn_ref, cos_ref, sin_ref = refs[:3]
    sample_in = refs[3:3 + N_SAMPLE_IN]
    cast_in = refs[3 + N_SAMPLE_IN:3 + N_SAMPLE_IN + N_CAST]
    refs = refs[3 + N_SAMPLE_IN + N_CAST:]
    merged_ref, sgla_ref, sret_ref = refs[:3]
    sample_out = refs[3:3 + N_SAMPLE_OUT]
    cast_out = refs[3 + N_SAMPLE_OUT:3 + N_SAMPLE_OUT + N_CAST]
    sgt_ref = refs[3 + N_SAMPLE_OUT + N_CAST]
    refs = refs[4 + N_SAMPLE_OUT + N_CAST:]
    z_even = dict(zip(Z_NAMES, refs[:nz]))
    z_odd = dict(zip(Z_NAMES, refs[nz:]))
    s = pl.program_id(0)
    n_blocks = pl.num_programs(0) - 1
    g_p = jnp.minimum(s, n_blocks - 1) // blocks_per_group
    r = jnp.maximum(s - 1, 0)
    g_r = r // blocks_per_group
    t_r = r % blocks_per_seq

    @pl.when(s == 0)
    def _():
        for ref in z_odd.values():
            ref[...] = jnp.zeros_like(ref)

    @pl.when(t_r == 0)
    def _():
        sgt_ref[...] = jnp.zeros_like(sgt_ref)
        sret_ref[...] = jnp.zeros_like(sret_ref)

    for src, dst in zip(cast_in, cast_out):
        dst[...] = src[...].astype(dst.dtype)

    def step(z_write, z_read):
        rec = _recur_block(z_read, lg_ref, g_r, gn_ref, sgt_ref, sret_ref, merged_ref)
        prj = _project_block(u_ref, la_ref, w, cos_ref, sin_ref, lg_ref, g_p, z_write)
        upd = _state_update(lg_ref, *sample_in, *sample_out)
        for _ in itertools.zip_longest(prj, rec, upd):
            pass

    @pl.when(s % 2 == 0)
    def _():
        step(z_even, z_odd)

    @pl.when(s % 2 == 1)
    def _():
        step(z_odd, z_even)

    @pl.when((t_r == blocks_per_seq - 1) & (s > 0))
    def _():
        sgla_ref[0, 0] = sgt_ref[...].T


def _norm_kernel(x_ref, g_ref, wr_ref, wup_ref, bg_ref, u_ref, la_ref):
    u = (_rms(x_ref[...]) * g_ref[...]).astype(u_ref.dtype)
    u_ref[...] = u
    r = _gate_code(u, wr_ref)
    for g in range(N_GROUPS):
        la_ref[g] = _gla_log_alpha(r, wup_ref[g], bg_ref[g])


def _norm_prompt(x, gain, w_f32, wup, bg):
    n, D = x.shape
    tm = min(TM_NORM, n)
    return pl.pallas_call(
        _norm_kernel,
        grid=(n // tm,),
        in_specs=[pl.BlockSpec((tm, D), lambda i: (i, 0)), pl.BlockSpec((1, D), lambda i: (0, 0)),
                  _gate_code_spec(), pl.BlockSpec(wup.shape, lambda i: (0, 0, 0)),
                  pl.BlockSpec(bg.shape, lambda i: (0, 0, 0))],
        out_specs=[pl.BlockSpec((tm, D), lambda i: (i, 0)),
                   pl.BlockSpec((N_GROUPS, tm, GLA_DK), lambda i: (0, i, 0))],
        out_shape=[jax.ShapeDtypeStruct((n, D), BF16), jax.ShapeDtypeStruct((N_GROUPS, n, GLA_DK), F32)],
        compiler_params=pltpu.CompilerParams(
            dimension_semantics=("arbitrary",), vmem_limit_bytes=VMEM_LIMIT),
        name="norm_prompt",
    )(x, gain, w_f32, wup, bg)


def _mix_prompt(u, la, lg, w_t, gn, cos, sin, sg, sr, rows, to_cast):
    B, L, D = u.shape
    nt = L // T_BLK
    n_blocks = N_GROUPS * B * nt
    n_req = sg.shape[0]
    assert n_req <= n_blocks + 1
    assert rows.shape == (n_req, 1, ROW_W) and len(to_cast) == N_CAST
    slab = ROW_TILE_BF16
    assert all(a.shape[0] <= slab * n_blocks and a.shape[0] % slab == 0 for a in to_cast)

    def cast_specs():
        return [pl.BlockSpec((slab, a.shape[1]), lambda s, n=a.shape[0] // slab: (jnp.minimum(s, n - 1), 0))
                for a in to_cast]

    def req_row(s):
        return (jnp.minimum(s, n_req - 1), 0, 0)

    def req_blk(s):
        return (jnp.minimum(s, n_req - 1), 0, 0, 0)

    def proj_idx(s):
        p = jnp.minimum(s, n_blocks - 1)
        return p // (B * nt), (p // nt) % B, p % nt

    def recur_idx(s):
        r = jnp.maximum(s - 1, 0)
        return r // (B * nt), (r // nt) % B, r % nt

    def out_map(s):
        g, b, t = recur_idx(s)
        return (b, t, g)

    def state_map(s):
        g, b, _ = recur_idx(s)
        return (b, g, 0, 0)

    z_scratch = [pltpu.VMEM(shape, dtype) for _, shape, dtype in Z_BUFFERS]
    return pl.pallas_call(
        functools.partial(_mix_prompt_kernel, blocks_per_group=B * nt, blocks_per_seq=nt),
        grid=(n_blocks + 1,),
        in_specs=[
            pl.BlockSpec(memory_space=pltpu.SMEM),
            pl.BlockSpec((1, T_BLK, D), lambda s: (proj_idx(s)[1], proj_idx(s)[2], 0)),
            pl.BlockSpec((1, 1, T_BLK, GLA_DK), lambda s: proj_idx(s) + (0,)),
            *_weight_specs(lambda s: proj_idx(s)[0], pipeline_mode=pl.Buffered(1)),
            pl.BlockSpec((1, GLA_DV), lambda s: (0, 0)),
            pl.BlockSpec((T_BLK, LANE), lambda s: (proj_idx(s)[2], 0)),
            pl.BlockSpec((T_BLK, LANE), lambda s: (proj_idx(s)[2], 0)),
            pl.BlockSpec((1, GLA_HEADS, GLA_DK, GLA_DV), req_blk),
            pl.BlockSpec((1, RET_HEADS, RET_DK, RET_DV), req_blk),
            pl.BlockSpec((1, 1, ROW_W), req_row),
            *cast_specs(),
        ],
        out_specs=[
            pl.BlockSpec((1, T_BLK, GW), out_map),
            pl.BlockSpec((1, 1, GLA_DK, GLA_DV), state_map),
            pl.BlockSpec((1, RET_PER_GROUP, RET_DK, RET_DV), state_map),
            pl.BlockSpec((1, GLA_HEADS, GLA_DK, GLA_DV), req_blk),
            pl.BlockSpec((1, RET_HEADS, RET_DK, RET_DV), req_blk),
            pl.BlockSpec((1, 1, O_W), req_row),
            *cast_specs(),
        ],
        out_shape=[
            jax.ShapeDtypeStruct((B, L, D_MODEL), BF16),
            jax.ShapeDtypeStruct((B, GLA_HEADS, GLA_DK, GLA_DV), F32),
            jax.ShapeDtypeStruct((B, RET_HEADS, RET_DK, RET_DV), F32),
            jax.ShapeDtypeStruct(sg.shape, F32),
            jax.ShapeDtypeStruct(sr.shape, F32),
            jax.ShapeDtypeStruct((n_req, 1, O_W), F32),
            *[jax.ShapeDtypeStruct(a.shape, BF16) for a in to_cast],
        ],
        scratch_shapes=[pltpu.VMEM((GLA_DV, GLA_DK), F32)] + z_scratch + z_scratch,
        compiler_params=pltpu.CompilerParams(
            dimension_semantics=("arbitrary",), vmem_limit_bytes=VMEM_LIMIT),
        name="mix_prompt",
    )(lg, u, la, *([w_t] * N_W), gn, cos, sin, sg, sr, rows, *to_cast)


GATE_PIECES = ("ga", "ma", "gb", "mb")


def _merge_sample(o_ref, gates, gn):
    ga_ref, ma_ref, gb_ref, mb_ref = gates
    merged = []
    for h in range(GLA_HEADS):
        sl = slice(h * GLA_DV, (h + 1) * GLA_DV)
        g_a = ga_ref[:, sl]
        part_a = _sigmoid(ma_ref[:, sl]) * (_rms(o_ref[:, 0, sl]) * gn * (g_a * _sigmoid(g_a)))
        parts_b = []
        for j in range(RET_PER_GROUP):
            sb = slice(h * GW + j * RET_DV, h * GW + (j + 1) * RET_DV)
            g_b = gb_ref[:, sb]
            o_b = o_ref[:, 0, GLA_V + sb.start:GLA_V + sb.stop]
            parts_b.append(_sigmoid(mb_ref[:, sb]) * (_rms(o_b) * (g_b * _sigmoid(g_b))))
        merged.append(part_a + jnp.concatenate(parts_b, axis=-1))
    return jnp.concatenate(merged, axis=-1)


def _out_kernel(x_ref, mg_ref, p_ref, xs_ref, os_ref, ps_ref, *refs, final_norm):
    gates = refs[:len(GATE_PIECES)]
    gn_ref, wout_ref, nple_ref, wpg_ref, wpp_ref, nfin_ref, y_ref, ys_ref = refs[len(GATE_PIECES):]
    i = pl.program_id(0)
    prompt_tiles = pl.num_programs(0) - 1

    def tile(x, merged, p):
        h = x + _dot(merged, wout_ref[...])
        hn = (_rms(h) * nple_ref[...]).astype(BF16)
        gate = _sigmoid(_dot(hn, wpg_ref[...]))
        h = h + gate * _dot(p.astype(BF16), wpp_ref[...])
        if final_norm:
            h = _rms(h) * nfin_ref[...]
        return h

    @pl.when(i < prompt_tiles)
    def _():
        y_ref[...] = tile(x_ref[...], mg_ref[...], p_ref[...])

    @pl.when(i == prompt_tiles)
    def _():
        merged = _merge_sample(os_ref, gates, gn_ref[...]).astype(BF16)
        ys_ref[:, 0, :] = tile(xs_ref[:, 0, :], merged, ps_ref[:, 0, :])


def _out_proj(x, merged, p, xs, o_s, p_s, z_raw, gn, w_out, nple, w_pg, w_pp, nfin, final_norm):
    n, D = x.shape
    tm = min(TM_OUT, n)
    nt = n // tm
    const = lambda i: (0, 0)
    once = dict(pipeline_mode=pl.Buffered(1))

    def token_spec(a):
        return pl.BlockSpec((tm, a.shape[1]), lambda i: (jnp.minimum(i, nt - 1), 0))

    def sample_spec(a):
        return pl.BlockSpec(a.shape, lambda i: (0, 0, 0), **once)

    def gate_spec(name):
        start, width = W_PIECES[name]
        assert start % D == 0 and N_GROUPS * width == D
        return pl.BlockSpec((z_raw.shape[0], D), lambda i: (0, start // D), **once)

    return pl.pallas_call(
        functools.partial(_out_kernel, final_norm=final_norm),
        grid=(nt + 1,),
        in_specs=[
            token_spec(x),
            token_spec(merged),
            token_spec(p),
            sample_spec(xs),
            sample_spec(o_s),
            sample_spec(p_s),
            *[gate_spec(name) for name in GATE_PIECES],
            pl.BlockSpec((1, GLA_DV), const),
            pl.BlockSpec((D, D), const, **once),
            pl.BlockSpec((1, D), const),
            pl.BlockSpec((D, D), const, **once),
            pl.BlockSpec((PLE_DIM, D), const, **once),
            pl.BlockSpec((1, D), const),
        ],
        out_specs=[token_spec(x), pl.BlockSpec(xs.shape, lambda i: (0, 0, 0))],
        out_shape=[jax.ShapeDtypeStruct(x.shape, F32), jax.ShapeDtypeStruct(xs.shape, F32)],
        compiler_params=pltpu.CompilerParams(
            dimension_semantics=("arbitrary",), vmem_limit_bytes=VMEM_LIMIT),
        name="out_proj",
    )(x, merged, p, xs, o_s, p_s, *([z_raw] * len(GATE_PIECES)), gn, w_out, nple, w_pg, w_pp, nfin)


def _sample_rows(zs_ref, r, wup_ref, bg_ref, cos_ref, sin_ref, row_ref):
    def piece(name):
        start, width = W_PIECES[name]
        assert start % PREP_ROWS == 0 and (N_GROUPS * width) % PREP_ROWS == 0
        first = start // PREP_ROWS
        return jnp.concatenate([zs_ref[c] for c in range(first, first + N_GROUPS * width // PREP_ROWS)], axis=-1)

    def put(name, value, h=0):
        off = ROW_OFFS[name] + h * value.shape[-1]
        row_ref[:, 0, off:off + value.shape[-1]] = value

    for g in range(N_GROUPS):
        put("dec", jnp.exp(_gla_log_alpha(r, wup_ref[g], bg_ref[g])), g)
    put("qa", piece("qa") * (GLA_DK ** -0.5))
    for name in ("ka", "va", "vb"):
        put(name, piece(name))
    cos = cos_ref[...]
    sin = sin_ref[...]
    q_b = piece("qb")
    k_b = piece("kb")
    for h in range(RET_HEADS):
        sl = slice(h * RET_DK, (h + 1) * RET_DK)
        put("qb", _rotary(q_b[:, sl], cos, sin), h)
        put("kb", _rotary(k_b[:, sl], cos, sin) * (RET_DK ** -0.5), h)


def _prep_kernel(x_ref, nmix_ref, w_ref, wr_ref, wup_ref, bg_ref, cos_ref, sin_ref,
                 wb_ref, z_ref, row_ref, u_ref, r_ref, zs_ref):
    k = pl.program_id(0)

    @pl.when(k == 0)
    def _():
        u = (_rms(_tokens(x_ref)) * nmix_ref[...]).astype(BF16)
        u_ref[...] = u
        r_ref[...] = _gate_code(u, wr_ref)

    wb = w_ref[...].astype(BF16)
    wb_ref[...] = wb.T
    z = _dot_nt(u_ref[...], wb)
    z_ref[...] = z
    zs_ref[k] = z

    @pl.when(k == pl.num_programs(0) - 1)
    def _():
        _sample_rows(zs_ref, r_ref[...], wup_ref, bg_ref, cos_ref, sin_ref, row_ref)


def _prep_weights(x, nmix, w_f32, wup, bg, cos, sin):
    n, D = x.shape[0], x.shape[-1]
    chunks = N_PACK // PREP_ROWS
    lo_chunks = R_START // PREP_ROWS
    assert R_START % PREP_ROWS == 0 and N_PACK % PREP_ROWS == 0 and HI_START % SUBLANE == 0

    def rows(k):
        skip = jnp.where(k >= lo_chunks, GLA_RANK // SUBLANE, 0)
        return (SUBLANE * ((PREP_ROWS // SUBLANE) * k + skip), 0)

    def whole(a):
        return pl.BlockSpec(a.shape, lambda k: (0,) * a.ndim)

    return pl.pallas_call(
        _prep_kernel,
        grid=(chunks,),
        in_specs=[
            whole(x),
            whole(nmix),
            pl.BlockSpec((pl.Element(PREP_ROWS), pl.Element(D)), rows),
            _gate_code_spec(),
            whole(wup),
            whole(bg),
            whole(cos),
            whole(sin),
        ],
        out_specs=[
            pl.BlockSpec((D, PREP_ROWS), lambda k: (0, k)),
            pl.BlockSpec((n, PREP_ROWS), lambda k: (0, k)),
            pl.BlockSpec((n, 1, ROW_W), lambda k: (0, 0, 0)),
        ],
        out_shape=[
            jax.ShapeDtypeStruct((D, N_PACK), BF16),
            jax.ShapeDtypeStruct((n, N_PACK), F32),
            jax.ShapeDtypeStruct((n, 1, ROW_W), F32),
        ],
        scratch_shapes=[pltpu.VMEM((n, D), BF16), pltpu.VMEM((n, LANE), F32),
                        pltpu.VMEM((chunks, n, PREP_ROWS), F32)],
        compiler_params=pltpu.CompilerParams(
            dimension_semantics=("arbitrary",), vmem_limit_bytes=VMEM_LIMIT),
        name="prep_weights",
    )(x, nmix, w_f32, w_f32, wup, bg, cos, sin)


def _rope_tables(pos):
    half = RET_DK // 2
    inv = 1.0 / (ROPE_BASE ** jnp.linspace(0.0, 1.0, half, dtype=jnp.float32))
    ang = pos[:, None] * inv[None, :]
    return jnp.cos(ang), jnp.sin(ang)


def kernel(x_prompt, x_sample, state_gla, state_ret, p_prompt, p_sample, norm_mix, w_in, w_gla_up, b_gla,
           gla_norm, w_out, norm_ple, w_ple_gate, w_ple_proj, norm_final):
    depth = w_in.shape[0]
    Bp, Lp, D = x_prompt.shape
    Bs, Ls, _ = x_sample.shape
    assert Ls == 1 and Lp % T_BLK == 0
    cos_p, sin_p = _rope_tables(jnp.arange(Lp, dtype=jnp.float32))
    cos_s, sin_s = _rope_tables(PAST_LEN + jnp.arange(Ls, dtype=jnp.float32))
    log_gamma = jnp.log(1.0 - jnp.exp2(-5.0 - jnp.arange(RET_HEADS, dtype=jnp.float32)))
    nfin = norm_final.reshape(1, D)

    hp = x_prompt
    hs = x_sample
    gla_p, ret_p, gla_s, ret_s = [], [], [], []
    for i in range(depth):
        last = i == depth - 1
        nmix = norm_mix[i].reshape(1, D)
        nple = norm_ple[i].reshape(1, D)
        gn = gla_norm[i].reshape(1, GLA_DV)
        w_f32 = w_in[i].T
        wup = jnp.pad(w_gla_up[i], ((0, LANE - GLA_RANK), (0, 0))).astype(BF16)
        wup = wup.reshape(LANE, GLA_HEADS, GLA_DK).transpose(1, 0, 2)
        bg = b_gla[i].reshape(GLA_HEADS, 1, GLA_DK)
        w_pp = w_ple_proj[i].astype(BF16)

        w_t, z_raw, rows = _prep_weights(hs, nmix, w_f32, wup, bg, cos_s, sin_s)
        u, la = _norm_prompt(hp.reshape(Bp * Lp, D), nmix, w_f32, wup, bg)
        merged, sg, sr, nsg, nsr, o_s, w_o, w_pg = _mix_prompt(
            u.reshape(Bp, Lp, D), la.reshape(N_GROUPS, Bp, Lp, GLA_DK), log_gamma, w_t, gn, cos_p, sin_p,
            state_gla[i], state_ret[i], rows, [w_out[i], w_ple_gate[i]])
        hp, hs = _out_proj(hp.reshape(Bp * Lp, D), merged.reshape(Bp * Lp, D),
                           p_prompt[i].reshape(Bp * Lp, PLE_DIM), hs, o_s, p_sample[i], z_raw, gn,
                           w_o, nple, w_pg, w_pp, nfin, last)
        hp = hp.reshape(Bp, Lp, D)
        gla_p.append(sg)
        ret_p.append(sr)
        gla_s.append(nsg)
        ret_s.append(nsr)

    return (hp, hs, jnp.stack(gla_p), jnp.stack(ret_p), jnp.stack(gla_s), jnp.stack(ret_s))
```

```python
import functools
import itertools

import jax
import jax.numpy as jnp
import numpy as np
from jax import lax
from jax.experimental import pallas as pl
from jax.experimental.pallas import tpu as pltpu

F32 = jnp.float32
BF16 = jnp.bfloat16

D_MODEL = 2048
PAST_LEN = 16384
PLE_DIM = 256
GLA_HEADS = 4
GLA_DK = 256
GLA_DV = 512
GLA_RANK = 16
GLA_TAU = 16.0
RET_HEADS = 8
RET_DK = 256
RET_DV = 256
ROPE_BASE = 10000.0
EPS = 1e-6

GLA_QK = GLA_HEADS * GLA_DK
GLA_V = GLA_HEADS * GLA_DV
RET_QK = RET_HEADS * RET_DK
RET_V = RET_HEADS * RET_DV
IN_SPLITS = (GLA_QK, GLA_QK, GLA_V, GLA_V, GLA_RANK, RET_QK, RET_QK, RET_V, RET_V, D_MODEL, D_MODEL)
IN_OFFS = tuple(int(v) for v in np.concatenate([[0], np.cumsum(IN_SPLITS)[:-1]]))

N_GROUPS = GLA_HEADS
RET_PER_GROUP = RET_HEADS // N_GROUPS
GW = GLA_DV
LANE = 128
SUBLANE = 8
ROW_TILE_BF16 = 16

R_START = IN_OFFS[4]
HI_START = IN_OFFS[5]
N_PACK = HI_START - GLA_RANK + sum(IN_SPLITS[5:])
W_PIECES = {
    "qa": (IN_OFFS[0], GLA_DK),
    "ka": (IN_OFFS[1], GLA_DK),
    "va": (IN_OFFS[2], GLA_DV),
    "ga": (IN_OFFS[3], GLA_DV),
    "qb": (IN_OFFS[5] - GLA_RANK, GW),
    "kb": (IN_OFFS[6] - GLA_RANK, GW),
    "vb": (IN_OFFS[7] - GLA_RANK, GW),
    "gb": (IN_OFFS[8] - GLA_RANK, GW),
    "ma": (IN_OFFS[9] - GLA_RANK, GW),
    "mb": (IN_OFFS[10] - GLA_RANK, GW),
}
W_NAMES = tuple(W_PIECES)

T_BLK = 256
C_GLA = 128
H_GLA = C_GLA // 2
N_CHUNKS = T_BLK // C_GLA
TM_OUT = 512
TM_NORM = 1024
PREP_ROWS = 1024
VMEM_LIMIT = 56 * 1024 * 1024
MIX_VMEM_LIMIT = 62 * 1024 * 1024
MIX_PREFETCHED = ("qa", "ka", "va", "ga", "qb", "kb")


def _rms(x):
    return x * lax.rsqrt(jnp.mean(x * x, axis=-1, keepdims=True) + EPS)


def _sigmoid(x):
    return 1.0 / (1.0 + jnp.exp(-x))


def _log_sigmoid(x):
    return jnp.minimum(x, 0.0) - jnp.log(1.0 + jnp.exp(-jnp.abs(x)))


def _tokens(ref):
    return ref[...] if len(ref.shape) == 2 else ref[:, 0, :]


def _dot(a, b):
    return jnp.dot(a, b, preferred_element_type=F32)


def _dot_nt(a, b):
    return lax.dot_general(a, b, (((1,), (1,)), ((), ())), preferred_element_type=F32)


def _dot_tn(a, b):
    return lax.dot_general(a, b, (((0,), (0,)), ((), ())), preferred_element_type=F32)


def _rotary(x, cos, sin):
    half = x.shape[-1] // 2
    x1, x2 = x[:, :half], x[:, half:]
    return jnp.concatenate([x1 * cos - x2 * sin, x1 * sin + x2 * cos], axis=-1)


def _gla_log_alpha(r, wup, bg):
    pre = _dot(r.astype(BF16), wup) + bg
    return _log_sigmoid(pre) * (1.0 / GLA_TAU)


def _weight_specs(group_of, prefetched=(), **kw):
    specs = []
    for n in W_NAMES:
        start, width = W_PIECES[n]
        assert start % width == 0

        def index_map(*idx, first=start // width):
            return (0, first + group_of(*idx))

        specs.append(pl.BlockSpec((D_MODEL, width), index_map, **({} if n in prefetched else kw)))
    return specs


def _gate_code_spec():
    return pl.BlockSpec((pl.Element(GLA_RANK), pl.Element(D_MODEL)), lambda *idx: (R_START, 0))


def _gate_code(u, wr_ref):
    wr = jnp.concatenate([wr_ref[...].astype(BF16), jnp.zeros((LANE - GLA_RANK, D_MODEL), BF16)], axis=0)
    return _dot_nt(u, wr)


N_W = len(W_NAMES)


Z_BUFFERS = (
    ("qx", (T_BLK, GLA_DK), BF16),
    ("kx", (T_BLK, GLA_DK), BF16),
    ("qc", (N_CHUNKS, H_GLA, GLA_DK), BF16),
    ("kc", (N_CHUNKS, H_GLA, GLA_DK), BF16),
    ("qi", (T_BLK, GLA_DK), BF16),
    ("ko", (T_BLK, GLA_DK), BF16),
    ("va", (T_BLK, GLA_DV), BF16),
    ("dl", (SUBLANE, GLA_DK), F32),
    ("gta", (T_BLK, GW), F32),
    ("qb", (RET_PER_GROUP, T_BLK, RET_DK), BF16),
    ("kb", (RET_PER_GROUP, T_BLK, RET_DK), BF16),
    ("qbi", (RET_PER_GROUP, T_BLK, RET_DK), BF16),
    ("kbo", (RET_PER_GROUP, T_BLK, RET_DK), BF16),
    ("vb", (RET_PER_GROUP, T_BLK, RET_DV), BF16),
    ("gtb", (T_BLK, GW), F32),
)
Z_NAMES = tuple(n for n, _, _ in Z_BUFFERS)
assert N_CHUNKS <= SUBLANE


VT_DEC = 0
VT_KA = VT_DEC + GLA_HEADS
VT_QA = VT_KA + GLA_HEADS
VT_KB = VT_QA + GLA_HEADS
VT_QB = VT_KB + RET_HEADS
VT_N = VT_QB + RET_HEADS
VT_PAD = -(-VT_N // SUBLANE) * SUBLANE


ROW_PIECES = (("dec", GLA_HEADS, GLA_DK), ("ka", GLA_HEADS, GLA_DK), ("qa", GLA_HEADS, GLA_DK),
              ("kb", RET_HEADS, RET_DK), ("qb", RET_HEADS, RET_DK), ("va", GLA_HEADS, GLA_DV),
              ("vb", RET_HEADS, RET_DV))
ROW_OFFS = dict(zip([n for n, _, _ in ROW_PIECES],
                    np.concatenate([[0], np.cumsum([h * w for _, h, w in ROW_PIECES])[:-1]]).tolist()))
ROW_W = sum(h * w for _, h, w in ROW_PIECES)
O_W = GLA_V + RET_V


def _state_update(lg_ref, sg_ref, sr_ref, row_ref, nsg_ref, nsr_ref, o_ref):
    def vec(name, h, width):
        off = ROW_OFFS[name] + h * width
        return row_ref[0, :, off:off + width]

    rows = [vec(name, h, GLA_DK) for name, heads, _ in ROW_PIECES[:5] for h in range(heads)]
    rows.append(jnp.zeros((VT_PAD - VT_N, GLA_DK), F32))
    vt = jnp.concatenate(rows, axis=0).T

    def col(i):
        return vt[:, i:i + 1]

    for h in range(GLA_HEADS):
        s_new = col(VT_DEC + h) * sg_ref[0, h] + col(VT_KA + h) * vec("va", h, GLA_DV)
        nsg_ref[0, h] = s_new
        o_ref[0, :, h * GLA_DV:(h + 1) * GLA_DV] = jnp.sum(col(VT_QA + h) * s_new, axis=0, keepdims=True)
        yield
    for h in range(RET_HEADS):
        gamma = jnp.exp(jnp.full((1, RET_DV), lg_ref[h], F32))
        s_new = gamma * sr_ref[0, h] + col(VT_KB + h) * vec("vb", h, RET_DV)
        nsr_ref[0, h] = s_new
        o_ref[0, :, GLA_V + h * RET_DV:GLA_V + (h + 1) * RET_DV] = jnp.sum(
            col(VT_QB + h) * s_new, axis=0, keepdims=True)
        yield


N_SAMPLE_IN = 3
N_SAMPLE_OUT = 3
N_CAST = 2


def _project_block(u_ref, la_ref, w, cos_ref, sin_ref, lg_ref, g, z):
    T = T_BLK
    u = u_ref[0]

    def proj(name, j=0, n=None):
        ref = w[name]
        n = ref.shape[1] if n is None else n
        return _dot(u, ref[:, j * n:(j + 1) * n])

    la = la_ref[0, 0]
    q = proj("qa") * (GLA_DK ** -0.5)
    yield
    k = proj("ka")
    ri = lax.broadcasted_iota(jnp.int32, (C_GLA, C_GLA), 0)
    ci = lax.broadcasted_iota(jnp.int32, (C_GLA, C_GLA), 1)
    tri = jnp.where(ri >= ci, 1.0, 0.0).astype(BF16)
    for c in range(N_CHUNKS):
        sl = slice(c * C_GLA, (c + 1) * C_GLA)
        la_c = la[sl]
        la_hi = la_c.astype(BF16)
        la_lo = (la_c - la_hi.astype(F32)).astype(BF16)
        b = _dot(tri, la_hi) + _dot(tri, la_lo)
        b_last = b[C_GLA - 1:C_GLA]
        q_c, k_c = q[sl], k[sl]
        z["qi"][sl] = (q_c * jnp.exp(b)).astype(BF16)
        z["ko"][sl] = (k_c * jnp.exp(b_last - b)).astype(BF16)
        z["dl"][c:c + 1] = jnp.exp(b_last)
        for h in range(2):
            hs = slice(h * H_GLA, (h + 1) * H_GLA)
            rows = slice(c * C_GLA + h * H_GLA, c * C_GLA + (h + 1) * H_GLA)
            b_h = b[hs]
            b_mid = b_h[H_GLA // 2 - 1:H_GLA // 2]
            z["qx"][rows] = (q_c[hs] * jnp.exp(b_h - b_mid)).astype(BF16)
            z["kx"][rows] = (k_c[hs] * jnp.exp(b_mid - b_h)).astype(BF16)
        b_edge = b[H_GLA - 1:H_GLA]
        z["qc"][c] = (q_c[H_GLA:] * jnp.exp(b[H_GLA:] - b_edge)).astype(BF16)
        z["kc"][c] = (k_c[:H_GLA] * jnp.exp(b_edge - b[:H_GLA])).astype(BF16)
    yield
    z["va"][...] = proj("va").astype(BF16)
    yield
    g_a = proj("ga")
    silu_a = g_a * _sigmoid(g_a)
    yield
    z["gta"][...] = _sigmoid(proj("ma")) * silu_a
    yield

    cos = cos_ref[...]
    sin = sin_ref[...]
    tl = lax.broadcasted_iota(jnp.int32, (T, LANE), 0).astype(F32)
    for j in range(RET_PER_GROUP):
        lg = lg_ref[g * RET_PER_GROUP + j]
        q_b = _rotary(proj("qb", j, RET_DK), cos, sin)
        k_b = _rotary(proj("kb", j, RET_DK), cos, sin) * (RET_DK ** -0.5)
        dec_in = jnp.exp((tl + 1.0) * lg)
        dec_out = jnp.exp((T - 1.0 - tl) * lg)
        z["qb"][j] = q_b.astype(BF16)
        z["kb"][j] = k_b.astype(BF16)
        z["qbi"][j] = (q_b * jnp.concatenate([dec_in, dec_in], axis=-1)).astype(BF16)
        z["kbo"][j] = (k_b * jnp.concatenate([dec_out, dec_out], axis=-1)).astype(BF16)
        yield
        z["vb"][j] = proj("vb", j, RET_DV).astype(BF16)
        g_b = proj("gb", j, RET_DV)
        yield
        z["gtb"][:, j * RET_DV:(j + 1) * RET_DV] = _sigmoid(proj("mb", j, RET_DV)) * (g_b * _sigmoid(g_b))
        yield


def _recur_block(z, lg_ref, g, gn_ref, sgt_ref, sret_ref, merged_ref):
    T = T_BLK
    ri = lax.broadcasted_iota(jnp.int32, (H_GLA, H_GLA), 0)
    ci = lax.broadcasted_iota(jnp.int32, (H_GLA, H_GLA), 1)
    causal = ri >= ci
    st = sgt_ref[...]
    o_chunks = []
    for c in range(N_CHUNKS):
        sl = slice(c * C_GLA, (c + 1) * C_GLA)
        v_c = z["va"][sl]
        diag = []
        for h in range(2):
            rows = slice(c * C_GLA + h * H_GLA, c * C_GLA + (h + 1) * H_GLA)
            diag.append(jnp.where(causal, _dot_nt(z["qx"][rows], z["kx"][rows]), 0.0))
        cross = _dot_nt(z["qc"][c], z["kc"][c])
        a = jnp.concatenate([jnp.concatenate([diag[0], jnp.zeros_like(cross)], axis=1),
                             jnp.concatenate([cross, diag[1]], axis=1)], axis=0).astype(BF16)
        o_chunks.append(_dot_nt(z["qi"][sl], st.astype(BF16)) + _dot(a, v_c))
        st = st * z["dl"][c:c + 1] + _dot_tn(v_c, z["ko"][sl])
        yield
    sgt_ref[...] = st
    part_a = z["gta"][...] * (_rms(jnp.concatenate(o_chunks, axis=0)) * gn_ref[...])
    yield

    rt = lax.broadcasted_iota(jnp.int32, (T, T), 0)
    ct = lax.broadcasted_iota(jnp.int32, (T, T), 1)
    parts_b = []
    for j in range(RET_PER_GROUP):
        lg = lg_ref[g * RET_PER_GROUP + j]
        v_b = z["vb"][j]
        dmat = jnp.where(rt >= ct, jnp.exp((rt - ct).astype(F32) * lg), 0.0)
        a = (_dot_nt(z["qb"][j], z["kb"][j]) * dmat).astype(BF16)
        s = sret_ref[0, j]
        o_b = _rms(_dot(z["qbi"][j], s.astype(BF16)) + _dot(a, v_b))
        dec_all = jnp.exp(jnp.full((1, RET_DV), T * lg, F32))
        sret_ref[0, j] = s * dec_all + _dot_tn(z["kbo"][j], v_b)
        parts_b.append(z["gtb"][:, j * RET_DV:(j + 1) * RET_DV] * o_b)
        yield
    merged_ref[0] = (part_a + jnp.concatenate(parts_b, axis=-1)).astype(merged_ref.dtype)


def _mix_prompt_kernel(lg_ref, u_ref, la_ref, *refs, blocks_per_group, blocks_per_seq):
    nw, nz = len(W_NAMES), len(Z_NAMES)
    w = dict(zip(W_NAMES, refs[:nw]))
    refs = list(refs[nw:])
    gn_ref, cos_ref, sin_ref = refs[:3]
    sample_in = refs[3:3 + N_SAMPLE_IN]
    cast_in = refs[3 + N_SAMPLE_IN:3 + N_SAMPLE_IN + N_CAST]
    refs = refs[3 + N_SAMPLE_IN + N_CAST:]
    merged_ref, sgla_ref, sret_ref = refs[:3]
    sample_out = refs[3:3 + N_SAMPLE_OUT]
    cast_out = refs[3 + N_SAMPLE_OUT:3 + N_SAMPLE_OUT + N_CAST]
    sgt_ref = refs[3 + N_SAMPLE_OUT + N_CAST]
    refs = refs[4 + N_SAMPLE_OUT + N_CAST:]
    z_even = dict(zip(Z_NAMES, refs[:nz]))
    z_odd = dict(zip(Z_NAMES, refs[nz:]))
    s = pl.program_id(0)
    n_blocks = pl.num_programs(0) - 1
    g_p = jnp.minimum(s, n_blocks - 1) // blocks_per_group
    r = jnp.maximum(s - 1, 0)
    g_r = r // blocks_per_group
    t_r = r % blocks_per_seq

    @pl.when(s == 0)
    def _():
        for ref in z_odd.values():
            ref[...] = jnp.zeros_like(ref)

    @pl.when(t_r == 0)
    def _():
        sgt_ref[...] = jnp.zeros_like(sgt_ref)
        sret_ref[...] = jnp.zeros_like(sret_ref)

    for src, dst in zip(cast_in, cast_out):
        dst[...] = src[...].astype(dst.dtype)

    def step(z_write, z_read):
        rec = _recur_block(z_read, lg_ref, g_r, gn_ref, sgt_ref, sret_ref, merged_ref)
        prj = _project_block(u_ref, la_ref, w, cos_ref, sin_ref, lg_ref, g_p, z_write)
        upd = _state_update(lg_ref, *sample_in, *sample_out)
        for _ in itertools.zip_longest(prj, rec, upd):
            pass

    @pl.when(s % 2 == 0)
    def _():
        step(z_even, z_odd)

    @pl.when(s % 2 == 1)
    def _():
        step(z_odd, z_even)

    @pl.when((t_r == blocks_per_seq - 1) & (s > 0))
    def _():
        sgla_ref[0, 0] = sgt_ref[...].T


def _norm_kernel(x_ref, g_ref, wr_ref, wup_ref, bg_ref, u_ref, la_ref):
    u = (_rms(x_ref[...]) * g_ref[...]).astype(u_ref.dtype)
    u_ref[...] = u
    r = _gate_code(u, wr_ref)
    for g in range(N_GROUPS):
        la_ref[g] = _gla_log_alpha(r, wup_ref[g], bg_ref[g])


def _norm_prompt(x, gain, w_f32, wup, bg):
    n, D = x.shape
    tm = min(TM_NORM, n)
    return pl.pallas_call(
        _norm_kernel,
        grid=(n // tm,),
        in_specs=[pl.BlockSpec((tm, D), lambda i: (i, 0)), pl.BlockSpec((1, D), lambda i: (0, 0)),
                  _gate_code_spec(), pl.BlockSpec(wup.shape, lambda i: (0, 0, 0)),
                  pl.BlockSpec(bg.shape, lambda i: (0, 0, 0))],
        out_specs=[pl.BlockSpec((tm, D), lambda i: (i, 0)),
                   pl.BlockSpec((N_GROUPS, tm, GLA_DK), lambda i: (0, i, 0))],
        out_shape=[jax.ShapeDtypeStruct((n, D), BF16), jax.ShapeDtypeStruct((N_GROUPS, n, GLA_DK), F32)],
        compiler_params=pltpu.CompilerParams(
            dimension_semantics=("arbitrary",), vmem_limit_bytes=VMEM_LIMIT),
        name="norm_prompt",
    )(x, gain, w_f32, wup, bg)


def _mix_prompt(u, la, lg, w_t, gn, cos, sin, sg, sr, rows, to_cast):
    B, L, D = u.shape
    nt = L // T_BLK
    n_blocks = N_GROUPS * B * nt
    n_req = sg.shape[0]
    assert n_req <= n_blocks + 1
    assert rows.shape == (n_req, 1, ROW_W) and len(to_cast) == N_CAST
    slab = ROW_TILE_BF16
    assert all(a.shape[0] <= slab * n_blocks and a.shape[0] % slab == 0 for a in to_cast)

    def cast_specs():
        return [pl.BlockSpec((slab, a.shape[1]), lambda s, n=a.shape[0] // slab: (jnp.minimum(s, n - 1), 0))
                for a in to_cast]

    def req_row(s):
        return (jnp.minimum(s, n_req - 1), 0, 0)

    def req_blk(s):
        return (jnp.minimum(s, n_req - 1), 0, 0, 0)

    def proj_idx(s):
        p = jnp.minimum(s, n_blocks - 1)
        return p // (B * nt), (p // nt) % B, p % nt

    def recur_idx(s):
        r = jnp.maximum(s - 1, 0)
        return r // (B * nt), (r // nt) % B, r % nt

    def out_map(s):
        g, b, t = recur_idx(s)
        return (b, t, g)

    def state_map(s):
        g, b, _ = recur_idx(s)
        return (b, g, 0, 0)

    z_scratch = [pltpu.VMEM(shape, dtype) for _, shape, dtype in Z_BUFFERS]
    return pl.pallas_call(
        functools.partial(_mix_prompt_kernel, blocks_per_group=B * nt, blocks_per_seq=nt),
        grid=(n_blocks + 1,),
        in_specs=[
            pl.BlockSpec(memory_space=pltpu.SMEM),
            pl.BlockSpec((1, T_BLK, D), lambda s: (proj_idx(s)[1], proj_idx(s)[2], 0)),
            pl.BlockSpec((1, 1, T_BLK, GLA_DK), lambda s: proj_idx(s) + (0,)),
            *_weight_specs(lambda s: proj_idx(s)[0], prefetched=MIX_PREFETCHED, pipeline_mode=pl.Buffered(1)),
            pl.BlockSpec((1, GLA_DV), lambda s: (0, 0)),
            pl.BlockSpec((T_BLK, LANE), lambda s: (proj_idx(s)[2], 0)),
            pl.BlockSpec((T_BLK, LANE), lambda s: (proj_idx(s)[2], 0)),
            pl.BlockSpec((1, GLA_HEADS, GLA_DK, GLA_DV), req_blk),
            pl.BlockSpec((1, RET_HEADS, RET_DK, RET_DV), req_blk),
            pl.BlockSpec((1, 1, ROW_W), req_row),
            *cast_specs(),
        ],
        out_specs=[
            pl.BlockSpec((1, T_BLK, GW), out_map),
            pl.BlockSpec((1, 1, GLA_DK, GLA_DV), state_map),
            pl.BlockSpec((1, RET_PER_GROUP, RET_DK, RET_DV), state_map),
            pl.BlockSpec((1, GLA_HEADS, GLA_DK, GLA_DV), req_blk),
            pl.BlockSpec((1, RET_HEADS, RET_DK, RET_DV), req_blk),
            pl.BlockSpec((1, 1, O_W), req_row),
            *cast_specs(),
        ],
        out_shape=[
            jax.ShapeDtypeStruct((B, L, D_MODEL), BF16),
            jax.ShapeDtypeStruct((B, GLA_HEADS, GLA_DK, GLA_DV), F32),
            jax.ShapeDtypeStruct((B, RET_HEADS, RET_DK, RET_DV), F32),
            jax.ShapeDtypeStruct(sg.shape, F32),
            jax.ShapeDtypeStruct(sr.shape, F32),
            jax.ShapeDtypeStruct((n_req, 1, O_W), F32),
            *[jax.ShapeDtypeStruct(a.shape, BF16) for a in to_cast],
        ],
        scratch_shapes=[pltpu.VMEM((GLA_DV, GLA_DK), F32)] + z_scratch + z_scratch,
        compiler_params=pltpu.CompilerParams(
            dimension_semantics=("arbitrary",), vmem_limit_bytes=MIX_VMEM_LIMIT),
        name="mix_prompt",
    )(lg, u, la, *([w_t] * N_W), gn, cos, sin, sg, sr, rows, *to_cast)


GATE_PIECES = ("ga", "ma", "gb", "mb")


def _merge_sample(o_ref, gates, gn):
    ga_ref, ma_ref, gb_ref, mb_ref = gates
    merged = []
    for h in range(GLA_HEADS):
        sl = slice(h * GLA_DV, (h + 1) * GLA_DV)
        g_a = ga_ref[:, sl]
        part_a = _sigmoid(ma_ref[:, sl]) * (_rms(o_ref[:, 0, sl]) * gn * (g_a * _sigmoid(g_a)))
        parts_b = []
        for j in range(RET_PER_GROUP):
            sb = slice(h * GW + j * RET_DV, h * GW + (j + 1) * RET_DV)
            g_b = gb_ref[:, sb]
            o_b = o_ref[:, 0, GLA_V + sb.start:GLA_V + sb.stop]
            parts_b.append(_sigmoid(mb_ref[:, sb]) * (_rms(o_b) * (g_b * _sigmoid(g_b))))
        merged.append(part_a + jnp.concatenate(parts_b, axis=-1))
    return jnp.concatenate(merged, axis=-1)


def _out_kernel(x_ref, mg_ref, p_ref, xs_ref, os_ref, ps_ref, *refs, final_norm):
    gates = refs[:len(GATE_PIECES)]
    gn_ref, wout_ref, nple_ref, wpg_ref, wpp_ref, nfin_ref, y_ref, ys_ref = refs[len(GATE_PIECES):]
    i = pl.program_id(0)
    prompt_tiles = pl.num_programs(0) - 1

    def tile(x, merged, p):
        h = x + _dot(merged, wout_ref[...])
        hn = (_rms(h) * nple_ref[...]).astype(BF16)
        gate = _sigmoid(_dot(hn, wpg_ref[...]))
        h = h + gate * _dot(p.astype(BF16), wpp_ref[...])
        if final_norm:
            h = _rms(h) * nfin_ref[...]
        return h

    @pl.when(i < prompt_tiles)
    def _():
        y_ref[...] = tile(x_ref[...], mg_ref[...], p_ref[...])

    @pl.when(i == prompt_tiles)
    def _():
        merged = _merge_sample(os_ref, gates, gn_ref[...]).astype(BF16)
        ys_ref[:, 0, :] = tile(xs_ref[:, 0, :], merged, ps_ref[:, 0, :])


def _out_proj(x, merged, p, xs, o_s, p_s, z_raw, gn, w_out, nple, w_pg, w_pp, nfin, final_norm):
    n, D = x.shape
    tm = min(TM_OUT, n)
    nt = n // tm
    const = lambda i: (0, 0)
    once = dict(pipeline_mode=pl.Buffered(1))

    def token_spec(a):
        return pl.BlockSpec((tm, a.shape[1]), lambda i: (jnp.minimum(i, nt - 1), 0))

    def sample_spec(a):
        return pl.BlockSpec(a.shape, lambda i: (0, 0, 0), **once)

    def gate_spec(name):
        start, width = W_PIECES[name]
        assert start % D == 0 and N_GROUPS * width == D
        return pl.BlockSpec((z_raw.shape[0], D), lambda i: (0, start // D), **once)

    return pl.pallas_call(
        functools.partial(_out_kernel, final_norm=final_norm),
        grid=(nt + 1,),
        in_specs=[
            token_spec(x),
            token_spec(merged),
            token_spec(p),
            sample_spec(xs),
            sample_spec(o_s),
            sample_spec(p_s),
            *[gate_spec(name) for name in GATE_PIECES],
            pl.BlockSpec((1, GLA_DV), const),
            pl.BlockSpec((D, D), const, **once),
            pl.BlockSpec((1, D), const),
            pl.BlockSpec((D, D), const, **once),
            pl.BlockSpec((PLE_DIM, D), const, **once),
            pl.BlockSpec((1, D), const),
        ],
        out_specs=[token_spec(x), pl.BlockSpec(xs.shape, lambda i: (0, 0, 0))],
        out_shape=[jax.ShapeDtypeStruct(x.shape, F32), jax.ShapeDtypeStruct(xs.shape, F32)],
        compiler_params=pltpu.CompilerParams(
            dimension_semantics=("arbitrary",), vmem_limit_bytes=VMEM_LIMIT),
        name="out_proj",
    )(x, merged, p, xs, o_s, p_s, *([z_raw] * len(GATE_PIECES)), gn, w_out, nple, w_pg, w_pp, nfin)


def _sample_rows(zs_ref, r, wup_ref, bg_ref, cos_ref, sin_ref, row_ref):
    def piece(name):
        start, width = W_PIECES[name]
        assert start % PREP_ROWS == 0 and (N_GROUPS * width) % PREP_ROWS == 0
        first = start // PREP_ROWS
        return jnp.concatenate([zs_ref[c] for c in range(first, first + N_GROUPS * width // PREP_ROWS)], axis=-1)

    def put(name, value, h=0):
        off = ROW_OFFS[name] + h * value.shape[-1]
        row_ref[:, 0, off:off + value.shape[-1]] = value

    for g in range(N_GROUPS):
        put("dec", jnp.exp(_gla_log_alpha(r, wup_ref[g], bg_ref[g])), g)
    put("qa", piece("qa") * (GLA_DK ** -0.5))
    for name in ("ka", "va", "vb"):
        put(name, piece(name))
    cos = cos_ref[...]
    sin = sin_ref[...]
    q_b = piece("qb")
    k_b = piece("kb")
    for h in range(RET_HEADS):
        sl = slice(h * RET_DK, (h + 1) * RET_DK)
        put("qb", _rotary(q_b[:, sl], cos, sin), h)
        put("kb", _rotary(k_b[:, sl], cos, sin) * (RET_DK ** -0.5), h)


def _prep_kernel(x_ref, nmix_ref, w_ref, wr_ref, wup_ref, bg_ref, cos_ref, sin_ref,
                 wb_ref, z_ref, row_ref, u_ref, r_ref, zs_ref):
    k = pl.program_id(0)

    @pl.when(k == 0)
    def _():
        u = (_rms(_tokens(x_ref)) * nmix_ref[...]).astype(BF16)
        u_ref[...] = u
        r_ref[...] = _gate_code(u, wr_ref)

    wb = w_ref[...].astype(BF16)
    wb_ref[...] = wb.T
    z = _dot_nt(u_ref[...], wb)
    z_ref[...] = z
    zs_ref[k] = z

    @pl.when(k == pl.num_programs(0) - 1)
    def _():
        _sample_rows(zs_ref, r_ref[...], wup_ref, bg_ref, cos_ref, sin_ref, row_ref)


def _prep_weights(x, nmix, w_f32, wup, bg, cos, sin):
    n, D = x.shape[0], x.shape[-1]
    chunks = N_PACK // PREP_ROWS
    lo_chunks = R_START // PREP_ROWS
    assert R_START % PREP_ROWS == 0 and N_PACK % PREP_ROWS == 0 and HI_START % SUBLANE == 0

    def rows(k):
        skip = jnp.where(k >= lo_chunks, GLA_RANK // SUBLANE, 0)
        return (SUBLANE * ((PREP_ROWS // SUBLANE) * k + skip), 0)

    def whole(a):
        return pl.BlockSpec(a.shape, lambda k: (0,) * a.ndim)

    return pl.pallas_call(
        _prep_kernel,
        grid=(chunks,),
        in_specs=[
            whole(x),
            whole(nmix),
            pl.BlockSpec((pl.Element(PREP_ROWS), pl.Element(D)), rows),
            _gate_code_spec(),
            whole(wup),
            whole(bg),
            whole(cos),
            whole(sin),
        ],
        out_specs=[
            pl.BlockSpec((D, PREP_ROWS), lambda k: (0, k)),
            pl.BlockSpec((n, PREP_ROWS), lambda k: (0, k)),
            pl.BlockSpec((n, 1, ROW_W), lambda k: (0, 0, 0)),
        ],
        out_shape=[
            jax.ShapeDtypeStruct((D, N_PACK), BF16),
            jax.ShapeDtypeStruct((n, N_PACK), F32),
            jax.ShapeDtypeStruct((n, 1, ROW_W), F32),
        ],
        scratch_shapes=[pltpu.VMEM((n, D), BF16), pltpu.VMEM((n, LANE), F32),
                        pltpu.VMEM((chunks, n, PREP_ROWS), F32)],
        compiler_params=pltpu.CompilerParams(
            dimension_semantics=("arbitrary",), vmem_limit_bytes=VMEM_LIMIT),
        name="prep_weights",
    )(x, nmix, w_f32, w_f32, wup, bg, cos, sin)


def _rope_tables(pos):
    half = RET_DK // 2
    inv = 1.0 / (ROPE_BASE ** jnp.linspace(0.0, 1.0, half, dtype=jnp.float32))
    ang = pos[:, None] * inv[None, :]
    return jnp.cos(ang), jnp.sin(ang)


def kernel(x_prompt, x_sample, state_gla, state_ret, p_prompt, p_sample, norm_mix, w_in, w_gla_up, b_gla,
           gla_norm, w_out, norm_ple, w_ple_gate, w_ple_proj, norm_final):
    depth = w_in.shape[0]
    Bp, Lp, D = x_prompt.shape
    Bs, Ls, _ = x_sample.shape
    assert Ls == 1 and Lp % T_BLK == 0
    cos_p, sin_p = _rope_tables(jnp.arange(Lp, dtype=jnp.float32))
    cos_s, sin_s = _rope_tables(PAST_LEN + jnp.arange(Ls, dtype=jnp.float32))
    log_gamma = jnp.log(1.0 - jnp.exp2(-5.0 - jnp.arange(RET_HEADS, dtype=jnp.float32)))
    nfin = norm_final.reshape(1, D)

    hp = x_prompt
    hs = x_sample
    gla_p, ret_p, gla_s, ret_s = [], [], [], []
    for i in range(depth):
        last = i == depth - 1
        nmix = norm_mix[i].reshape(1, D)
        nple = norm_ple[i].reshape(1, D)
        gn = gla_norm[i].reshape(1, GLA_DV)
        w_f32 = w_in[i].T
        wup = jnp.pad(w_gla_up[i], ((0, LANE - GLA_RANK), (0, 0))).astype(BF16)
        wup = wup.reshape(LANE, GLA_HEADS, GLA_DK).transpose(1, 0, 2)
        bg = b_gla[i].reshape(GLA_HEADS, 1, GLA_DK)
        w_pp = w_ple_proj[i].astype(BF16)

        w_t, z_raw, rows = _prep_weights(hs, nmix, w_f32, wup, bg, cos_s, sin_s)
        u, la = _norm_prompt(hp.reshape(Bp * Lp, D), nmix, w_f32, wup, bg)
        merged, sg, sr, nsg, nsr, o_s, w_o, w_pg = _mix_prompt(
            u.reshape(Bp, Lp, D), la.reshape(N_GROUPS, Bp, Lp, GLA_DK), log_gamma, w_t, gn, cos_p, sin_p,
            state_gla[i], state_ret[i], rows, [w_out[i], w_ple_gate[i]])
        hp, hs = _out_proj(hp.reshape(Bp * Lp, D), merged.reshape(Bp * Lp, D),
                           p_prompt[i].reshape(Bp * Lp, PLE_DIM), hs, o_s, p_sample[i], z_raw, gn,
                           w_o, nple, w_pg, w_pp, nfin, last)
        hp = hp.reshape(Bp, Lp, D)
        gla_p.append(sg)
        ret_p.append(sr)
        gla_s.append(nsg)
        ret_s.append(nsr)

    return (hp, hs, jnp.stack(gla_p), jnp.stack(ret_p), jnp.stack(gla_s), jnp.stack(ret_s))
```
